```python
import math
import jax, jax.numpy as jnp
from jax import lax
import numpy as np

D_MODEL = 1024
BATCH = 8
SEQ = 2048
DEPTH = 2

SSM_WIDTH = D_MODEL
MLSTM_WIDTH = D_MODEL
MIX_WIDTH = SSM_WIDTH + MLSTM_WIDTH
SSM_GROUP = 16
SSM_GROUPS = SSM_WIDTH // SSM_GROUP
SSM_STATE = 64
MLSTM_HEADS = 4
MLSTM_HEAD_DIM = MLSTM_WIDTH // MLSTM_HEADS
CONV_WIDTH = 4
MLSTM_CHUNK = 64
IN_COLS = 2 * SSM_WIDTH + 3 * MLSTM_WIDTH
EPS = 1e-6

kernel_name = "hybrid_s5_mlstm_parallel_heads"


def rms_norm(x, gain):
    xf = x.astype(jnp.float32)
    y = xf * lax.rsqrt(jnp.mean(xf * xf, axis=-1, keepdims=True) + EPS)
    return y * gain.astype(jnp.float32)


def s5_branch(u, lam_re, lam_im, log_dt, b_re, b_im, c_re, c_im, d_skip, w_glu, b_glu):
    bsz, seq, _ = u.shape
    f32 = jnp.float32
    uf = u.astype(f32).reshape(bsz, seq, SSM_GROUPS, SSM_GROUP)
    lam = lax.complex(lam_re.astype(f32), lam_im.astype(f32))
    dt = jnp.exp(log_dt.astype(f32))[:, None]
    a_bar = jnp.exp(lam * dt)
    b_cplx = lax.complex(b_re.astype(f32), b_im.astype(f32))
    b_bar = ((a_bar - 1.0) / lam)[..., None] * b_cplx
    bu = jnp.einsum('bsgc,gpc->bsgp', uf.astype(jnp.complex64), b_bar)
    a_seq = jnp.broadcast_to(a_bar, (1, seq) + a_bar.shape)

    def combine(left, right):
        a_l, b_l = left
        a_r, b_r = right
        return a_r * a_l, a_r * b_l + b_r

    _, states = lax.associative_scan(combine, (a_seq, bu), axis=1)
    c_cplx = lax.complex(c_re.astype(f32), c_im.astype(f32))
    y = jnp.real(jnp.einsum('bsgp,gcp->bsgc', states, c_cplx))
    y = y + d_skip.astype(f32).reshape(SSM_GROUPS, SSM_GROUP) * uf
    y = jax.nn.gelu(y.reshape(bsz, seq, SSM_WIDTH))
    return y * jax.nn.sigmoid(y @ w_glu.astype(f32) + b_glu.astype(f32))


def mlstm_cell(q, k, v, i_pre, log_f):
    bsz, nh, seq, dh = q.shape
    L = MLSTM_CHUNK
    nc = seq // L

    def to_chunks(t):
        t = t.reshape((bsz, nh, nc, L) + t.shape[3:])
        return jnp.moveaxis(t, 2, 0)

    causal = jnp.tril(jnp.ones((L, L), dtype=bool))

    def step(carry, inp):
        c_mat, n_vec, m_prev = carry
        q_, k_, v_, i_, lf = inp
        b = jnp.cumsum(lf, axis=-1)
        b_tot = b[..., -1]
        log_d = jnp.where(causal, b[..., :, None] - b[..., None, :] + i_[..., None, :], -jnp.inf)
        m_inter = b + m_prev[..., None]
        m_t = jnp.maximum(m_inter, jnp.max(log_d, axis=-1))
        w_inter = jnp.exp(m_inter - m_t)
        s = jnp.einsum('bhld,bhsd->bhls', q_, k_) * jnp.exp(log_d - m_t[..., None])
        num = (w_inter[..., None] * jnp.einsum('bhvd,bhld->bhlv', c_mat, q_)
               + jnp.einsum('bhls,bhsv->bhlv', s, v_))
        den = w_inter * jnp.einsum('bhd,bhld->bhl', n_vec, q_) + jnp.sum(s, axis=-1)
        h = num / jnp.maximum(jnp.abs(den), jnp.exp(-m_t))[..., None]
        log_w = b_tot[..., None] - b + i_
        m_next = jnp.maximum(b_tot + m_prev, jnp.max(log_w, axis=-1))
        decay = jnp.exp(b_tot + m_prev - m_next)
        w = jnp.exp(log_w - m_next[..., None])
        c_mat = decay[..., None, None] * c_mat + jnp.einsum('bhs,bhsv,bhsd->bhvd', w, v_, k_)
        n_vec = decay[..., None] * n_vec + jnp.einsum('bhs,bhsd->bhd', w, k_)
        return (c_mat, n_vec, m_next), h

    init = (jnp.zeros((bsz, nh, dh, dh), jnp.float32),
            jnp.zeros((bsz, nh, dh), jnp.float32),
            jnp.zeros((bsz, nh), jnp.float32))
    _, hs = lax.scan(step, init, (to_chunks(q), to_chunks(k), to_chunks(v),
                                  to_chunks(i_pre), to_chunks(log_f)))
    return jnp.moveaxis(hs, 0, 2).reshape(bsz, nh, seq, dh)


def mlstm_branch(xm, o_pre, conv_w, conv_b, wq, wk, wv, w_gates, b_igate, b_fgate, norm_gain, skip):
    bsz, seq, _ = xm.shape
    f32 = jnp.float32
    H, dh = MLSTM_HEADS, MLSTM_HEAD_DIM
    xc = lax.conv_general_dilated(
        xm, conv_w[:, None, :].astype(xm.dtype), window_strides=(1,),
        padding=[(CONV_WIDTH - 1, 0)], dimension_numbers=('NWC', 'WIO', 'NWC'),
        feature_group_count=MLSTM_WIDTH)
    xc = jax.nn.silu(xc.astype(f32) + conv_b.astype(f32))
    xmf = xm.astype(f32)
    q = jnp.einsum('bshd,hde->bshe', xc.reshape(bsz, seq, H, dh), wq.astype(f32))
    k = jnp.einsum('bshd,hde->bshe', xc.reshape(bsz, seq, H, dh), wk.astype(f32)) * (dh ** -0.5)
    v = jnp.einsum('bshd,hde->bshe', xmf.reshape(bsz, seq, H, dh), wv.astype(f32))
    qkv = jnp.concatenate([q.reshape(bsz, seq, -1), k.reshape(bsz, seq, -1),
                           v.reshape(bsz, seq, -1)], axis=-1)
    gates = qkv @ w_gates.astype(f32)
    i_pre = gates[..., :H] + b_igate.astype(f32)
    log_f = jax.nn.log_sigmoid(gates[..., H:] + b_fgate.astype(f32))
    h = mlstm_cell(q.transpose(0, 2, 1, 3), k.transpose(0, 2, 1, 3), v.transpose(0, 2, 1, 3),
                   i_pre.transpose(0, 2, 1), log_f.transpose(0, 2, 1))
    h = h.transpose(0, 2, 1, 3) * jax.nn.sigmoid(o_pre.astype(f32)).reshape(bsz, seq, H, dh)
    mu = jnp.mean(h, axis=-1, keepdims=True)
    var = jnp.mean(jnp.square(h - mu), axis=-1, keepdims=True)
    hn = ((h - mu) * lax.rsqrt(var + EPS)).reshape(bsz, seq, MLSTM_WIDTH) * norm_gain.astype(f32)
    return hn + skip.astype(f32) * xc


def hybrid_layer(x, cond, norm_gain, w_mod, b_mod, w_in,
                 ssm_lambda_re, ssm_lambda_im, ssm_log_dt, ssm_b_re, ssm_b_im, ssm_c_re, ssm_c_im,
                 ssm_d, ssm_w_glu, ssm_b_glu, ssm_out_gain,
                 m_conv_w, m_conv_b, m_wq, m_wk, m_wv, m_w_gates, m_b_igate, m_b_fgate,
                 m_norm_gain, m_skip, w_out):
    mod = jax.nn.silu(cond) @ w_mod + b_mod
    shift, scale, gate = jnp.split(mod.astype(jnp.float32), 3, axis=-1)
    h = (rms_norm(x, norm_gain) * (1.0 + scale[:, None, :]) + shift[:, None, :]).astype(x.dtype)
    proj = h @ w_in
    w = SSM_WIDTH
    ssm_in, ssm_gate, m_in, m_o, m_gate = jnp.split(
        proj, [w, 2 * w, 2 * w + MLSTM_WIDTH, 2 * w + 2 * MLSTM_WIDTH], axis=-1)
    ssm_y = s5_branch(ssm_in, ssm_lambda_re, ssm_lambda_im, ssm_log_dt, ssm_b_re, ssm_b_im,
                      ssm_c_re, ssm_c_im, ssm_d, ssm_w_glu, ssm_b_glu)
    ssm_y = rms_norm(ssm_y, ssm_out_gain) * jax.nn.silu(ssm_gate.astype(jnp.float32))
    m_y = mlstm_branch(m_in, m_o, m_conv_w, m_conv_b, m_wq, m_wk, m_wv, m_w_gates,
                       m_b_igate, m_b_fgate, m_norm_gain, m_skip)
    m_y = m_y * jax.nn.silu(m_gate.astype(jnp.float32))
    mixed = jnp.concatenate([ssm_y, m_y], axis=-1).astype(x.dtype)
    out = mixed @ w_out
    return (x.astype(jnp.float32) + gate[:, None, :] * out.astype(jnp.float32)).astype(x.dtype)


def setup_inputs(seed: int = 0) -> dict:
    key = jax.random.key(seed)
    ks = list(jax.random.split(key, 32))
    ctr = [0]

    def nk():
        ctr[0] += 1
        return ks[ctr[0] - 1]

    def nrm(shape, scale):
        return jax.random.normal(nk(), shape, jnp.float32) * scale

    L, D, W, MW = DEPTH, D_MODEL, SSM_WIDTH, MLSTM_WIDTH
    G, P, Cg = SSM_GROUPS, SSM_STATE, SSM_GROUP
    H, dh = MLSTM_HEADS, MLSTM_HEAD_DIM
    n_idx = jnp.arange(P, dtype=jnp.float32)[None, None, :]
    return {
        "x": nrm((BATCH, SEQ, D), 1.0),
        "c": nrm((BATCH, D), 1.0),
        "norm_gain": 1.0 + nrm((L, D), 0.02),
        "w_mod": nrm((L, D, 3 * D), 0.5 * D ** -0.5),
        "b_mod": nrm((L, 3 * D), 0.02),
        "w_in": nrm((L, D, IN_COLS), D ** -0.5),
        "ssm_lambda_re": -0.5 + nrm((L, G, P), 0.01),
        "ssm_lambda_im": math.pi * n_idx + nrm((L, G, P), 0.01),
        "ssm_log_dt": jax.random.uniform(nk(), (L, G), jnp.float32,
                                         minval=math.log(1e-3), maxval=math.log(1e-1)),
        "ssm_b_re": nrm((L, G, P, Cg), (2 * Cg) ** -0.5),
        "ssm_b_im": nrm((L, G, P, Cg), (2 * Cg) ** -0.5),
        "ssm_c_re": nrm((L, G, Cg, P), 0.5),
        "ssm_c_im": nrm((L, G, Cg, P), 0.5),
        "ssm_d": nrm((L, W), 1.0),
        "ssm_w_glu": nrm((L, W, W), W ** -0.5),
        "ssm_b_glu": nrm((L, W), 0.02),
        "ssm_out_gain": 1.0 + nrm((L, W), 0.02),
        "m_conv_w": nrm((L, CONV_WIDTH, MW), CONV_WIDTH ** -0.5),
        "m_conv_b": nrm((L, MW), 0.02),
        "m_wq": nrm((L, H, dh, dh), dh ** -0.5),
        "m_wk": nrm((L, H, dh, dh), dh ** -0.5),
        "m_wv": nrm((L, H, dh, dh), dh ** -0.5),
        "m_w_gates": nrm((L, 3 * MW, 2 * H), 0.1 * (3 * MW) ** -0.5),
        "m_b_igate": nrm((L, H), 0.1),
        "m_b_fgate": jnp.linspace(3.0, 6.0, H, dtype=jnp.float32)[None, :] + nrm((L, H), 0.1),
        "m_norm_gain": 1.0 + nrm((L, MW), 0.02),
        "m_skip": 1.0 + nrm((L, MW), 0.02),
        "w_out": nrm((L, MIX_WIDTH, D), MIX_WIDTH ** -0.5),
        "final_gain": 1.0 + nrm((D,), 0.02),
    }


def reference(x, c, norm_gain, w_mod, b_mod, w_in,
              ssm_lambda_re, ssm_lambda_im, ssm_log_dt, ssm_b_re, ssm_b_im, ssm_c_re, ssm_c_im,
              ssm_d, ssm_w_glu, ssm_b_glu, ssm_out_gain,
              m_conv_w, m_conv_b, m_wq, m_wk, m_wv, m_w_gates, m_b_igate, m_b_fgate,
              m_norm_gain, m_skip, w_out, final_gain):
    h = x
    for l in range(DEPTH):
        h = hybrid_layer(h, c, norm_gain[l], w_mod[l], b_mod[l], w_in[l],
                         ssm_lambda_re[l], ssm_lambda_im[l], ssm_log_dt[l], ssm_b_re[l], ssm_b_im[l],
                         ssm_c_re[l], ssm_c_im[l], ssm_d[l], ssm_w_glu[l], ssm_b_glu[l], ssm_out_gain[l],
                         m_conv_w[l], m_conv_b[l], m_wq[l], m_wk[l], m_wv[l], m_w_gates[l],
                         m_b_igate[l], m_b_fgate[l], m_norm_gain[l], m_skip[l], w_out[l])
    return rms_norm(h, final_gain).astype(x.dtype)
```

```python
import functools
import math

import jax
import jax.numpy as jnp
from jax import lax
from jax.experimental import pallas as pl
from jax.experimental.pallas import tpu as pltpu

F32 = jnp.float32
BF16 = jnp.bfloat16

D_MODEL = 1024
SSM_GROUP = 16
SSM_GROUPS = D_MODEL // SSM_GROUP
SSM_STATE = 64
HEADS = 4
HEAD_DIM = D_MODEL // HEADS
CONV_WIDTH = 4
EPS = 1e-6

S5_SUB = 16
S5_ROW = S5_SUB * SSM_GROUP
S5_GROUPS_PER_STEP = 4
M_CHUNK = 256
M_AUG = HEAD_DIM + 128
VMEM_LIMIT = 56 * 1024 * 1024


def _dot(a, b):
    return jnp.dot(a, b, preferred_element_type=F32)


def _sigmoid(x):
    return jax.nn.sigmoid(x)


def _rms(x):
    return x * lax.rsqrt(jnp.mean(x * x, axis=-1, keepdims=True) + EPS)


def _modulated_norm(x, gain, mod):
    shift = mod[:, 0:D_MODEL]
    scale = mod[:, D_MODEL:2 * D_MODEL]
    return _rms(x) * gain * (1.0 + scale) + shift


def _split_hi_lo(v):
    hi = v.astype(BF16)
    lo = (v - hi.astype(F32)).astype(BF16)
    return hi, lo


def _mod_kernel(c_ref, w_ref, b_ref, o_ref):
    cv = c_ref[...]
    act = cv * _sigmoid(cv)
    o_ref[...] = _dot(act.astype(BF16), w_ref[...].astype(BF16)) + b_ref[...]


def _modulation(c, w_mod, b_mod):
    depth, d, n = w_mod.shape
    bsz = c.shape[0]
    nb = n // d
    return pl.pallas_call(
        _mod_kernel,
        out_shape=jax.ShapeDtypeStruct((depth, bsz, n), F32),
        grid=(depth, nb),
        in_specs=[
            pl.BlockSpec((bsz, d), lambda l, j: (0, 0)),
            pl.BlockSpec((None, d, d), lambda l, j: (l, 0, j)),
            pl.BlockSpec((None, 1, d), lambda l, j: (l, 0, j)),
        ],
        out_specs=pl.BlockSpec((None, bsz, d), lambda l, j: (l, 0, j)),
        compiler_params=pltpu.CompilerParams(
            dimension_semantics=("parallel", "parallel"),
            vmem_limit_bytes=VMEM_LIMIT),
        name="adaln_mod",
    )(c, w_mod, b_mod.reshape(depth, 1, n))


def _s5_inproj_kernel(x_ref, mod_ref, gain_ref, w_ref, u_ref, g_ref):
    h = _modulated_norm(x_ref[...], gain_ref[...], mod_ref[...]).astype(BF16)
    u_ref[...] = _dot(h, w_ref[:, 0:D_MODEL])
    g_ref[...] = _dot(h, w_ref[:, D_MODEL:2 * D_MODEL])


def _s5_inproj(xf, mod_l, gain, w_ssm, seq):
    t, d = xf.shape
    tm = 512
    per_b = seq // tm
    return pl.pallas_call(
        _s5_inproj_kernel,
        out_shape=(jax.ShapeDtypeStruct((t, d), F32),
                   jax.ShapeDtypeStruct((t, d), F32)),
        grid=(t // tm,),
        in_specs=[
            pl.BlockSpec((tm, d), lambda i: (i, 0)),
            pl.BlockSpec((None, 1, 3 * d), lambda i: (i // per_b, 0, 0)),
            pl.BlockSpec((1, d), lambda i: (0, 0)),
            pl.BlockSpec((d, 2 * d), lambda i: (0, 0)),
        ],
        out_specs=(pl.BlockSpec((tm, d), lambda i: (i, 0)),
                   pl.BlockSpec((tm, d), lambda i: (i, 0))),
        compiler_params=pltpu.CompilerParams(
            dimension_semantics=("parallel",),
            vmem_limit_bytes=VMEM_LIMIT),
        name="s5_inproj",
    )(xf, mod_l, gain, w_ssm)


def _s5_core_kernel(z_ref, toep_ref, ws_ref, wo_ref, a_ref, y_ref, loc_ref, sprev_ref,
                    *, n_sub):
    gs = S5_GROUPS_PER_STEP
    for j in range(gs):
        loc_ref[j] = _dot(z_ref[j], ws_ref[j])

    ar = [a_ref[j, 0:8, :] for j in range(gs)]
    ai = [a_ref[j, 8:16, :] for j in range(gs)]

    def step(k, carry):
        r0 = pl.multiple_of(k * 8, 8)
        new = []
        for j in range(gs):
            s_re, s_im = carry[j]
            sprev_ref[j, pl.ds(r0, 8), 0:128] = s_re
            sprev_ref[j, pl.ds(r0, 8), 128:256] = s_im
            l_re = loc_ref[j, pl.ds(r0, 8), 0:128]
            l_im = loc_ref[j, pl.ds(r0, 8), 128:256]
            new.append((s_re * ar[j] - s_im * ai[j] + l_re,
                        s_im * ar[j] + s_re * ai[j] + l_im))
        return tuple(new)

    zero = jnp.zeros((8, 128), F32)
    lax.fori_loop(0, n_sub, step, tuple((zero, zero) for _ in range(gs)))

    for j in range(gs):
        y_ref[j] = (_dot(z_ref[j], toep_ref[j])
                    + _dot(sprev_ref[j].astype(BF16), wo_ref[j]))


def _s5_core(z, toep, ws, wo, a16, n_sub):
    g, rows, lanes = z.shape
    gs = S5_GROUPS_PER_STEP
    wspec = pl.BlockSpec((gs, lanes, lanes), lambda i: (i, 0, 0))
    return pl.pallas_call(
        functools.partial(_s5_core_kernel, n_sub=n_sub),
        out_shape=jax.ShapeDtypeStruct((g, rows, lanes), F32),
        grid=(g // gs,),
        in_specs=[
            pl.BlockSpec((gs, rows, lanes), lambda i: (i, 0, 0)),
            wspec, wspec, wspec,
            pl.BlockSpec((gs, 16, 128), lambda i: (i, 0, 0)),
        ],
        out_specs=pl.BlockSpec((gs, rows, lanes), lambda i: (i, 0, 0)),
        scratch_shapes=[pltpu.VMEM((gs, rows, lanes), F32),
                        pltpu.VMEM((gs, rows, lanes), F32)],
        compiler_params=pltpu.CompilerParams(
            dimension_semantics=("parallel",),
            vmem_limit_bytes=VMEM_LIMIT),
        name="s5_core",
    )(z, toep, ws, wo, a16)


def _s5_weights(lam_re, lam_im, log_dt, b_re, b_im, c_re, c_im):
    g, p, cg, lb = SSM_GROUPS, SSM_STATE, SSM_GROUP, S5_SUB
    hp = lax.Precision.HIGHEST
    lam = lax.complex(lam_re, lam_im)
    lam_dt = lam * jnp.exp(log_dt)[:, None]
    a_bar = jnp.exp(lam_dt)
    b_bar = ((a_bar - 1.0) / lam)[..., None] * lax.complex(b_re, b_im)
    c_cplx = lax.complex(c_re, c_im)
    steps = jnp.arange(lb + 1, dtype=lam_re.dtype)
    a_pow = jnp.exp(lam_dt[:, None, :] * steps[None, :, None])

    kker = jnp.real(jnp.einsum('gcp,gjp,gpd->gjcd', c_cplx, a_pow[:, :lb], b_bar,
                               precision=hp))
    l_idx = jnp.arange(lb)
    lag = l_idx[None, :] - l_idx[:, None]
    t5 = kker[:, jnp.maximum(lag, 0)]
    t5 = jnp.where((lag >= 0)[None, :, :, None, None], t5, 0.0)
    toep = t5.transpose(0, 1, 4, 2, 3).reshape(g, lb * cg, lb * cg)

    ws_c = a_pow[:, lb - 1 - l_idx][:, :, :, None] * b_bar[:, None]
    ws_c = ws_c.transpose(0, 1, 3, 2).reshape(g, lb * cg, p)
    zc = jnp.zeros((g, lb * cg, 128 - p), lam_re.dtype)
    ws = jnp.concatenate([jnp.real(ws_c), zc, jnp.imag(ws_c), zc], axis=-1)

    wo_c = c_cplx[:, None] * a_pow[:, 1:lb + 1][:, :, None, :]
    wo_c = wo_c.transpose(0, 3, 1, 2).reshape(g, p, lb * cg)
    zr = jnp.zeros((g, 128 - p, lb * cg), lam_re.dtype)
    wo = jnp.concatenate([jnp.real(wo_c), zr, -jnp.imag(wo_c), zr], axis=1)

    a_blk = a_pow[:, lb]
    zp = jnp.zeros((g, 128 - p), lam_re.dtype)
    a_re = jnp.concatenate([jnp.real(a_blk), zp], axis=-1)
    a_im = jnp.concatenate([jnp.imag(a_blk), zp], axis=-1)
    a16 = jnp.concatenate([jnp.broadcast_to(a_re[:, None, :], (g, 8, 128)),
                           jnp.broadcast_to(a_im[:, None, :], (g, 8, 128))], axis=1)
    return toep.astype(BF16), ws.astype(BF16), wo.astype(BF16), a16.astype(F32)


def _log_sigmoid(x):
    return -(jnp.maximum(-x, 0.0) + jnp.log1p(jnp.exp(-jnp.abs(x))))


def _mlstm_kernel(x_ref, mod_ref, gain_ref, w3_ref, cw_ref, cb_ref,
                  wq_ref, wk_ref, wv_ref, wg_ref, wgt_ref, bg_ref, bgt_ref,
                  ng_ref, skip_ref, o_ref,
                  ct_ref, m_ref, tail_ref, ext_ref, qkv_ref):
    L, dh, d = M_CHUNK, HEAD_DIM, D_MODEL

    @pl.when(pl.program_id(1) == 0)
    def _():
        ct_ref[...] = jnp.zeros_like(ct_ref)
        m_ref[...] = jnp.zeros_like(m_ref)
        tail_ref[...] = jnp.zeros_like(tail_ref)

    h = _modulated_norm(x_ref[...], gain_ref[...], mod_ref[...]).astype(BF16)
    proj = _dot(h, w3_ref[...])
    m_in = proj[:, 0:d]
    m_o = proj[:, d:2 * d]
    m_g = proj[:, 2 * d:3 * d]

    ext_ref[0:8, :] = tail_ref[...]
    ext_ref[8:8 + L, :] = m_in
    tail_ref[...] = m_in[L - 8:L, :]
    acc = m_in * cw_ref[CONV_WIDTH - 1:CONV_WIDTH, :] + cb_ref[...]
    for j in range(CONV_WIDTH - 1):
        back = CONV_WIDTH - 1 - j
        acc = acc + ext_ref[pl.ds(8 - back, L), :] * cw_ref[j:j + 1, :]
    xc = acc * _sigmoid(acc)

    xcb = xc.astype(BF16)
    minb = m_in.astype(BF16)
    qs, ks, vs = [], [], []
    for hd in range(HEADS):
        sl = slice(hd * dh, (hd + 1) * dh)
        q = _dot(xcb[:, sl], wq_ref[hd])
        k = _dot(xcb[:, sl], wk_ref[hd]) * (dh ** -0.5)
        v = _dot(minb[:, sl], wv_ref[hd])
        qs.append(q)
        ks.append(k)
        vs.append(v)
        qkv_ref[:, hd * dh:(hd + 1) * dh] = q.astype(BF16)
        qkv_ref[:, d + hd * dh:d + (hd + 1) * dh] = k.astype(BF16)
        qkv_ref[:, 2 * d + hd * dh:2 * d + (hd + 1) * dh] = v.astype(BF16)

    qkv = qkv_ref[...]
    gates = _dot(qkv, wg_ref[...]) + bg_ref[...]
    gates_t = lax.dot_general(wgt_ref[...], qkv, (((1,), (1,)), ((), ())),
                              preferred_element_type=F32) + bgt_ref[...]

    rows = lax.broadcasted_iota(jnp.int32, (L, L), 0)
    cols = lax.broadcasted_iota(jnp.int32, (L, L), 1)
    causal = cols <= rows
    tri_l = jnp.where(causal, 1.0, 0.0).astype(BF16)
    tri_u = jnp.where(rows <= cols, 1.0, 0.0).astype(BF16)

    lf_hi, lf_lo = _split_hi_lo(_log_sigmoid(gates))
    b_cols = _dot(tri_l, lf_hi) + _dot(tri_l, lf_lo)
    lft_hi, lft_lo = _split_hi_lo(_log_sigmoid(gates_t))
    b_rows = _dot(lft_hi, tri_u) + _dot(lft_lo, tri_u)

    lane = lax.broadcasted_iota(jnp.int32, (L, 128), 1)
    ones_blk = jnp.where(lane == 0, 1.0, 0.0)

    for hd in range(HEADS):
        sl = slice(hd * dh, (hd + 1) * dh)
        i_r = gates_t[hd:hd + 1, :]
        b_r = b_rows[HEADS + hd:HEADS + hd + 1, :]
        i_c = gates[:, hd:hd + 1]
        b_c = b_cols[:, HEADS + hd:HEADS + hd + 1]
        m_prev = m_ref[hd:hd + 1, 0:1]

        log_d = jnp.where(causal, b_c + (i_r - b_r), -jnp.inf)
        m_inter = b_c + m_prev
        m_t = jnp.maximum(m_inter, jnp.max(log_d, axis=-1, keepdims=True))
        dmat = jnp.exp(log_d - m_t)
        qb = qs[hd].astype(BF16)
        kb = ks[hd].astype(BF16)
        s = lax.dot_general(qb, kb, (((1,), (1,)), ((), ())),
                            preferred_element_type=F32)
        p = (s * dmat).astype(BF16)
        v_aug = jnp.concatenate([vs[hd], ones_blk], axis=-1)
        w_inter = jnp.exp(m_inter - m_t)
        inter = _dot(qb, ct_ref[hd].astype(BF16))
        intra = _dot(p, v_aug.astype(BF16))
        nd = w_inter * inter + intra
        num = nd[:, 0:dh]
        den = nd[:, dh:dh + 1]
        hh = num / jnp.maximum(jnp.abs(den), jnp.exp(-m_t))

        b_tot = b_c[L - 1:L, :]
        log_w = b_tot - b_c + i_c
        m_next = jnp.maximum(b_tot + m_prev, jnp.max(log_w, axis=0, keepdims=True))
        decay = jnp.exp(b_tot + m_prev - m_next)
        w_c = jnp.exp(log_w - m_next)
        upd = _dot(ks[hd].T.astype(BF16), (w_c * v_aug).astype(BF16))
        ct_ref[hd] = decay * ct_ref[hd] + upd
        m_ref[hd:hd + 1, :] = jnp.broadcast_to(m_next, (1, 128))

        hg = hh * _sigmoid(m_o[:, sl])
        mu = jnp.mean(hg, axis=-1, keepdims=True)
        dev = hg - mu
        var = jnp.mean(dev * dev, axis=-1, keepdims=True)
        hn = dev * lax.rsqrt(var + EPS) * ng_ref[:, sl] + skip_ref[:, sl] * xc[:, sl]
        gate = m_g[:, sl]
        o_ref[:, sl] = hn * (gate * _sigmoid(gate))


def _mlstm(xf, mod_l, gain, w3, cw, cb, wq, wk, wv, wg, wgt, bg, bgt, ng, skip, bsz, seq):
    t, d = xf.shape
    L = M_CHUNK
    nc = seq // L
    full2 = lambda shape: pl.BlockSpec(shape, lambda b, c: (0, 0))
    full3 = lambda shape: pl.BlockSpec(shape, lambda b, c: (0, 0, 0))
    return pl.pallas_call(
        _mlstm_kernel,
        out_shape=jax.ShapeDtypeStruct((t, d), F32),
        grid=(bsz, nc),
        in_specs=[
            pl.BlockSpec((L, d), lambda b, c: (b * nc + c, 0)),
            pl.BlockSpec((None, 1, 3 * d), lambda b, c: (b, 0, 0)),
            full2((1, d)),
            full2((d, 3 * d)),
            full2((CONV_WIDTH, d)),
            full2((1, d)),
            full3((HEADS, HEAD_DIM, HEAD_DIM)),
            full3((HEADS, HEAD_DIM, HEAD_DIM)),
            full3((HEADS, HEAD_DIM, HEAD_DIM)),
            full2((3 * d, 128)),
            full2((8, 3 * d)),
            full2((1, 128)),
            full2((8, 1)),
            full2((1, d)),
            full2((1, d)),
        ],
        out_specs=pl.BlockSpec((L, d), lambda b, c: (b * nc + c, 0)),
        scratch_shapes=[
            pltpu.VMEM((HEADS, HEAD_DIM, M_AUG), F32),
            pltpu.VMEM((8, 128), F32),
            pltpu.VMEM((8, d), F32),
            pltpu.VMEM((L + 8, d), F32),
            pltpu.VMEM((L, 3 * d), BF16),
        ],
        compiler_params=pltpu.CompilerParams(
            dimension_semantics=("parallel", "arbitrary"),
            vmem_limit_bytes=VMEM_LIMIT),
        name="mlstm",
    )(xf, mod_l, gain, w3, cw, cb, wq, wk, wv, wg, wgt, bg, bgt, ng, skip)


def _gelu_tanh(x):
    inner = math.sqrt(2.0 / math.pi) * (x + 0.044715 * (x * x * x))
    return x * (0.5 * (1.0 + jnp.tanh(inner)))


def _out_kernel(ys_ref, u_ref, sg_ref, my_ref, x_ref, mod_ref, d_ref, bglu_ref,
                og_ref, fg_ref, wglu_ref, wout_ref, o_ref, *, final):
    d = D_MODEL
    y = _gelu_tanh(ys_ref[...] + d_ref[...] * u_ref[...])
    glu = y * _sigmoid(_dot(y.astype(BF16), wglu_ref[...]) + bglu_ref[...])
    sg = sg_ref[...]
    ssm_y = _rms(glu) * og_ref[...] * (sg * _sigmoid(sg))
    out = (_dot(ssm_y.astype(BF16), wout_ref[0:d, :])
           + _dot(my_ref[...].astype(BF16), wout_ref[d:2 * d, :]))
    gate = mod_ref[:, 2 * d:3 * d]
    xn = x_ref[...] + gate * out
    if final:
        xn = _rms(xn) * fg_ref[...]
    o_ref[...] = xn


def _out_stage(ys, u, sg, my, xf, mod_l, dskip, bglu, og, fg, wglu, wout, seq, final):
    t, d = xf.shape
    tm = 256
    per_b = seq // tm
    row = pl.BlockSpec((tm, d), lambda i: (i, 0))
    vec = pl.BlockSpec((1, d), lambda i: (0, 0))
    return pl.pallas_call(
        functools.partial(_out_kernel, final=final),
        out_shape=jax.ShapeDtypeStruct((t, d), F32),
        grid=(t // tm,),
        in_specs=[row, row, row, row, row,
                  pl.BlockSpec((None, 1, 3 * d), lambda i: (i // per_b, 0, 0)),
                  vec, vec, vec, vec,
                  pl.BlockSpec((d, d), lambda i: (0, 0)),
                  pl.BlockSpec((2 * d, d), lambda i: (0, 0))],
        out_specs=row,
        compiler_params=pltpu.CompilerParams(
            dimension_semantics=("parallel",),
            vmem_limit_bytes=VMEM_LIMIT),
        name="out_stage",
    )(ys, u, sg, my, xf, mod_l, dskip, bglu, og, fg, wglu, wout)


def kernel(x, c, norm_gain, w_mod, b_mod, w_in, ssm_lambda_re, ssm_lambda_im, ssm_log_dt,
           ssm_b_re, ssm_b_im, ssm_c_re, ssm_c_im, ssm_d, ssm_w_glu, ssm_b_glu,
           ssm_out_gain, m_conv_w, m_conv_b, m_wq, m_wk, m_wv, m_w_gates, m_b_igate,
           m_b_fgate, m_norm_gain, m_skip, w_out, final_gain):
    bsz, seq, d = x.shape
    depth = w_in.shape[0]
    t = bsz * seq
    g, cg, lb = SSM_GROUPS, SSM_GROUP, S5_SUB
    n_sub = seq // lb
    assert d == D_MODEL and seq % M_CHUNK == 0 and seq % 512 == 0

    xf = x.reshape(t, d)
    mod_all = _modulation(c, w_mod, b_mod)
    row = lambda v: v.reshape(1, -1)

    for l in range(depth):
        mod_l = mod_all[l].reshape(bsz, 1, 3 * d)
        w_in_b = w_in[l].astype(BF16)
        gain = row(norm_gain[l])

        u, sg = _s5_inproj(xf, mod_l, gain, w_in_b[:, 0:2 * d], seq)
        z = (u.reshape(bsz, n_sub, lb, g, cg).transpose(3, 1, 0, 2, 4)
             .reshape(g, n_sub * bsz, lb * cg).astype(BF16))
        toep, ws, wo, a16 = _s5_weights(ssm_lambda_re[l], ssm_lambda_im[l], ssm_log_dt[l],
                                        ssm_b_re[l], ssm_b_im[l], ssm_c_re[l], ssm_c_im[l])
        yz = _s5_core(z, toep, ws, wo, a16, n_sub)
        ys = (yz.reshape(g, n_sub, bsz, lb, cg).transpose(2, 1, 3, 0, 4)
              .reshape(t, d))

        wg = jnp.zeros((3 * d, 128), F32).at[:, 0:2 * HEADS].set(m_w_gates[l])
        bg = jnp.zeros((1, 128), F32).at[0, 0:HEADS].set(m_b_igate[l])
        bg = bg.at[0, HEADS:2 * HEADS].set(m_b_fgate[l])
        wgt = m_w_gates[l].T
        bgt = jnp.concatenate([m_b_igate[l], m_b_fgate[l]]).reshape(2 * HEADS, 1)
        my = _mlstm(xf, mod_l, gain, w_in_b[:, 2 * d:5 * d], m_conv_w[l], row(m_conv_b[l]),
                    m_wq[l].astype(BF16), m_wk[l].astype(BF16), m_wv[l].astype(BF16),
                    wg.astype(BF16), wgt.astype(BF16), bg, bgt,
                    row(m_norm_gain[l]), row(m_skip[l]), bsz, seq)

        xf = _out_stage(ys, u, sg, my, xf, mod_l, row(ssm_d[l]), row(ssm_b_glu[l]),
                        row(ssm_out_gain[l]), row(final_gain),
                        ssm_w_glu[l].astype(BF16), w_out[l].astype(BF16),
                        seq, final=(l == depth - 1))

    return xf.reshape(bsz, seq, d)
```

```python
import functools
import math

import numpy as np
import jax
import jax.numpy as jnp
from jax import lax
from jax.experimental import pallas as pl
from jax.experimental.pallas import tpu as pltpu

F32 = jnp.float32
BF16 = jnp.bfloat16

D_MODEL = 1024
SSM_GROUP = 16
SSM_GROUPS = D_MODEL // SSM_GROUP
SSM_STATE = 64
HEADS = 4
HEAD_DIM = D_MODEL // HEADS
CONV_WIDTH = 4
EPS = 1e-6

SUB = 16
SUBS_PER_SEG = 16
SEG_LEN = SUB * SUBS_PER_SEG
N_SEG = 8
SEQ = N_SEG * SEG_LEN
BLK_LANES = N_SEG * SUBS_PER_SEG
S5_ROW = SUB * SSM_GROUP
STATE_LANES = 256
S5_GROUPS_PER_STEP = 4
L_PAIR = 2
M_AUG = HEAD_DIM + 128
VMEM_LIMIT = 56 * 1024 * 1024

_NT = (((1,), (1,)), ((), ()))
_TN = (((0,), (0,)), ((), ()))


def _dot(a, b):
    return jnp.dot(a, b, preferred_element_type=F32)


def _dot_nt(a, b):
    return lax.dot_general(a, b, _NT, preferred_element_type=F32)


def _dot_tn(a, b):
    return lax.dot_general(a, b, _TN, preferred_element_type=F32)


def _sigmoid(x):
    return jax.nn.sigmoid(x)


def _rms(x):
    return x * lax.rsqrt(jnp.mean(x * x, axis=-1, keepdims=True) + EPS)


def _modulated_norm(x, gain, mod):
    shift = mod[:, 0:D_MODEL]
    scale = mod[:, D_MODEL:2 * D_MODEL]
    return _rms(x) * gain * (1.0 + scale) + shift


def _split_hi_lo(v):
    hi = v.astype(BF16)
    lo = (v - hi.astype(F32)).astype(BF16)
    return hi, lo


def _cmul_add(p_re, p_im, s_re, s_im, add_re, add_im):
    return (p_re * s_re - p_im * s_im + add_re,
            p_re * s_im + p_im * s_re + add_im)


def _mod_kernel(c_ref, w_ref, b_ref, o_ref):
    cv = c_ref[...]
    act = cv * _sigmoid(cv)
    o_ref[...] = _dot(act.astype(BF16), w_ref[...].astype(BF16)) + b_ref[...]


def _modulation(c, w_mod, b_mod):
    depth, d, n = w_mod.shape
    bsz = c.shape[0]
    return pl.pallas_call(
        _mod_kernel,
        out_shape=jax.ShapeDtypeStruct((depth, bsz, n), F32),
        grid=(depth, n // d),
        in_specs=[
            pl.BlockSpec((bsz, d), lambda l, j: (0, 0)),
            pl.BlockSpec((None, d, d), lambda l, j: (l, 0, j)),
            pl.BlockSpec((None, 1, d), lambda l, j: (l, 0, j)),
        ],
        out_specs=pl.BlockSpec((None, bsz, d), lambda l, j: (l, 0, j)),
        compiler_params=pltpu.CompilerParams(
            dimension_semantics=("parallel", "parallel"),
            vmem_limit_bytes=VMEM_LIMIT),
        name="adaln_mod",
    )(c, w_mod, b_mod.reshape(depth, 1, n))


def _s5_in_kernel(x_ref, mod_ref, gain_ref, wut_ref, wgt_ref, q_ref, zt_ref, gt_ref):
    d = D_MODEL
    x = x_ref[...].reshape(L_PAIR * BLK_LANES, d)
    h = _modulated_norm(x, gain_ref[...], mod_ref[...]).astype(BF16)
    ut = _dot_nt(wut_ref[...], h).astype(BF16)
    sgt = _dot_nt(wgt_ref[...], h)
    for l in range(L_PAIR):
        lanes = slice(l * BLK_LANES, (l + 1) * BLK_LANES)
        p = _dot(ut[:, lanes], q_ref[...])
        zt_ref[:, l * SSM_GROUP:(l + 1) * SSM_GROUP, :] = (
            p.reshape(SSM_GROUPS, SSM_GROUP, BLK_LANES).astype(BF16))
        g = sgt[:, lanes]
        gt_ref[l] = (g * _sigmoid(g)).astype(BF16)


def _s5_in(xp, mod_l, gain, wut, wgt, qperm):
    bsz = xp.shape[0]
    d = D_MODEL
    n_lp = SUB // L_PAIR
    return pl.pallas_call(
        _s5_in_kernel,
        out_shape=(jax.ShapeDtypeStruct((SSM_GROUPS, S5_ROW, bsz * BLK_LANES), BF16),
                   jax.ShapeDtypeStruct((bsz, SUB, d, BLK_LANES), BF16)),
        grid=(bsz, n_lp),
        in_specs=[
            pl.BlockSpec((None, L_PAIR, N_SEG, SUBS_PER_SEG, d), lambda b, p: (b, p, 0, 0, 0)),
            pl.BlockSpec((None, 1, 3 * d), lambda b, p: (b, 0, 0)),
            pl.BlockSpec((1, d), lambda b, p: (0, 0)),
            pl.BlockSpec((d, d), lambda b, p: (0, 0)),
            pl.BlockSpec((d, d), lambda b, p: (0, 0)),
            pl.BlockSpec((BLK_LANES, BLK_LANES), lambda b, p: (0, 0)),
        ],
        out_specs=(
            pl.BlockSpec((SSM_GROUPS, L_PAIR * SSM_GROUP, BLK_LANES), lambda b, p: (0, p, b)),
            pl.BlockSpec((None, L_PAIR, d, BLK_LANES), lambda b, p: (b, p, 0, 0)),
        ),
        compiler_params=pltpu.CompilerParams(
            dimension_semantics=("parallel", "parallel"),
            vmem_limit_bytes=VMEM_LIMIT),
        name="s5_in",
    )(xp, mod_l, gain, wut, wgt, qperm)


def _s5_core_kernel(zt_ref, toept_ref, ws_ref, wot_ref, atab_ref, seg_ref, dcol_ref,
                    yt_ref, loc_ref, sprev_ref, *, bsz):
    gs = S5_GROUPS_PER_STEP
    half = STATE_LANES // 2
    row8 = lax.broadcasted_iota(jnp.int32, (N_SEG, half), 0)

    def shift_down(v, n):
        return jnp.where(row8 >= n, pltpu.roll(v, n, axis=0), 0.0)

    for j in range(gs):
        loc_ref[j] = _dot_tn(zt_ref[j], ws_ref[j])
        a_re = atab_ref[j, 0:8, :]
        a_im = atab_ref[j, 8:16, :]
        for b in range(bsz):
            base = b * BLK_LANES

            def rows(kl, base=base):
                return pl.ds(base + kl * N_SEG, N_SEG)

            e_re = jnp.zeros((N_SEG, half), F32)
            e_im = jnp.zeros((N_SEG, half), F32)
            for kl in range(SUBS_PER_SEG):
                e_re, e_im = _cmul_add(a_re, a_im, e_re, e_im,
                                       loc_ref[j, rows(kl), 0:half],
                                       loc_ref[j, rows(kl), half:STATE_LANES])
            for i, n in enumerate((1, 2, 4)):
                p_re = seg_ref[j, i:i + 1, :]
                p_im = seg_ref[j, 3 + i:4 + i, :]
                e_re, e_im = _cmul_add(p_re, p_im, shift_down(e_re, n), shift_down(e_im, n),
                                       e_re, e_im)
            s_re = shift_down(e_re, 1)
            s_im = shift_down(e_im, 1)
            for kl in range(SUBS_PER_SEG):
                sprev_ref[j, rows(kl), 0:half] = s_re
                sprev_ref[j, rows(kl), half:STATE_LANES] = s_im
                s_re, s_im = _cmul_add(a_re, a_im, s_re, s_im,
                                       loc_ref[j, rows(kl), 0:half],
                                       loc_ref[j, rows(kl), half:STATE_LANES])

        zt = zt_ref[j]
        out = (_dot(toept_ref[j], zt)
               + _dot_nt(wot_ref[j], sprev_ref[j].astype(BF16))
               + dcol_ref[j] * zt.astype(F32))
        yt_ref[j] = out.astype(BF16)


def _s5_core(zt, toept, ws, wot, atab, segtab, dcol, bsz):
    g, rows, lanes = zt.shape
    gs = S5_GROUPS_PER_STEP
    wspec = pl.BlockSpec((gs, S5_ROW, STATE_LANES), lambda i: (i, 0, 0))
    return pl.pallas_call(
        functools.partial(_s5_core_kernel, bsz=bsz),
        out_shape=jax.ShapeDtypeStruct((g, rows, lanes), BF16),
        grid=(g // gs,),
        in_specs=[
            pl.BlockSpec((gs, rows, lanes), lambda i: (i, 0, 0)),
            wspec, wspec, wspec,
            pl.BlockSpec((gs, 16, 128), lambda i: (i, 0, 0)),
            pl.BlockSpec((gs, 8, 128), lambda i: (i, 0, 0)),
            pl.BlockSpec((gs, S5_ROW, 1), lambda i: (i, 0, 0)),
        ],
        out_specs=pl.BlockSpec((gs, rows, lanes), lambda i: (i, 0, 0)),
        scratch_shapes=[pltpu.VMEM((gs, lanes, STATE_LANES), F32),
                        pltpu.VMEM((gs, lanes, STATE_LANES), F32)],
        compiler_params=pltpu.CompilerParams(
            dimension_semantics=("parallel",),
            vmem_limit_bytes=VMEM_LIMIT),
        name="s5_core",
    )(zt, toept, ws, wot, atab, segtab, dcol)


def _s5_weights(lam_re, lam_im, log_dt, b_re, b_im, c_re, c_im, d_skip):
    g, p, cg, lb = SSM_GROUPS, SSM_STATE, SSM_GROUP, SUB
    hp = lax.Precision.HIGHEST
    lam = lax.complex(lam_re, lam_im)
    lam_dt = lam * jnp.exp(log_dt)[:, None]
    a_bar = jnp.exp(lam_dt)
    b_bar = ((a_bar - 1.0) / lam)[..., None] * lax.complex(b_re, b_im)
    c_cplx = lax.complex(c_re, c_im)
    steps = jnp.arange(lb + 1, dtype=lam_re.dtype)
    a_pow = jnp.exp(lam_dt[:, None, :] * steps[None, :, None])

    kker = jnp.real(jnp.einsum('gcp,gjp,gpd->gjcd', c_cplx, a_pow[:, :lb], b_bar,
                               precision=hp))
    l_idx = jnp.arange(lb)
    lag = l_idx[:, None] - l_idx[None, :]
    t5 = kker[:, jnp.maximum(lag, 0)]
    t5 = jnp.where((lag >= 0)[None, :, :, None, None], t5, 0.0)
    toept = t5.transpose(0, 1, 3, 2, 4).reshape(g, lb * cg, lb * cg)

    ws_c = a_pow[:, lb - 1 - l_idx][:, :, :, None] * b_bar[:, None]
    ws_c = ws_c.transpose(0, 1, 3, 2).reshape(g, lb * cg, p)
    zc = jnp.zeros((g, lb * cg, 128 - p), lam_re.dtype)
    ws = jnp.concatenate([jnp.real(ws_c), zc, jnp.imag(ws_c), zc], axis=-1)

    wo_c = c_cplx[:, None] * a_pow[:, 1:lb + 1][:, :, None, :]
    wo_c = wo_c.reshape(g, lb * cg, p)
    wot = jnp.concatenate([jnp.real(wo_c), zc, -jnp.imag(wo_c), zc], axis=-1)

    def lane_pad(v):
        return jnp.concatenate([v, jnp.zeros(v.shape[:-1] + (128 - p,), v.dtype)], axis=-1)

    a_blk = a_pow[:, lb][:, None, :]
    atab = jnp.concatenate([jnp.broadcast_to(lane_pad(jnp.real(a_blk)), (g, 8, 128)),
                            jnp.broadcast_to(lane_pad(jnp.imag(a_blk)), (g, 8, 128))], axis=1)
    seg_steps = jnp.asarray([1.0, 2.0, 4.0], lam_re.dtype) * SEG_LEN
    a_seg = jnp.exp(lam_dt[:, None, :] * seg_steps[None, :, None])
    segtab = jnp.concatenate([lane_pad(jnp.real(a_seg)), lane_pad(jnp.imag(a_seg)),
                              jnp.zeros((g, 2, 128), lam_re.dtype)], axis=1)
    dcol = jnp.broadcast_to(d_skip.reshape(g, 1, cg), (g, lb, cg)).reshape(g, lb * cg, 1)
    return (toept.astype(BF16), ws.astype(BF16), wot.astype(BF16),
            atab.astype(F32), segtab.astype(F32), dcol.astype(F32))


def _log_sigmoid(x):
    return -(jnp.maximum(-x, 0.0) + jnp.log1p(jnp.exp(-jnp.abs(x))))


def _chunk_time(idx):
    return ((idx & (SUBS_PER_SEG - 1)) * SUB) | (idx >> 4)


def _mlstm_kernel(x_ref, mod_ref, gain_ref, w3_ref, cw_ref, cb_ref,
                  wq_ref, wk_ref, wv_ref, wg_ref, wgt_ref, bg_ref, bgt_ref,
                  ng_ref, skip_ref, o_ref,
                  ct_ref, m_ref, tail_ref, ext_ref, qkv_ref):
    L, dh, d = SEG_LEN, HEAD_DIM, D_MODEL
    halo = (CONV_WIDTH - 1) * SUBS_PER_SEG

    @pl.when(pl.program_id(1) == 0)
    def _():
        ct_ref[...] = jnp.zeros_like(ct_ref)
        m_ref[...] = jnp.zeros_like(m_ref)
        tail_ref[...] = jnp.zeros_like(tail_ref)

    x = x_ref[...].reshape(L, d)
    h = _modulated_norm(x, gain_ref[...], mod_ref[...]).astype(BF16)
    proj = _dot(h, w3_ref[...])
    m_in = proj[:, 0:d]
    m_o = proj[:, d:2 * d]
    m_g = proj[:, 2 * d:3 * d]

    ext_ref[halo:halo + L, :] = m_in
    row16 = lax.broadcasted_iota(jnp.int32, (SUBS_PER_SEG, d), 0)
    for i in range(CONV_WIDTH - 1):
        lsrc = SUB - (CONV_WIDTH - 1) + i
        r0 = halo + lsrc * SUBS_PER_SEG
        shifted = ext_ref[pl.ds(r0 - 1, SUBS_PER_SEG), :]
        ext_ref[i * SUBS_PER_SEG:(i + 1) * SUBS_PER_SEG, :] = jnp.where(
            row16 == 0, tail_ref[i:i + 1, :], shifted)
        tail_ref[i:i + 1, :] = m_in[lsrc * SUBS_PER_SEG + SUBS_PER_SEG - 1:
                                    (lsrc + 1) * SUBS_PER_SEG, :]
    acc = m_in * cw_ref[CONV_WIDTH - 1:CONV_WIDTH, :] + cb_ref[...]
    for j in range(CONV_WIDTH - 1):
        back = CONV_WIDTH - 1 - j
        acc = acc + ext_ref[pl.ds(halo - back * SUBS_PER_SEG, L), :] * cw_ref[j:j + 1, :]
    xc = acc * _sigmoid(acc)

    xcb = xc.astype(BF16)
    minb = m_in.astype(BF16)
    qs, ks, vs = [], [], []
    for hd in range(HEADS):
        sl = slice(hd * dh, (hd + 1) * dh)
        q = _dot(xcb[:, sl], wq_ref[hd])
        k = _dot(xcb[:, sl], wk_ref[hd]) * (dh ** -0.5)
        v = _dot(minb[:, sl], wv_ref[hd])
        qs.append(q)
        ks.append(k)
        vs.append(v)
        qkv_ref[:, hd * dh:(hd + 1) * dh] = q.astype(BF16)
        qkv_ref[:, d + hd * dh:d + (hd + 1) * dh] = k.astype(BF16)
        qkv_ref[:, 2 * d + hd * dh:2 * d + (hd + 1) * dh] = v.astype(BF16)

    qkv = qkv_ref[...]
    gates = _dot(qkv, wg_ref[...]) + bg_ref[...]
    gates_t = _dot_nt(wgt_ref[...], qkv) + bgt_ref[...]

    t_row = _chunk_time(lax.broadcasted_iota(jnp.int32, (L, L), 0))
    t_col = _chunk_time(lax.broadcasted_iota(jnp.int32, (L, L), 1))
    causal = t_col <= t_row
    tri_l = jnp.where(causal, 1.0, 0.0).astype(BF16)
    tri_u = jnp.where(t_row <= t_col, 1.0, 0.0).astype(BF16)

    lf_hi, lf_lo = _split_hi_lo(_log_sigmoid(gates))
    b_cols = _dot(tri_l, lf_hi) + _dot(tri_l, lf_lo)
    lft_hi, lft_lo = _split_hi_lo(_log_sigmoid(gates_t))
    b_rows = _dot(lft_hi, tri_u) + _dot(lft_lo, tri_u)

    lane = lax.broadcasted_iota(jnp.int32, (L, 128), 1)
    ones_blk = jnp.where(lane == 0, 1.0, 0.0)

    for hd in range(HEADS):
        sl = slice(hd * dh, (hd + 1) * dh)
        i_r = gates_t[hd:hd + 1, :]
        b_r = b_rows[HEADS + hd:HEADS + hd + 1, :]
        i_c = gates[:, hd:hd + 1]
        b_c = b_cols[:, HEADS + hd:HEADS + hd + 1]
        m_prev = m_ref[hd:hd + 1, 0:1]

        log_d = jnp.where(causal, b_c + (i_r - b_r), -jnp.inf)
        m_inter = b_c + m_prev
        m_t = jnp.maximum(m_inter, jnp.max(log_d, axis=-1, keepdims=True))
        dmat = jnp.exp(log_d - m_t)
        qb = qs[hd].astype(BF16)
        kb = ks[hd].astype(BF16)
        s = _dot_nt(qb, kb)
        p = (s * dmat).astype(BF16)
        v_aug = jnp.concatenate([vs[hd], ones_blk], axis=-1)
        w_inter = jnp.exp(m_inter - m_t)
        inter = _dot(qb, ct_ref[hd].astype(BF16))
        intra = _dot(p, v_aug.astype(BF16))
        nd = w_inter * inter + intra
        num = nd[:, 0:dh]
        den = nd[:, dh:dh + 1]
        hh = num / jnp.maximum(jnp.abs(den), jnp.exp(-m_t))

        b_tot = b_c[L - 1:L, :]
        log_w = b_tot - b_c + i_c
        m_next = jnp.maximum(b_tot + m_prev, jnp.max(log_w, axis=0, keepdims=True))
        decay = jnp.exp(b_tot + m_prev - m_next)
        w_c = jnp.exp(log_w - m_next)
        upd = _dot(ks[hd].T.astype(BF16), (w_c * v_aug).astype(BF16))
        ct_ref[hd] = decay * ct_ref[hd] + upd
        m_ref[hd:hd + 1, :] = jnp.broadcast_to(m_next, (1, 128))

        hg = hh * _sigmoid(m_o[:, sl])
        mu = jnp.mean(hg, axis=-1, keepdims=True)
        dev = hg - mu
        var = jnp.mean(dev * dev, axis=-1, keepdims=True)
        hn = dev * lax.rsqrt(var + EPS) * ng_ref[:, sl] + skip_ref[:, sl] * xc[:, sl]
        gate = m_g[:, sl]
        y = hn * (gate * _sigmoid(gate))
        o_ref[:, :, sl] = y.reshape(SUB, SUBS_PER_SEG, dh).astype(o_ref.dtype)


def _mlstm(xp, mod_l, gain, w3, cw, cb, wq, wk, wv, wg, wgt, bg, bgt, ng, skip):
    bsz = xp.shape[0]
    d = D_MODEL
    L = SEG_LEN
    full2 = lambda shape: pl.BlockSpec(shape, lambda b, c: (0, 0))
    full3 = lambda shape: pl.BlockSpec(shape, lambda b, c: (0, 0, 0))
    seg_spec = pl.BlockSpec((None, SUB, None, SUBS_PER_SEG, d), lambda b, c: (b, 0, c, 0, 0))
    halo = (CONV_WIDTH - 1) * SUBS_PER_SEG
    return pl.pallas_call(
        _mlstm_kernel,
        out_shape=jax.ShapeDtypeStruct(xp.shape, BF16),
        grid=(bsz, N_SEG),
        in_specs=[
            seg_spec,
            pl.BlockSpec((None, 1, 3 * d), lambda b, c: (b, 0, 0)),
            full2((1, d)),
            full2((d, 3 * d)),
            full2((CONV_WIDTH, d)),
            full2((1, d)),
            full3((HEADS, HEAD_DIM, HEAD_DIM)),
            full3((HEADS, HEAD_DIM, HEAD_DIM)),
            full3((HEADS, HEAD_DIM, HEAD_DIM)),
            full2((3 * d, 128)),
            full2((8, 3 * d)),
            full2((1, 128)),
            full2((8, 1)),
            full2((1, d)),
            full2((1, d)),
        ],
        out_specs=seg_spec,
        scratch_shapes=[
            pltpu.VMEM((HEADS, HEAD_DIM, M_AUG), F32),
            pltpu.VMEM((8, 128), F32),
            pltpu.VMEM((8, d), F32),
            pltpu.VMEM((halo + L, d), F32),
            pltpu.VMEM((L, 3 * d), BF16),
        ],
        compiler_params=pltpu.CompilerParams(
            dimension_semantics=("parallel", "arbitrary"),
            vmem_limit_bytes=VMEM_LIMIT),
        name="mlstm",
    )(xp, mod_l, gain, w3, cw, cb, wq, wk, wv, wg, wgt, bg, bgt, ng, skip)


def _gelu_tanh(x):
    inner = math.sqrt(2.0 / math.pi) * (x + 0.044715 * (x * x * x))
    return x * (0.5 * (1.0 + jnp.tanh(inner)))


def _out_kernel(yt_ref, gt_ref, my_ref, x_ref, mod_ref, bglu_ref, og_ref, fg_ref,
                wglut_ref, wout_ref, qt_ref, o_ref, *, final):
    d = D_MODEL
    rows = L_PAIR * BLK_LANES
    parts = []
    for l in range(L_PAIR):
        yl = yt_ref[:, l * SSM_GROUP:(l + 1) * SSM_GROUP, :].reshape(d, BLK_LANES)
        parts.append(_dot(yl, qt_ref[...]))
    y = _gelu_tanh(jnp.concatenate(parts, axis=-1))
    glu = y * _sigmoid(_dot(wglut_ref[...], y.astype(BF16)) + bglu_ref[...])
    gate_s = jnp.concatenate([gt_ref[l] for l in range(L_PAIR)], axis=-1).astype(F32)
    ms = jnp.mean(glu * glu, axis=0, keepdims=True)
    ssm_y = glu * lax.rsqrt(ms + EPS) * og_ref[...] * gate_s
    out = (_dot_tn(ssm_y.astype(BF16), wout_ref[0:d, :])
           + _dot(my_ref[...].reshape(rows, d), wout_ref[d:2 * d, :]))
    gate = mod_ref[:, 2 * d:3 * d]
    xn = x_ref[...].reshape(rows, d) + gate * out
    if final:
        xn = _rms(xn) * fg_ref[...]
    o_ref[...] = xn.reshape(L_PAIR, N_SEG, SUBS_PER_SEG, d)


def _out_stage(yt, gt, my, xp, mod_l, bglu_col, og_col, fg, wglut, wout, qperm_t, final):
    bsz = xp.shape[0]
    d = D_MODEL
    n_lp = SUB // L_PAIR
    tok = pl.BlockSpec((None, L_PAIR, N_SEG, SUBS_PER_SEG, d), lambda b, p: (b, p, 0, 0, 0))
    col = pl.BlockSpec((d, 1), lambda b, p: (0, 0))
    return pl.pallas_call(
        functools.partial(_out_kernel, final=final),
        out_shape=jax.ShapeDtypeStruct(xp.shape, F32),
        grid=(bsz, n_lp),
        in_specs=[
            pl.BlockSpec((SSM_GROUPS, L_PAIR * SSM_GROUP, BLK_LANES), lambda b, p: (0, p, b)),
            pl.BlockSpec((None, L_PAIR, d, BLK_LANES), lambda b, p: (b, p, 0, 0)),
            tok, tok,
            pl.BlockSpec((None, 1, 3 * d), lambda b, p: (b, 0, 0)),
            col, col,
            pl.BlockSpec((1, d), lambda b, p: (0, 0)),
            pl.BlockSpec((d, d), lambda b, p: (0, 0)),
            pl.BlockSpec((2 * d, d), lambda b, p: (0, 0)),
            pl.BlockSpec((BLK_LANES, BLK_LANES), lambda b, p: (0, 0)),
        ],
        out_specs=tok,
        compiler_params=pltpu.CompilerParams(
            dimension_semantics=("parallel", "parallel"),
            vmem_limit_bytes=VMEM_LIMIT),
        name="out_stage",
    )(yt, gt, my, xp, mod_l, bglu_col, og_col, fg, wglut, wout, qperm_t)


def _lane_permutation():
    q = np.zeros((BLK_LANES, BLK_LANES), np.float32)
    for seg in range(N_SEG):
        for kl in range(SUBS_PER_SEG):
            q[seg * SUBS_PER_SEG + kl, kl * N_SEG + seg] = 1.0
    return q


def kernel(x, c, norm_gain, w_mod, b_mod, w_in, ssm_lambda_re, ssm_lambda_im, ssm_log_dt,
           ssm_b_re, ssm_b_im, ssm_c_re, ssm_c_im, ssm_d, ssm_w_glu, ssm_b_glu,
           ssm_out_gain, m_conv_w, m_conv_b, m_wq, m_wk, m_wv, m_w_gates, m_b_igate,
           m_b_fgate, m_norm_gain, m_skip, w_out, final_gain):
    bsz, seq, d = x.shape
    depth = w_in.shape[0]
    assert d == D_MODEL and seq == SEQ

    xp = x.reshape(bsz, N_SEG, SUBS_PER_SEG, SUB, d).transpose(0, 3, 1, 2, 4)
    mod_all = _modulation(c, w_mod, b_mod)
    row = lambda v: v.reshape(1, -1)
    colv = lambda v: v.reshape(-1, 1)
    qperm = jnp.asarray(_lane_permutation(), BF16)
    qperm_t = jnp.asarray(_lane_permutation().T, BF16)

    for l in range(depth):
        mod_l = mod_all[l].reshape(bsz, 1, 3 * d)
        w_in_b = w_in[l].astype(BF16)
        gain = row(norm_gain[l])

        zt, gt = _s5_in(xp, mod_l, gain, w_in_b[:, 0:d].T, w_in_b[:, d:2 * d].T, qperm)
        toept, ws, wot, atab, segtab, dcol = _s5_weights(
            ssm_lambda_re[l], ssm_lambda_im[l], ssm_log_dt[l], ssm_b_re[l], ssm_b_im[l],
            ssm_c_re[l], ssm_c_im[l], ssm_d[l])
        yt = _s5_core(zt, toept, ws, wot, atab, segtab, dcol, bsz)

        wg = jnp.zeros((3 * d, 128), F32).at[:, 0:2 * HEADS].set(m_w_gates[l])
        bg = jnp.zeros((1, 128), F32).at[0, 0:HEADS].set(m_b_igate[l])
        bg = bg.at[0, HEADS:2 * HEADS].set(m_b_fgate[l])
        wgt = m_w_gates[l].T
        bgt = jnp.concatenate([m_b_igate[l], m_b_fgate[l]]).reshape(2 * HEADS, 1)
        my = _mlstm(xp, mod_l, gain, w_in_b[:, 2 * d:5 * d], m_conv_w[l], row(m_conv_b[l]),
                    m_wq[l].astype(BF16), m_wk[l].astype(BF16), m_wv[l].astype(BF16),
                    wg.astype(BF16), wgt.astype(BF16), bg, bgt,
                    row(m_norm_gain[l]), row(m_skip[l]))

        xp = _out_stage(yt, gt, my, xp, mod_l, colv(ssm_b_glu[l]), colv(ssm_out_gain[l]),
                        row(final_gain), ssm_w_glu[l].T.astype(BF16), w_out[l].astype(BF16),
                        qperm_t, final=(l == depth - 1))

    return xp.transpose(0, 2, 3, 1, 4).reshape(bsz, seq, d)
```

```python
import functools
import math

import numpy as np
import jax
import jax.numpy as jnp
from jax import lax
from jax.experimental import pallas as pl
from jax.experimental.pallas import tpu as pltpu

F32 = jnp.float32
BF16 = jnp.bfloat16

D_MODEL = 1024
SSM_GROUP = 16
SSM_GROUPS = D_MODEL // SSM_GROUP
SSM_STATE = 64
HEADS = 4
HEAD_DIM = D_MODEL // HEADS
CONV_WIDTH = 4
EPS = 1e-6

SUB = 16
SUBS_PER_SEG = 16
SEG_LEN = SUB * SUBS_PER_SEG
N_SEG = 8
SEQ = N_SEG * SEG_LEN
BLK_LANES = N_SEG * SUBS_PER_SEG
S5_ROW = SUB * SSM_GROUP
STATE_HALF = 128
STATE_LANES = 2 * STATE_HALF
S5_GROUPS_PER_STEP = 4
L_PAIR = 2
M_AUG = HEAD_DIM + 128
VMEM_LIMIT = 56 * 1024 * 1024

_NT = (((1,), (1,)), ((), ()))
_TN = (((0,), (0,)), ((), ()))


def _dot(a, b):
    return jnp.dot(a, b, preferred_element_type=F32)


def _dot_nt(a, b):
    return lax.dot_general(a, b, _NT, preferred_element_type=F32)


def _dot_tn(a, b):
    return lax.dot_general(a, b, _TN, preferred_element_type=F32)


def _sigmoid(x):
    return jax.nn.sigmoid(x)


def _rms(x):
    return x * lax.rsqrt(jnp.mean(x * x, axis=-1, keepdims=True) + EPS)


def _modulated_norm(x, gain, mod):
    shift = mod[:, 0:D_MODEL]
    scale = mod[:, D_MODEL:2 * D_MODEL]
    return _rms(x) * gain * (1.0 + scale) + shift


def _split_hi_lo(v):
    hi = v.astype(BF16)
    lo = (v - hi.astype(F32)).astype(BF16)
    return hi, lo


def _dot_nt_f32(a, b):
    a_hi, a_lo = _split_hi_lo(a)
    b_hi, b_lo = _split_hi_lo(b)
    return _dot_nt(a_hi, b_hi) + _dot_nt(a_hi, b_lo) + _dot_nt(a_lo, b_hi)


def _cmul(a_re, a_im, b_re, b_im):
    return a_re * b_re - a_im * b_im, a_re * b_im + a_im * b_re


def _cmul_add(p_re, p_im, s_re, s_im, add_re, add_im):
    return (p_re * s_re - p_im * s_im + add_re,
            p_re * s_im + p_im * s_re + add_im)


def _mod_kernel(c_ref, w_ref, b_ref, o_ref):
    cv = c_ref[...]
    act = cv * _sigmoid(cv)
    o_ref[...] = _dot(act.astype(BF16), w_ref[...].astype(BF16)) + b_ref[...]


def _modulation(c, w_mod, b_mod):
    depth, d, n = w_mod.shape
    bsz = c.shape[0]
    return pl.pallas_call(
        _mod_kernel,
        out_shape=jax.ShapeDtypeStruct((depth, bsz, n), F32),
        grid=(depth, n // d),
        in_specs=[
            pl.BlockSpec((bsz, d), lambda l, j: (0, 0)),
            pl.BlockSpec((None, d, d), lambda l, j: (l, 0, j)),
            pl.BlockSpec((None, 1, d), lambda l, j: (l, 0, j)),
        ],
        out_specs=pl.BlockSpec((None, bsz, d), lambda l, j: (l, 0, j)),
        compiler_params=pltpu.CompilerParams(
            dimension_semantics=("parallel", "parallel"),
            vmem_limit_bytes=VMEM_LIMIT),
        name="adaln_mod",
    )(c, w_mod, b_mod.reshape(depth, 1, n))


def _s5_prep_kernel(lre_ref, lim_ref, ldt_ref, btre_ref, btim_ref, cre_ref, cim_ref,
                    toept_ref, ws_ref, wot_ref, atab_ref, seg_ref):
    gp = S5_GROUPS_PER_STEP
    lane_blk = lax.broadcasted_iota(jnp.int32, (S5_ROW, S5_ROW), 1) // SSM_GROUP
    row8 = lax.broadcasted_iota(jnp.int32, (8, STATE_HALF), 0)
    for j in range(gp):
        lr = lre_ref[j]
        li = lim_ref[j]
        dt = jnp.exp(ldt_ref[j])
        mag = jnp.exp(lr * dt)
        a_re = mag * jnp.cos(li * dt)
        a_im = mag * jnp.sin(li * dt)
        inv = 1.0 / (lr * lr + li * li)
        k_re = ((a_re - 1.0) * lr + a_im * li) * inv
        k_im = (a_im * lr - (a_re - 1.0) * li) * inv
        bb_re, bb_im = _cmul(k_re, k_im, btre_ref[j], btim_ref[j])
        c_re = cre_ref[j]
        c_im = cim_ref[j]

        pw = [(jnp.ones_like(a_re), jnp.zeros_like(a_re))]
        for _ in range(SUB):
            pw.append(_cmul(pw[-1][0], pw[-1][1], a_re, a_im))

        ws_rows, wot_rows, ca_re, ca_im = [], [], [], []
        for l in range(SUB):
            w_re, w_im = _cmul(pw[SUB - 1 - l][0], pw[SUB - 1 - l][1], bb_re, bb_im)
            ws_rows.append(jnp.concatenate([w_re, w_im], axis=-1))
            o_re, o_im = _cmul(c_re, c_im, pw[l + 1][0], pw[l + 1][1])
            wot_rows.append(jnp.concatenate([o_re, -o_im], axis=-1))
            g_re, g_im = _cmul(c_re, c_im, pw[l][0], pw[l][1])
            ca_re.append(g_re)
            ca_im.append(g_im)
        ws_ref[j] = jnp.concatenate(ws_rows, axis=0).astype(BF16)
        wot_ref[j] = jnp.concatenate(wot_rows, axis=0).astype(BF16)

        bbt_re = jnp.concatenate([bb_re] * SUB, axis=0)
        bbt_im = jnp.concatenate([bb_im] * SUB, axis=0)
        kw = (_dot_nt_f32(jnp.concatenate(ca_re, axis=0), bbt_re)
              - _dot_nt_f32(jnp.concatenate(ca_im, axis=0), bbt_im))
        toep = jnp.where(lane_blk == 0, kw, 0.0)
        for lp in range(1, SUB):
            shifted = jnp.concatenate(
                [jnp.zeros((SSM_GROUP * lp, S5_ROW), F32), kw[0:S5_ROW - SSM_GROUP * lp]], axis=0)
            toep = jnp.where(lane_blk == lp, shifted, toep)
        toept_ref[j] = toep.astype(BF16)

        s_re, s_im = pw[SUB]
        atab_ref[j] = jnp.concatenate([jnp.broadcast_to(s_re, (8, STATE_HALF)),
                                       jnp.broadcast_to(s_im, (8, STATE_HALF))], axis=0)
        for _ in range(4):
            s_re, s_im = _cmul(s_re, s_im, s_re, s_im)
        seg = jnp.zeros((8, STATE_HALF), F32)
        for i in range(3):
            seg = jnp.where(row8 == i, s_re, jnp.where(row8 == 3 + i, s_im, seg))
            s_re, s_im = _cmul(s_re, s_im, s_re, s_im)
        seg_ref[j] = seg


def _s5_prep(lam_re, lam_im, log_dt, b_re, b_im, c_re, c_im):
    depth, g, p = lam_re.shape
    cg = SSM_GROUP
    n = depth * g
    gp = S5_GROUPS_PER_STEP
    lane_pad = ((0, 0), (0, 0), (0, STATE_HALF - p))

    def state_rows(v, fill):
        v = v.reshape(n, -1, p)
        return jnp.pad(v, lane_pad, constant_values=fill)

    args = (state_rows(lam_re, -1.0), state_rows(lam_im, 0.0), log_dt.reshape(n, 1, 1),
            state_rows(b_re.transpose(0, 1, 3, 2), 0.0), state_rows(b_im.transpose(0, 1, 3, 2), 0.0),
            state_rows(c_re, 0.0), state_rows(c_im, 0.0))
    vec = pl.BlockSpec((gp, 1, STATE_HALF), lambda i: (i, 0, 0))
    mat = pl.BlockSpec((gp, cg, STATE_HALF), lambda i: (i, 0, 0))
    big = pl.BlockSpec((gp, S5_ROW, STATE_LANES), lambda i: (i, 0, 0))
    return pl.pallas_call(
        _s5_prep_kernel,
        out_shape=(jax.ShapeDtypeStruct((n, S5_ROW, S5_ROW), BF16),
                   jax.ShapeDtypeStruct((n, S5_ROW, STATE_LANES), BF16),
                   jax.ShapeDtypeStruct((n, S5_ROW, STATE_LANES), BF16),
                   jax.ShapeDtypeStruct((n, 16, STATE_HALF), F32),
                   jax.ShapeDtypeStruct((n, 8, STATE_HALF), F32)),
        grid=(n // gp,),
        in_specs=[vec, vec, pl.BlockSpec((gp, 1, 1), lambda i: (i, 0, 0)), mat, mat, mat, mat],
        out_specs=(big, big, big,
                   pl.BlockSpec((gp, 16, STATE_HALF), lambda i: (i, 0, 0)),
                   pl.BlockSpec((gp, 8, STATE_HALF), lambda i: (i, 0, 0))),
        compiler_params=pltpu.CompilerParams(
            dimension_semantics=("parallel",),
            vmem_limit_bytes=VMEM_LIMIT),
        name="s5_prep",
    )(*args)


def _s5_in_kernel(x_ref, mod_ref, gain_ref, wut_ref, wgt_ref, q_ref, zt_ref, gt_ref):
    d = D_MODEL
    x = x_ref[...].reshape(L_PAIR * BLK_LANES, d)
    h = _modulated_norm(x, gain_ref[...], mod_ref[...]).astype(BF16)
    ut = _dot_nt(wut_ref[...], h).astype(BF16)
    sgt = _dot_nt(wgt_ref[...], h)
    for l in range(L_PAIR):
        lanes = slice(l * BLK_LANES, (l + 1) * BLK_LANES)
        p = _dot(ut[:, lanes], q_ref[...])
        zt_ref[:, l * SSM_GROUP:(l + 1) * SSM_GROUP, :] = (
            p.reshape(SSM_GROUPS, SSM_GROUP, BLK_LANES).astype(BF16))
        g = sgt[:, lanes]
        gt_ref[l] = (g * _sigmoid(g)).astype(BF16)


def _s5_in(xp, mod_all, gain_all, w_ssm_t, qperm, layer):
    bsz = xp.shape[0]
    d = D_MODEL
    n_lp = SUB // L_PAIR
    return pl.pallas_call(
        _s5_in_kernel,
        out_shape=(jax.ShapeDtypeStruct((SSM_GROUPS, S5_ROW, bsz * BLK_LANES), BF16),
                   jax.ShapeDtypeStruct((bsz, SUB, d, BLK_LANES), BF16)),
        grid=(bsz, n_lp),
        in_specs=[
            pl.BlockSpec((None, L_PAIR, N_SEG, SUBS_PER_SEG, d), lambda b, p: (b, p, 0, 0, 0)),
            pl.BlockSpec((None, None, 1, 3 * d), lambda b, p: (layer, b, 0, 0)),
            pl.BlockSpec((None, 1, d), lambda b, p: (layer, 0, 0)),
            pl.BlockSpec((None, d, d), lambda b, p: (layer, 0, 0)),
            pl.BlockSpec((None, d, d), lambda b, p: (layer, 1, 0)),
            pl.BlockSpec((BLK_LANES, BLK_LANES), lambda b, p: (0, 0)),
        ],
        out_specs=(
            pl.BlockSpec((SSM_GROUPS, L_PAIR * SSM_GROUP, BLK_LANES), lambda b, p: (0, p, b)),
            pl.BlockSpec((None, L_PAIR, d, BLK_LANES), lambda b, p: (b, p, 0, 0)),
        ),
        compiler_params=pltpu.CompilerParams(
            dimension_semantics=("parallel", "parallel"),
            vmem_limit_bytes=VMEM_LIMIT),
        name="s5_in",
    )(xp, mod_all, gain_all, w_ssm_t, w_ssm_t, qperm)


def _s5_core_kernel(zt_ref, toept_ref, ws_ref, wot_ref, atab_ref, seg_ref, dcol_ref,
                    yt_ref, loc_ref, sprev_ref, *, bsz):
    gs = S5_GROUPS_PER_STEP
    half = STATE_HALF
    row8 = lax.broadcasted_iota(jnp.int32, (N_SEG, half), 0)

    def shift_down(v, n):
        return jnp.where(row8 >= n, pltpu.roll(v, n, axis=0), 0.0)

    for j in range(gs):
        loc_ref[j] = _dot_tn(zt_ref[j], ws_ref[j])
        a_re = atab_ref[j, 0:8, :]
        a_im = atab_ref[j, 8:16, :]
        for b in range(bsz):
            base = b * BLK_LANES

            def rows(kl, base=base):
                return pl.ds(base + kl * N_SEG, N_SEG)

            e_re = jnp.zeros((N_SEG, half), F32)
            e_im = jnp.zeros((N_SEG, half), F32)
            for kl in range(SUBS_PER_SEG):
                e_re, e_im = _cmul_add(a_re, a_im, e_re, e_im,
                                       loc_ref[j, rows(kl), 0:half],
                                       loc_ref[j, rows(kl), half:STATE_LANES])
            for i, n in enumerate((1, 2, 4)):
                p_re = seg_ref[j, i:i + 1, :]
                p_im = seg_ref[j, 3 + i:4 + i, :]
                e_re, e_im = _cmul_add(p_re, p_im, shift_down(e_re, n), shift_down(e_im, n),
                                       e_re, e_im)
            s_re = shift_down(e_re, 1)
            s_im = shift_down(e_im, 1)
            for kl in range(SUBS_PER_SEG):
                sprev_ref[j, rows(kl), 0:half] = s_re
                sprev_ref[j, rows(kl), half:STATE_LANES] = s_im
                s_re, s_im = _cmul_add(a_re, a_im, s_re, s_im,
                                       loc_ref[j, rows(kl), 0:half],
                                       loc_ref[j, rows(kl), half:STATE_LANES])

        zt = zt_ref[j]
        out = (_dot(toept_ref[j], zt)
               + _dot_nt(wot_ref[j], sprev_ref[j].astype(BF16))
               + dcol_ref[j] * zt.astype(F32))
        yt_ref[j] = out.astype(BF16)


def _s5_core(zt, toept, ws, wot, atab, segtab, dcol, bsz, layer):
    g, rows, lanes = zt.shape
    gs = S5_GROUPS_PER_STEP
    off = layer * (g // gs)
    wspec = pl.BlockSpec((gs, S5_ROW, STATE_LANES), lambda i: (off + i, 0, 0))
    return pl.pallas_call(
        functools.partial(_s5_core_kernel, bsz=bsz),
        out_shape=jax.ShapeDtypeStruct((g, rows, lanes), BF16),
        grid=(g // gs,),
        in_specs=[
            pl.BlockSpec((gs, rows, lanes), lambda i: (i, 0, 0)),
            wspec, wspec, wspec,
            pl.BlockSpec((gs, 16, STATE_HALF), lambda i: (off + i, 0, 0)),
            pl.BlockSpec((gs, 8, STATE_HALF), lambda i: (off + i, 0, 0)),
            pl.BlockSpec((gs, S5_ROW, 1), lambda i: (off + i, 0, 0)),
        ],
        out_specs=pl.BlockSpec((gs, rows, lanes), lambda i: (i, 0, 0)),
        scratch_shapes=[pltpu.VMEM((gs, lanes, STATE_LANES), F32),
                        pltpu.VMEM((gs, lanes, STATE_LANES), F32)],
        compiler_params=pltpu.CompilerParams(
            dimension_semantics=("parallel",),
            vmem_limit_bytes=VMEM_LIMIT),
        name="s5_core",
    )(zt, toept, ws, wot, atab, segtab, dcol)


def _log_sigmoid(x):
    return -(jnp.maximum(-x, 0.0) + jnp.log1p(jnp.exp(-jnp.abs(x))))


def _chunk_time(idx):
    return ((idx & (SUBS_PER_SEG - 1)) * SUB) | (idx >> 4)


def _mlstm_kernel(x_ref, mod_ref, gain_ref, w3_ref, cw_ref, cb_ref,
                  wq_ref, wk_ref, wv_ref, wgt_ref, bgt_ref, ng_ref, skip_ref, o_ref,
                  ct_ref, m_ref, tail_ref, ext_ref, qkv_ref, proj_ref, xc_ref):
    L, dh, d = SEG_LEN, HEAD_DIM, D_MODEL
    halo = (CONV_WIDTH - 1) * SUBS_PER_SEG

    @pl.when(pl.program_id(1) == 0)
    def _():
        ct_ref[...] = jnp.zeros_like(ct_ref)
        m_ref[...] = jnp.zeros_like(m_ref)
        tail_ref[...] = jnp.zeros_like(tail_ref)

    x = x_ref[...].reshape(L, d)
    h = _modulated_norm(x, gain_ref[...], mod_ref[...]).astype(BF16)
    proj_ref[...] = _dot(h, w3_ref[...])
    m_in = proj_ref[:, 0:d]

    ext_ref[halo:halo + L, :] = m_in
    row16 = lax.broadcasted_iota(jnp.int32, (SUBS_PER_SEG, d), 0)
    for i in range(CONV_WIDTH - 1):
        lsrc = SUB - (CONV_WIDTH - 1) + i
        r0 = halo + lsrc * SUBS_PER_SEG
        shifted = ext_ref[pl.ds(r0 - 1, SUBS_PER_SEG), :]
        ext_ref[i * SUBS_PER_SEG:(i + 1) * SUBS_PER_SEG, :] = jnp.where(
            row16 == 0, tail_ref[i:i + 1, :], shifted)
        tail_ref[i:i + 1, :] = ext_ref[r0 + SUBS_PER_SEG - 1:r0 + SUBS_PER_SEG, :]
    acc = m_in * cw_ref[CONV_WIDTH - 1:CONV_WIDTH, :] + cb_ref[...]
    for j in range(CONV_WIDTH - 1):
        back = CONV_WIDTH - 1 - j
        acc = acc + ext_ref[pl.ds(halo - back * SUBS_PER_SEG, L), :] * cw_ref[j:j + 1, :]
    xc = acc * _sigmoid(acc)
    xc_ref[...] = xc

    xcb = xc.astype(BF16)
    minb = m_in.astype(BF16)
    for hd in range(HEADS):
        sl = slice(hd * dh, (hd + 1) * dh)
        qkv_ref[:, hd * dh:(hd + 1) * dh] = _dot(xcb[:, sl], wq_ref[hd]).astype(BF16)
        qkv_ref[:, d + hd * dh:d + (hd + 1) * dh] = (
            _dot(xcb[:, sl], wk_ref[hd]) * (dh ** -0.5)).astype(BF16)
        qkv_ref[:, 2 * d + hd * dh:2 * d + (hd + 1) * dh] = (
            _dot(minb[:, sl], wv_ref[hd]).astype(BF16))

    gates_t = _dot_nt(wgt_ref[...], qkv_ref[...]) + bgt_ref[...]

    t_row = _chunk_time(lax.broadcasted_iota(jnp.int32, (L, L), 0))
    t_col = _chunk_time(lax.broadcasted_iota(jnp.int32, (L, L), 1))
    causal = t_col <= t_row
    tri_u = jnp.where(t_row <= t_col, 1.0, 0.0).astype(BF16)

    lft_hi, lft_lo = _split_hi_lo(_log_sigmoid(gates_t))
    b_rows = _dot(lft_hi, tri_u) + _dot(lft_lo, tri_u)
    stacked = jnp.concatenate([gates_t, b_rows, jnp.zeros((128 - 16, L), F32)], axis=0)
    cols = stacked.T

    lane = lax.broadcasted_iota(jnp.int32, (L, 128), 1)
    ones_blk = jnp.where(lane == 0, 1.0, 0.0).astype(BF16)

    for hd in range(HEADS):
        sl = slice(hd * dh, (hd + 1) * dh)
        i_r = gates_t[hd:hd + 1, :]
        b_r = b_rows[HEADS + hd:HEADS + hd + 1, :]
        i_c = cols[:, hd:hd + 1]
        b_c = cols[:, 8 + HEADS + hd:8 + HEADS + hd + 1]
        m_prev = m_ref[hd:hd + 1, 0:1]

        log_d = jnp.where(causal, b_c + (i_r - b_r), -jnp.inf)
        m_inter = b_c + m_prev
        m_t = jnp.maximum(m_inter, jnp.max(log_d, axis=-1, keepdims=True))
        dmat = jnp.exp(log_d - m_t)
        qb = qkv_ref[:, sl]
        kb = qkv_ref[:, d + hd * dh:d + (hd + 1) * dh]
        s = _dot_nt(qb, kb)
        p = (s * dmat).astype(BF16)
        v_aug = jnp.concatenate([qkv_ref[:, 2 * d + hd * dh:2 * d + (hd + 1) * dh], ones_blk],
                                axis=-1)
        w_inter = jnp.exp(m_inter - m_t)
        inter = _dot(qb, ct_ref[hd].astype(BF16))
        intra = _dot(p, v_aug)
        nd = w_inter * inter + intra
        num = nd[:, 0:dh]
        den = nd[:, dh:dh + 1]
        hh = num / jnp.maximum(jnp.abs(den), jnp.exp(-m_t))

        b_tot = b_c[L - 1:L, :]
        log_w = b_tot - b_c + i_c
        m_next = jnp.maximum(b_tot + m_prev, jnp.max(log_w, axis=0, keepdims=True))
        decay = jnp.exp(b_tot + m_prev - m_next)
        w_c = jnp.exp(log_w - m_next)
        upd = _dot(kb.T, (w_c * v_aug.astype(F32)).astype(BF16))
        ct_ref[hd] = decay * ct_ref[hd] + upd
        m_ref[hd:hd + 1, :] = jnp.broadcast_to(m_next, (1, 128))

        hg = hh * _sigmoid(proj_ref[:, d + hd * dh:d + (hd + 1) * dh])
        mu = jnp.mean(hg, axis=-1, keepdims=True)
        dev = hg - mu
        var = jnp.mean(dev * dev, axis=-1, keepdims=True)
        hn = dev * lax.rsqrt(var + EPS) * ng_ref[:, sl] + skip_ref[:, sl] * xc_ref[:, sl]
        gate = proj_ref[:, 2 * d + hd * dh:2 * d + (hd + 1) * dh]
        y = hn * (gate * _sigmoid(gate))
        o_ref[:, :, sl] = y.reshape(SUB, SUBS_PER_SEG, dh).astype(o_ref.dtype)


def _mlstm(xp, mod_all, gain_all, w3, cw, cb, wq, wk, wv, wgt, bgt, ng, skip, layer):
    bsz = xp.shape[0]
    d = D_MODEL
    L = SEG_LEN
    lay2 = lambda shape: pl.BlockSpec((None,) + shape, lambda b, c: (layer, 0, 0))
    lay3 = lambda shape: pl.BlockSpec((None,) + shape, lambda b, c: (layer, 0, 0, 0))
    seg_spec = pl.BlockSpec((None, SUB, None, SUBS_PER_SEG, d), lambda b, c: (b, 0, c, 0, 0))
    halo = (CONV_WIDTH - 1) * SUBS_PER_SEG
    return pl.pallas_call(
        _mlstm_kernel,
        out_shape=jax.ShapeDtypeStruct(xp.shape, BF16),
        grid=(bsz, N_SEG),
        in_specs=[
            seg_spec,
            pl.BlockSpec((None, None, 1, 3 * d), lambda b, c: (layer, b, 0, 0)),
            lay2((1, d)),
            lay2((d, 3 * d)),
            lay2((CONV_WIDTH, d)),
            lay2((1, d)),
            lay3((HEADS, HEAD_DIM, HEAD_DIM)),
            lay3((HEADS, HEAD_DIM, HEAD_DIM)),
            lay3((HEADS, HEAD_DIM, HEAD_DIM)),
            lay2((8, 3 * d)),
            lay2((8, 1)),
            lay2((1, d)),
            lay2((1, d)),
        ],
        out_specs=seg_spec,
        scratch_shapes=[
            pltpu.VMEM((HEADS, HEAD_DIM, M_AUG), F32),
            pltpu.VMEM((8, 128), F32),
            pltpu.VMEM((8, d), F32),
            pltpu.VMEM((halo + L, d), F32),
            pltpu.VMEM((L, 3 * d), BF16),
            pltpu.VMEM((L, 3 * d), F32),
            pltpu.VMEM((L, d), F32),
        ],
        compiler_params=pltpu.CompilerParams(
            dimension_semantics=("parallel", "arbitrary"),
            vmem_limit_bytes=VMEM_LIMIT),
        name="mlstm",
    )(xp, mod_all, gain_all, w3, cw, cb, wq, wk, wv, wgt, bgt, ng, skip)


def _gelu_tanh(x):
    inner = math.sqrt(2.0 / math.pi) * (x + 0.044715 * (x * x * x))
    return x * (0.5 * (1.0 + jnp.tanh(inner)))


def _out_kernel(yt_ref, gt_ref, my_ref, x_ref, mod_ref, bglu_ref, og_ref, fg_ref,
                wglut_ref, wout_ref, qt_ref, o_ref, *, final):
    d = D_MODEL
    rows = L_PAIR * BLK_LANES
    out_m = _dot(my_ref[...].reshape(rows, d), wout_ref[d:2 * d, :])
    parts = []
    for l in range(L_PAIR):
        yl = yt_ref[:, l * SSM_GROUP:(l + 1) * SSM_GROUP, :].reshape(d, BLK_LANES)
        parts.append(_dot(yl, qt_ref[...]))
    y = _gelu_tanh(jnp.concatenate(parts, axis=-1))
    glu = y * _sigmoid(_dot(wglut_ref[...], y.astype(BF16)) + bglu_ref[...])
    gate_s = jnp.concatenate([gt_ref[l] for l in range(L_PAIR)], axis=-1).astype(F32)
    ms = jnp.mean(glu * glu, axis=0, keepdims=True)
    ssm_y = glu * lax.rsqrt(ms + EPS) * og_ref[...] * gate_s
    out = _dot_tn(ssm_y.astype(BF16), wout_ref[0:d, :]) + out_m
    gate = mod_ref[:, 2 * d:3 * d]
    xn = x_ref[...].reshape(rows, d) + gate * out
    if final:
        xn = _rms(xn) * fg_ref[...]
    o_ref[...] = xn.reshape(L_PAIR, N_SEG, SUBS_PER_SEG, d)


def _out_stage(yt, gt, my, xp, mod_all, bglu_col, og_col, fg, wglut, wout, qperm_t, layer, final):
    bsz = xp.shape[0]
    d = D_MODEL
    n_lp = SUB // L_PAIR
    tok = pl.BlockSpec((None, L_PAIR, N_SEG, SUBS_PER_SEG, d), lambda b, p: (b, p, 0, 0, 0))
    col = pl.BlockSpec((None, d, 1), lambda b, p: (layer, 0, 0))
    return pl.pallas_call(
        functools.partial(_out_kernel, final=final),
        out_shape=jax.ShapeDtypeStruct(xp.shape, F32),
        grid=(bsz, n_lp),
        in_specs=[
            pl.BlockSpec((SSM_GROUPS, L_PAIR * SSM_GROUP, BLK_LANES), lambda b, p: (0, p, b)),
            pl.BlockSpec((None, L_PAIR, d, BLK_LANES), lambda b, p: (b, p, 0, 0)),
            tok, tok,
            pl.BlockSpec((None, None, 1, 3 * d), lambda b, p: (layer, b, 0, 0)),
            col, col,
            pl.BlockSpec((1, d), lambda b, p: (0, 0)),
            pl.BlockSpec((None, d, d), lambda b, p: (layer, 0, 0)),
            pl.BlockSpec((None, 2 * d, d), lambda b, p: (layer, 0, 0)),
            pl.BlockSpec((BLK_LANES, BLK_LANES), lambda b, p: (0, 0)),
        ],
        out_specs=tok,
        compiler_params=pltpu.CompilerParams(
            dimension_semantics=("parallel", "parallel"),
            vmem_limit_bytes=VMEM_LIMIT),
        name="out_stage",
    )(yt, gt, my, xp, mod_all, bglu_col, og_col, fg, wglut, wout, qperm_t)


def _lane_permutation():
    q = np.zeros((BLK_LANES, BLK_LANES), np.float32)
    for seg in range(N_SEG):
        for kl in range(SUBS_PER_SEG):
            q[seg * SUBS_PER_SEG + kl, kl * N_SEG + seg] = 1.0
    return q


def kernel(x, c, norm_gain, w_mod, b_mod, w_in, ssm_lambda_re, ssm_lambda_im, ssm_log_dt,
           ssm_b_re, ssm_b_im, ssm_c_re, ssm_c_im, ssm_d, ssm_w_glu, ssm_b_glu,
           ssm_out_gain, m_conv_w, m_conv_b, m_wq, m_wk, m_wv, m_w_gates, m_b_igate,
           m_b_fgate, m_norm_gain, m_skip, w_out, final_gain):
    bsz, seq, d = x.shape
    depth = w_in.shape[0]
    assert d == D_MODEL and seq == SEQ

    xp = x.reshape(bsz, N_SEG, SUBS_PER_SEG, SUB, d).transpose(0, 3, 1, 2, 4)
    mod_all = _modulation(c, w_mod, b_mod).reshape(depth, bsz, 1, 3 * d)
    rows = lambda v: v.reshape(depth, 1, -1)
    cols = lambda v: v.reshape(depth, -1, 1)
    qperm = jnp.asarray(_lane_permutation(), BF16)
    qperm_t = jnp.asarray(_lane_permutation().T, BF16)

    gain_all = rows(norm_gain)
    w_ssm_t = w_in[:, :, 0:2 * d].transpose(0, 2, 1).astype(BF16)
    w3 = w_in[:, :, 2 * d:5 * d].astype(BF16)
    toept, ws, wot, atab, segtab = _s5_prep(ssm_lambda_re, ssm_lambda_im, ssm_log_dt,
                                            ssm_b_re, ssm_b_im, ssm_c_re, ssm_c_im)
    dcol = jnp.broadcast_to(ssm_d.reshape(depth * SSM_GROUPS, 1, SSM_GROUP),
                            (depth * SSM_GROUPS, SUB, SSM_GROUP)).reshape(-1, S5_ROW, 1)
    wq, wk, wv = m_wq.astype(BF16), m_wk.astype(BF16), m_wv.astype(BF16)
    wgt = m_w_gates.transpose(0, 2, 1).astype(BF16)
    bgt = jnp.concatenate([m_b_igate, m_b_fgate], axis=-1).reshape(depth, 2 * HEADS, 1)
    wglut = ssm_w_glu.transpose(0, 2, 1).astype(BF16)
    wout = w_out.astype(BF16)
    fg = final_gain.reshape(1, d)

    for l in range(depth):
        zt, gt = _s5_in(xp, mod_all, gain_all, w_ssm_t, qperm, l)
        yt = _s5_core(zt, toept, ws, wot, atab, segtab, dcol, bsz, l)
        my = _mlstm(xp, mod_all, gain_all, w3, m_conv_w, rows(m_conv_b), wq, wk, wv, wgt, bgt,
                    rows(m_norm_gain), rows(m_skip), l)
        xp = _out_stage(yt, gt, my, xp, mod_all, cols(ssm_b_glu), cols(ssm_out_gain), fg,
                        wglut, wout, qperm_t, l, final=(l == depth - 1))

    return xp.transpose(0, 2, 3, 1, 4).reshape(bsz, seq, d)
```

```python
import functools
import math

import numpy as np
import jax
import jax.numpy as jnp
from jax import lax
from jax.experimental import pallas as pl
from jax.experimental.pallas import tpu as pltpu

F32 = jnp.float32
BF16 = jnp.bfloat16

D_MODEL = 1024
SSM_GROUP = 16
SSM_GROUPS = D_MODEL // SSM_GROUP
SSM_STATE = 64
HEADS = 4
HEAD_DIM = D_MODEL // HEADS
CONV_WIDTH = 4
EPS = 1e-6

SUB = 16
SUBS_PER_SEG = 16
SEG_LEN = SUB * SUBS_PER_SEG
N_SEG = 8
SEQ = N_SEG * SEG_LEN
BLK_LANES = N_SEG * SUBS_PER_SEG
S5_ROW = SUB * SSM_GROUP
STATE_HALF = 128
STATE_LANES = 2 * STATE_HALF
S5_GROUPS_PER_STEP = 4
L_PAIR = 2
M_AUG = HEAD_DIM + 128
VMEM_LIMIT = 56 * 1024 * 1024

_NT = (((1,), (1,)), ((), ()))
_TN = (((0,), (0,)), ((), ()))


def _dot(a, b):
    return jnp.dot(a, b, preferred_element_type=F32)


def _dot_nt(a, b):
    return lax.dot_general(a, b, _NT, preferred_element_type=F32)


def _dot_tn(a, b):
    return lax.dot_general(a, b, _TN, preferred_element_type=F32)


def _sigmoid(x):
    return jax.nn.sigmoid(x)


def _rms(x):
    return x * lax.rsqrt(jnp.mean(x * x, axis=-1, keepdims=True) + EPS)


def _modulated_norm(x, gain, mod):
    shift = mod[:, 0:D_MODEL]
    scale = mod[:, D_MODEL:2 * D_MODEL]
    return _rms(x) * gain * (1.0 + scale) + shift


def _split_hi_lo(v):
    hi = v.astype(BF16)
    lo = (v - hi.astype(F32)).astype(BF16)
    return hi, lo


def _dot_nt_f32(a, b):
    a_hi, a_lo = _split_hi_lo(a)
    b_hi, b_lo = _split_hi_lo(b)
    return _dot_nt(a_hi, b_hi) + _dot_nt(a_hi, b_lo) + _dot_nt(a_lo, b_hi)


def _cmul(a_re, a_im, b_re, b_im):
    return a_re * b_re - a_im * b_im, a_re * b_im + a_im * b_re


def _cmul_add(p_re, p_im, s_re, s_im, add_re, add_im):
    return (p_re * s_re - p_im * s_im + add_re,
            p_re * s_im + p_im * s_re + add_im)


def _mod_kernel(c_ref, w_ref, b_ref, o_ref):
    cv = c_ref[...]
    act = cv * _sigmoid(cv)
    o_ref[...] = _dot(act.astype(BF16), w_ref[...].astype(BF16)) + b_ref[...]


def _modulation(c, w_mod, b_mod):
    depth, d, n = w_mod.shape
    bsz = c.shape[0]
    return pl.pallas_call(
        _mod_kernel,
        out_shape=jax.ShapeDtypeStruct((depth, bsz, n), F32),
        grid=(depth, n // d),
        in_specs=[
            pl.BlockSpec((bsz, d), lambda l, j: (0, 0)),
            pl.BlockSpec((None, d, d), lambda l, j: (l, 0, j)),
            pl.BlockSpec((None, 1, d), lambda l, j: (l, 0, j)),
        ],
        out_specs=pl.BlockSpec((None, bsz, d), lambda l, j: (l, 0, j)),
        compiler_params=pltpu.CompilerParams(
            dimension_semantics=("parallel", "parallel"),
            vmem_limit_bytes=VMEM_LIMIT),
        name="adaln_mod",
    )(c, w_mod, b_mod.reshape(depth, 1, n))


def _s5_prep_kernel(lre_ref, lim_ref, ldt_ref, btre_ref, btim_ref, cre_ref, cim_ref,
                    toept_ref, ws_ref, wot_ref, atab_ref, seg_ref):
    gp = S5_GROUPS_PER_STEP
    lane_blk = lax.broadcasted_iota(jnp.int32, (S5_ROW, S5_ROW), 1) // SSM_GROUP
    row8 = lax.broadcasted_iota(jnp.int32, (8, STATE_HALF), 0)
    for j in range(gp):
        lr = lre_ref[j]
        li = lim_ref[j]
        dt = jnp.exp(ldt_ref[j])
        mag = jnp.exp(lr * dt)
        a_re = mag * jnp.cos(li * dt)
        a_im = mag * jnp.sin(li * dt)
        inv = 1.0 / (lr * lr + li * li)
        k_re = ((a_re - 1.0) * lr + a_im * li) * inv
        k_im = (a_im * lr - (a_re - 1.0) * li) * inv
        bb_re, bb_im = _cmul(k_re, k_im, btre_ref[j], btim_ref[j])
        c_re = cre_ref[j]
        c_im = cim_ref[j]

        pw = [(jnp.ones_like(a_re), jnp.zeros_like(a_re))]
        for _ in range(SUB):
            pw.append(_cmul(pw[-1][0], pw[-1][1], a_re, a_im))

        ws_rows, wot_rows, ca_re, ca_im = [], [], [], []
        for l in range(SUB):
            w_re, w_im = _cmul(pw[SUB - 1 - l][0], pw[SUB - 1 - l][1], bb_re, bb_im)
            ws_rows.append(jnp.concatenate([w_re, w_im], axis=-1))
            o_re, o_im = _cmul(c_re, c_im, pw[l + 1][0], pw[l + 1][1])
            wot_rows.append(jnp.concatenate([o_re, -o_im], axis=-1))
            g_re, g_im = _cmul(c_re, c_im, pw[l][0], pw[l][1])
            ca_re.append(g_re)
            ca_im.append(g_im)
        ws_ref[j] = jnp.concatenate(ws_rows, axis=0).astype(BF16)
        wot_ref[j] = jnp.concatenate(wot_rows, axis=0).astype(BF16)

        bbt_re = jnp.concatenate([bb_re] * SUB, axis=0)
        bbt_im = jnp.concatenate([bb_im] * SUB, axis=0)
        kw = (_dot_nt_f32(jnp.concatenate(ca_re, axis=0), bbt_re)
              - _dot_nt_f32(jnp.concatenate(ca_im, axis=0), bbt_im))
        toep = jnp.where(lane_blk == 0, kw, 0.0)
        for lp in range(1, SUB):
            shifted = jnp.concatenate(
                [jnp.zeros((SSM_GROUP * lp, S5_ROW), F32), kw[0:S5_ROW - SSM_GROUP * lp]], axis=0)
            toep = jnp.where(lane_blk == lp, shifted, toep)
        toept_ref[j] = toep.astype(BF16)

        s_re, s_im = pw[SUB]
        atab_ref[j] = jnp.concatenate([jnp.broadcast_to(s_re, (8, STATE_HALF)),
                                       jnp.broadcast_to(s_im, (8, STATE_HALF))], axis=0)
        for _ in range(4):
            s_re, s_im = _cmul(s_re, s_im, s_re, s_im)
        seg = jnp.zeros((8, STATE_HALF), F32)
        for i in range(3):
            seg = jnp.where(row8 == i, s_re, jnp.where(row8 == 3 + i, s_im, seg))
            s_re, s_im = _cmul(s_re, s_im, s_re, s_im)
        seg_ref[j] = seg


def _s5_prep(lam_re, lam_im, log_dt, b_re, b_im, c_re, c_im):
    depth, g, p = lam_re.shape
    cg = SSM_GROUP
    n = depth * g
    gp = S5_GROUPS_PER_STEP
    lane_pad = ((0, 0), (0, 0), (0, STATE_HALF - p))

    def state_rows(v, fill):
        v = v.reshape(n, -1, p)
        return jnp.pad(v, lane_pad, constant_values=fill)

    args = (state_rows(lam_re, -1.0), state_rows(lam_im, 0.0), log_dt.reshape(n, 1, 1),
            state_rows(b_re.transpose(0, 1, 3, 2), 0.0), state_rows(b_im.transpose(0, 1, 3, 2), 0.0),
            state_rows(c_re, 0.0), state_rows(c_im, 0.0))
    vec = pl.BlockSpec((gp, 1, STATE_HALF), lambda i: (i, 0, 0))
    mat = pl.BlockSpec((gp, cg, STATE_HALF), lambda i: (i, 0, 0))
    big = pl.BlockSpec((gp, S5_ROW, STATE_LANES), lambda i: (i, 0, 0))
    return pl.pallas_call(
        _s5_prep_kernel,
        out_shape=(jax.ShapeDtypeStruct((n, S5_ROW, S5_ROW), BF16),
                   jax.ShapeDtypeStruct((n, S5_ROW, STATE_LANES), BF16),
                   jax.ShapeDtypeStruct((n, S5_ROW, STATE_LANES), BF16),
                   jax.ShapeDtypeStruct((n, 16, STATE_HALF), F32),
                   jax.ShapeDtypeStruct((n, 8, STATE_HALF), F32)),
        grid=(n // gp,),
        in_specs=[vec, vec, pl.BlockSpec((gp, 1, 1), lambda i: (i, 0, 0)), mat, mat, mat, mat],
        out_specs=(big, big, big,
                   pl.BlockSpec((gp, 16, STATE_HALF), lambda i: (i, 0, 0)),
                   pl.BlockSpec((gp, 8, STATE_HALF), lambda i: (i, 0, 0))),
        compiler_params=pltpu.CompilerParams(
            dimension_semantics=("parallel",),
            vmem_limit_bytes=VMEM_LIMIT),
        name="s5_prep",
    )(*args)


def _s5_in_kernel(x_ref, mod_ref, gain_ref, wut_ref, wgt_ref, q_ref, zt_ref, gt_ref):
    d = D_MODEL
    x = x_ref[...].reshape(L_PAIR * BLK_LANES, d)
    h = _modulated_norm(x, gain_ref[...], mod_ref[...]).astype(BF16)
    ut = _dot_nt(wut_ref[...], h).astype(BF16)
    sgt = _dot_nt(wgt_ref[...], h)
    for l in range(L_PAIR):
        lanes = slice(l * BLK_LANES, (l + 1) * BLK_LANES)
        p = _dot(ut[:, lanes], q_ref[...])
        zt_ref[:, l * SSM_GROUP:(l + 1) * SSM_GROUP, :] = (
            p.reshape(SSM_GROUPS, SSM_GROUP, BLK_LANES).astype(BF16))
        g = sgt[:, lanes]
        gt_ref[l] = (g * _sigmoid(g)).astype(BF16)


def _s5_in(xp, mod_all, gain_all, w_ssm_t, qperm, layer):
    bsz = xp.shape[0]
    d = D_MODEL
    n_lp = SUB // L_PAIR
    return pl.pallas_call(
        _s5_in_kernel,
        out_shape=(jax.ShapeDtypeStruct((SSM_GROUPS, S5_ROW, bsz * BLK_LANES), BF16),
                   jax.ShapeDtypeStruct((bsz, SUB, d, BLK_LANES), BF16)),
        grid=(bsz, n_lp),
        in_specs=[
            pl.BlockSpec((None, L_PAIR, N_SEG, SUBS_PER_SEG, d), lambda b, p: (b, p, 0, 0, 0)),
            pl.BlockSpec((None, None, 1, 3 * d), lambda b, p: (layer, b, 0, 0)),
            pl.BlockSpec((None, 1, d), lambda b, p: (layer, 0, 0)),
            pl.BlockSpec((None, d, d), lambda b, p: (layer, 0, 0)),
            pl.BlockSpec((None, d, d), lambda b, p: (layer, 1, 0)),
            pl.BlockSpec((BLK_LANES, BLK_LANES), lambda b, p: (0, 0)),
        ],
        out_specs=(
            pl.BlockSpec((SSM_GROUPS, L_PAIR * SSM_GROUP, BLK_LANES), lambda b, p: (0, p, b)),
            pl.BlockSpec((None, L_PAIR, d, BLK_LANES), lambda b, p: (b, p, 0, 0)),
        ),
        compiler_params=pltpu.CompilerParams(
            dimension_semantics=("parallel", "parallel"),
            vmem_limit_bytes=VMEM_LIMIT),
        name="s5_in",
    )(xp, mod_all, gain_all, w_ssm_t, w_ssm_t, qperm)


def _s5_core_kernel(zt_ref, toept_ref, ws_ref, wot_ref, atab_ref, seg_ref, dcol_ref,
                    yt_ref, loc_ref, sprev_ref, *, bsz):
    gs = S5_GROUPS_PER_STEP
    half = STATE_HALF
    row8 = lax.broadcasted_iota(jnp.int32, (N_SEG, half), 0)

    def shift_down(v, n):
        return jnp.where(row8 >= n, pltpu.roll(v, n, axis=0), 0.0)

    for j in range(gs):
        loc_ref[j] = _dot_tn(zt_ref[j], ws_ref[j])
        a_re = atab_ref[j, 0:8, :]
        a_im = atab_ref[j, 8:16, :]
        for b in range(bsz):
            base = b * BLK_LANES

            def rows(kl, base=base):
                return pl.ds(base + kl * N_SEG, N_SEG)

            e_re = jnp.zeros((N_SEG, half), F32)
            e_im = jnp.zeros((N_SEG, half), F32)
            for kl in range(SUBS_PER_SEG):
                e_re, e_im = _cmul_add(a_re, a_im, e_re, e_im,
                                       loc_ref[j, rows(kl), 0:half],
                                       loc_ref[j, rows(kl), half:STATE_LANES])
            for i, n in enumerate((1, 2, 4)):
                p_re = seg_ref[j, i:i + 1, :]
                p_im = seg_ref[j, 3 + i:4 + i, :]
                e_re, e_im = _cmul_add(p_re, p_im, shift_down(e_re, n), shift_down(e_im, n),
                                       e_re, e_im)
            s_re = shift_down(e_re, 1)
            s_im = shift_down(e_im, 1)
            for kl in range(SUBS_PER_SEG):
                sprev_ref[j, rows(kl), 0:half] = s_re
                sprev_ref[j, rows(kl), half:STATE_LANES] = s_im
                s_re, s_im = _cmul_add(a_re, a_im, s_re, s_im,
                                       loc_ref[j, rows(kl), 0:half],
                                       loc_ref[j, rows(kl), half:STATE_LANES])

        zt = zt_ref[j]
        out = (_dot(toept_ref[j], zt)
               + _dot_nt(wot_ref[j], sprev_ref[j].astype(BF16))
               + dcol_ref[j] * zt.astype(F32))
        yt_ref[j] = out.astype(BF16)


def _s5_core(zt, toept, ws, wot, atab, segtab, dcol, bsz, layer):
    g, rows, lanes = zt.shape
    gs = S5_GROUPS_PER_STEP
    off = layer * (g // gs)
    wspec = pl.BlockSpec((gs, S5_ROW, STATE_LANES), lambda i: (off + i, 0, 0))
    return pl.pallas_call(
        functools.partial(_s5_core_kernel, bsz=bsz),
        out_shape=jax.ShapeDtypeStruct((g, rows, lanes), BF16),
        grid=(g // gs,),
        in_specs=[
            pl.BlockSpec((gs, rows, lanes), lambda i: (i, 0, 0)),
            wspec, wspec, wspec,
            pl.BlockSpec((gs, 16, STATE_HALF), lambda i: (off + i, 0, 0)),
            pl.BlockSpec((gs, 8, STATE_HALF), lambda i: (off + i, 0, 0)),
            pl.BlockSpec((gs, S5_ROW, 1), lambda i: (off + i, 0, 0)),
        ],
        out_specs=pl.BlockSpec((gs, rows, lanes), lambda i: (i, 0, 0)),
        scratch_shapes=[pltpu.VMEM((gs, lanes, STATE_LANES), F32),
                        pltpu.VMEM((gs, lanes, STATE_LANES), F32)],
        compiler_params=pltpu.CompilerParams(
            dimension_semantics=("parallel",),
            vmem_limit_bytes=VMEM_LIMIT),
        name="s5_core",
    )(zt, toept, ws, wot, atab, segtab, dcol)


def _log_sigmoid(x):
    return -(jnp.maximum(-x, 0.0) + jnp.log1p(jnp.exp(-jnp.abs(x))))


def _chunk_time(idx):
    return ((idx & (SUBS_PER_SEG - 1)) * SUB) | (idx >> 4)


def _mlstm_kernel(x_ref, mod_ref, gain_ref, w3_ref, cw_ref, cb_ref,
                  wq_ref, wk_ref, wv_ref, wgt_ref, bgt_ref, ng_ref, skip_ref, o_ref,
                  ct_ref, m_ref, tail_ref, ext_ref, qkv_ref, proj_ref, xc_ref):
    L, dh, d = SEG_LEN, HEAD_DIM, D_MODEL
    halo = (CONV_WIDTH - 1) * SUBS_PER_SEG

    @pl.when(pl.program_id(1) == 0)
    def _():
        ct_ref[...] = jnp.zeros_like(ct_ref)
        m_ref[...] = jnp.zeros_like(m_ref)
        tail_ref[...] = jnp.zeros_like(tail_ref)

    x = x_ref[...].reshape(L, d)
    h = _modulated_norm(x, gain_ref[...], mod_ref[...]).astype(BF16)
    proj_ref[...] = _dot(h, w3_ref[...])
    m_in = proj_ref[:, 0:d]

    ext_ref[halo:halo + L, :] = m_in
    row16 = lax.broadcasted_iota(jnp.int32, (SUBS_PER_SEG, d), 0)
    for i in range(CONV_WIDTH - 1):
        lsrc = SUB - (CONV_WIDTH - 1) + i
        r0 = halo + lsrc * SUBS_PER_SEG
        shifted = ext_ref[pl.ds(r0 - 1, SUBS_PER_SEG), :]
        ext_ref[i * SUBS_PER_SEG:(i + 1) * SUBS_PER_SEG, :] = jnp.where(
            row16 == 0, tail_ref[i:i + 1, :], shifted)
        tail_ref[i:i + 1, :] = ext_ref[r0 + SUBS_PER_SEG - 1:r0 + SUBS_PER_SEG, :]
    acc = m_in * cw_ref[CONV_WIDTH - 1:CONV_WIDTH, :] + cb_ref[...]
    for j in range(CONV_WIDTH - 1):
        back = CONV_WIDTH - 1 - j
        acc = acc + ext_ref[pl.ds(halo - back * SUBS_PER_SEG, L), :] * cw_ref[j:j + 1, :]
    xc = acc * _sigmoid(acc)
    xc_ref[...] = xc

    xcb = xc.astype(BF16)
    minb = m_in.astype(BF16)
    for hd in range(HEADS):
        sl = slice(hd * dh, (hd + 1) * dh)
        qkv_ref[:, hd * dh:(hd + 1) * dh] = _dot(xcb[:, sl], wq_ref[hd]).astype(BF16)
        qkv_ref[:, d + hd * dh:d + (hd + 1) * dh] = (
            _dot(xcb[:, sl], wk_ref[hd]) * (dh ** -0.5)).astype(BF16)
        qkv_ref[:, 2 * d + hd * dh:2 * d + (hd + 1) * dh] = (
            _dot(minb[:, sl], wv_ref[hd]).astype(BF16))

    gates_t = _dot_nt(wgt_ref[...], qkv_ref[...]) + bgt_ref[...]

    t_row = _chunk_time(lax.broadcasted_iota(jnp.int32, (L, L), 0))
    t_col = _chunk_time(lax.broadcasted_iota(jnp.int32, (L, L), 1))
    causal = t_col <= t_row
    tri_u = jnp.where(t_row <= t_col, 1.0, 0.0).astype(BF16)

    lft_hi, lft_lo = _split_hi_lo(_log_sigmoid(gates_t))
    b_rows = _dot(lft_hi, tri_u) + _dot(lft_lo, tri_u)
    stacked = jnp.concatenate([gates_t, b_rows, jnp.zeros((128 - 16, L), F32)], axis=0)
    cols = stacked.T

    lane = lax.broadcasted_iota(jnp.int32, (L, 128), 1)
    ones_blk = jnp.where(lane == 0, 1.0, 0.0).astype(BF16)

    for hd in range(HEADS):
        sl = slice(hd * dh, (hd + 1) * dh)
        i_r = gates_t[hd:hd + 1, :]
        b_r = b_rows[HEADS + hd:HEADS + hd + 1, :]
        i_c = cols[:, hd:hd + 1]
        b_c = cols[:, 8 + HEADS + hd:8 + HEADS + hd + 1]
        m_prev = m_ref[hd:hd + 1, 0:1]

        log_d = jnp.where(causal, b_c + (i_r - b_r), -jnp.inf)
        m_inter = b_c + m_prev
        m_t = jnp.maximum(m_inter, jnp.max(log_d, axis=-1, keepdims=True))
        dmat = jnp.exp(log_d - m_t)
        qb = qkv_ref[:, sl]
        kb = qkv_ref[:, d + hd * dh:d + (hd + 1) * dh]
        s = _dot_nt(qb, kb)
        p = (s * dmat).astype(BF16)
        v_aug = jnp.concatenate([qkv_ref[:, 2 * d + hd * dh:2 * d + (hd + 1) * dh], ones_blk],
                                axis=-1)
        w_inter = jnp.exp(m_inter - m_t)
        inter = _dot(qb, ct_ref[hd].astype(BF16))
        intra = _dot(p, v_aug)
        nd = w_inter * inter + intra
        num = nd[:, 0:dh]
        den = nd[:, dh:dh + 1]
        hh = num / jnp.maximum(jnp.abs(den), jnp.exp(-m_t))

        b_tot = b_c[L - 1:L, :]
        log_w = b_tot - b_c + i_c
        m_next = jnp.maximum(b_tot + m_prev, jnp.max(log_w, axis=0, keepdims=True))
        decay = jnp.exp(b_tot + m_prev - m_next)
        w_c = jnp.exp(log_w - m_next)
        upd = _dot(kb.T, (w_c * v_aug.astype(F32)).astype(BF16))
        ct_ref[hd] = decay * ct_ref[hd] + upd
        m_ref[hd:hd + 1, :] = jnp.broadcast_to(m_next, (1, 128))

        hg = hh * _sigmoid(proj_ref[:, d + hd * dh:d + (hd + 1) * dh])
        mu = jnp.mean(hg, axis=-1, keepdims=True)
        dev = hg - mu
        var = jnp.mean(dev * dev, axis=-1, keepdims=True)
        hn = dev * lax.rsqrt(var + EPS) * ng_ref[:, sl] + skip_ref[:, sl] * xc_ref[:, sl]
        gate = proj_ref[:, 2 * d + hd * dh:2 * d + (hd + 1) * dh]
        y = hn * (gate * _sigmoid(gate))
        o_ref[:, :, sl] = y.reshape(SUB, SUBS_PER_SEG, dh).astype(o_ref.dtype)


def _mlstm(xp, mod_all, gain_all, w3, cw, cb, wq, wk, wv, wgt, bgt, ng, skip, layer):
    bsz = xp.shape[0]
    d = D_MODEL
    L = SEG_LEN
    lay2 = lambda shape: pl.BlockSpec((None,) + shape, lambda b, c: (layer, 0, 0))
    lay3 = lambda shape: pl.BlockSpec((None,) + shape, lambda b, c: (layer, 0, 0, 0))
    seg_spec = pl.BlockSpec((None, SUB, None, SUBS_PER_SEG, d), lambda b, c: (b, 0, c, 0, 0))
    halo = (CONV_WIDTH - 1) * SUBS_PER_SEG
    return pl.pallas_call(
        _mlstm_kernel,
        out_shape=jax.ShapeDtypeStruct(xp.shape, BF16),
        grid=(bsz, N_SEG),
        in_specs=[
            seg_spec,
            pl.BlockSpec((None, None, 1, 3 * d), lambda b, c: (layer, b, 0, 0)),
            lay2((1, d)),
            lay2((d, 3 * d)),
            lay2((CONV_WIDTH, d)),
            lay2((1, d)),
            lay3((HEADS, HEAD_DIM, HEAD_DIM)),
            lay3((HEADS, HEAD_DIM, HEAD_DIM)),
            lay3((HEADS, HEAD_DIM, HEAD_DIM)),
            lay2((8, 3 * d)),
            lay2((8, 1)),
            lay2((1, d)),
            lay2((1, d)),
        ],
        out_specs=seg_spec,
        scratch_shapes=[
            pltpu.VMEM((HEADS, HEAD_DIM, M_AUG), F32),
            pltpu.VMEM((8, 128), F32),
            pltpu.VMEM((8, d), F32),
            pltpu.VMEM((halo + L, d), F32),
            pltpu.VMEM((L, 3 * d), BF16),
            pltpu.VMEM((L, 3 * d), F32),
            pltpu.VMEM((L, d), F32),
        ],
        compiler_params=pltpu.CompilerParams(
            dimension_semantics=("parallel", "arbitrary"),
            vmem_limit_bytes=VMEM_LIMIT),
        name="mlstm",
    )(xp, mod_all, gain_all, w3, cw, cb, wq, wk, wv, wgt, bgt, ng, skip)


M_CHAINS = 2


def _mlstm2_kernel(x_ref, mod_ref, gain_ref, w3_ref, cw_ref, cb_ref,
                   wq_ref, wk_ref, wv_ref, wg_ref, bg_ref, ng_ref, skip_ref, o_ref,
                   ct_ref, m_ref, tail_ref, ext_ref, qkv_ref, proj_ref, xc_ref, hn_ref, cvb_ref):
    L, dh, d = SEG_LEN, HEAD_DIM, D_MODEL
    halo = (CONV_WIDTH - 1) * SUBS_PER_SEG

    @pl.when(pl.program_id(1) == 0)
    def _():
        ct_ref[...] = jnp.zeros_like(ct_ref)
        m_ref[...] = jnp.zeros_like(m_ref)
        tail_ref[...] = jnp.zeros_like(tail_ref)

    t_row = _chunk_time(lax.broadcasted_iota(jnp.int32, (L, L), 0))
    t_col = _chunk_time(lax.broadcasted_iota(jnp.int32, (L, L), 1))
    causal = t_col <= t_row
    tri_l = jnp.where(causal, 1.0, 0.0).astype(BF16)
    lane = lax.broadcasted_iota(jnp.int32, (L, 128), 1)
    ones_blk = jnp.where(lane == 0, 1.0, 0.0).astype(BF16)
    row16 = lax.broadcasted_iota(jnp.int32, (SUBS_PER_SEG, d), 0)

    def stage_norm_in(c):
        x = x_ref[c].reshape(L, d)
        hn_ref[c] = _modulated_norm(x, gain_ref[...], mod_ref[c]).astype(BF16)
        proj_ref[c, :, 0:d] = _dot(hn_ref[c], w3_ref[:, 0:d])

    def stage_gate_proj(c):
        proj_ref[c, :, d:2 * d] = _dot(hn_ref[c], w3_ref[:, d:2 * d])
        proj_ref[c, :, 2 * d:3 * d] = _dot(hn_ref[c], w3_ref[:, 2 * d:3 * d])

    def stage_conv(c):
        m_in = proj_ref[c, :, 0:d]
        ext_ref[c, halo:halo + L, :] = m_in
        cvb_ref[c, :, d:2 * d] = m_in.astype(BF16)
        for i in range(CONV_WIDTH - 1):
            lsrc = SUB - (CONV_WIDTH - 1) + i
            r0 = halo + lsrc * SUBS_PER_SEG
            shifted = ext_ref[c, pl.ds(r0 - 1, SUBS_PER_SEG), :]
            ext_ref[c, i * SUBS_PER_SEG:(i + 1) * SUBS_PER_SEG, :] = jnp.where(
                row16 == 0, tail_ref[c, i:i + 1, :], shifted)
            tail_ref[c, i:i + 1, :] = ext_ref[c, r0 + SUBS_PER_SEG - 1:r0 + SUBS_PER_SEG, :]
        acc = m_in * cw_ref[CONV_WIDTH - 1:CONV_WIDTH, :] + cb_ref[...]
        for j in range(CONV_WIDTH - 1):
            back = CONV_WIDTH - 1 - j
            acc = acc + ext_ref[c, pl.ds(halo - back * SUBS_PER_SEG, L), :] * cw_ref[j:j + 1, :]
        xc = acc * _sigmoid(acc)
        xc_ref[c] = xc
        cvb_ref[c, :, 0:d] = xc.astype(BF16)

    def stage_qkv(c):
        for hd in range(HEADS):
            sl = slice(hd * dh, (hd + 1) * dh)
            xcb = cvb_ref[c, :, hd * dh:(hd + 1) * dh]
            qkv_ref[c, :, hd * dh:(hd + 1) * dh] = _dot(xcb, wq_ref[hd]).astype(BF16)
            qkv_ref[c, :, d + hd * dh:d + (hd + 1) * dh] = (
                _dot(xcb, wk_ref[hd]) * (dh ** -0.5)).astype(BF16)
            qkv_ref[c, :, 2 * d + hd * dh:2 * d + (hd + 1) * dh] = (
                _dot(cvb_ref[c, :, d + hd * dh:d + (hd + 1) * dh], wv_ref[hd]).astype(BF16))
        gates = _dot(qkv_ref[c], wg_ref[...]) + bg_ref[...]
        log_f = _log_sigmoid(gates)
        lf_hi, lf_lo = _split_hi_lo(log_f)
        b_cols = _dot(tri_l, lf_hi) + _dot(tri_l, lf_lo)
        return gates.T, b_cols.T, gates, b_cols

    def stage_head(c, hd, gate_forms):
        gates_t, b_rows, gates, b_cols = gate_forms
        sl = slice(hd * dh, (hd + 1) * dh)
        i_r = gates_t[hd:hd + 1, :]
        b_r = b_rows[HEADS + hd:HEADS + hd + 1, :]
        i_c = gates[:, hd:hd + 1]
        b_c = b_cols[:, HEADS + hd:HEADS + hd + 1]
        m_prev = m_ref[c, hd:hd + 1, 0:1]

        log_d = jnp.where(causal, b_c + (i_r - b_r), -jnp.inf)
        m_inter = b_c + m_prev
        m_t = jnp.maximum(m_inter, jnp.max(log_d, axis=-1, keepdims=True))
        dmat = jnp.exp(log_d - m_t)
        qb = qkv_ref[c, :, sl]
        kb = qkv_ref[c, :, d + hd * dh:d + (hd + 1) * dh]
        s = _dot_nt(qb, kb)
        p = (s * dmat).astype(BF16)
        v_aug = jnp.concatenate([qkv_ref[c, :, 2 * d + hd * dh:2 * d + (hd + 1) * dh], ones_blk],
                                axis=-1)
        w_inter = jnp.exp(m_inter - m_t)
        inter = _dot(qb, ct_ref[c, hd].astype(BF16))
        intra = _dot(p, v_aug)
        nd = w_inter * inter + intra
        num = nd[:, 0:dh]
        den = nd[:, dh:dh + 1]
        hh = num / jnp.maximum(jnp.abs(den), jnp.exp(-m_t))

        b_tot = b_c[L - 1:L, :]
        log_w = b_tot - b_c + i_c
        m_next = jnp.maximum(b_tot + m_prev, jnp.max(log_w, axis=0, keepdims=True))
        decay = jnp.exp(b_tot + m_prev - m_next)
        w_c = jnp.exp(log_w - m_next)
        upd = _dot(kb.T, (w_c * v_aug.astype(F32)).astype(BF16))
        ct_ref[c, hd] = decay * ct_ref[c, hd] + upd
        m_ref[c, hd:hd + 1, :] = jnp.broadcast_to(m_next, (1, 128))

        hg = hh * _sigmoid(proj_ref[c, :, d + hd * dh:d + (hd + 1) * dh])
        mu = jnp.mean(hg, axis=-1, keepdims=True)
        dev = hg - mu
        var = jnp.mean(dev * dev, axis=-1, keepdims=True)
        hn = dev * lax.rsqrt(var + EPS) * ng_ref[:, sl] + skip_ref[:, sl] * xc_ref[c, :, sl]
        gate = proj_ref[c, :, 2 * d + hd * dh:2 * d + (hd + 1) * dh]
        y = hn * (gate * _sigmoid(gate))
        o_ref[c, :, :, sl] = y.reshape(SUB, SUBS_PER_SEG, dh).astype(o_ref.dtype)

    a, b = 0, 1
    stage_norm_in(a)
    stage_conv(a)
    stage_norm_in(b)
    stage_gate_proj(a)
    gf_a = stage_qkv(a)
    stage_conv(b)
    stage_gate_proj(b)
    stage_head(a, 0, gf_a)
    gf_b = stage_qkv(b)
    stage_head(a, 1, gf_a)
    stage_head(b, 0, gf_b)
    stage_head(a, 2, gf_a)
    stage_head(b, 1, gf_b)
    stage_head(a, 3, gf_a)
    stage_head(b, 2, gf_b)
    stage_head(b, 3, gf_b)


def _mlstm2(xp, mod_all, gain_all, w3, cw, cb, wq, wk, wv, wg, bg, ng, skip, layer):
    bsz = xp.shape[0]
    d = D_MODEL
    L = SEG_LEN
    nc = M_CHAINS
    lay2 = lambda shape: pl.BlockSpec((None,) + shape, lambda b, c: (layer, 0, 0))
    lay3 = lambda shape: pl.BlockSpec((None,) + shape, lambda b, c: (layer, 0, 0, 0))
    seg_spec = pl.BlockSpec((nc, SUB, None, SUBS_PER_SEG, d), lambda b, c: (b, 0, c, 0, 0))
    halo = (CONV_WIDTH - 1) * SUBS_PER_SEG
    return pl.pallas_call(
        _mlstm2_kernel,
        out_shape=jax.ShapeDtypeStruct(xp.shape, BF16),
        grid=(bsz // nc, N_SEG),
        in_specs=[
            seg_spec,
            pl.BlockSpec((None, nc, 1, 3 * d), lambda b, c: (layer, b, 0, 0)),
            lay2((1, d)),
            lay2((d, 3 * d)),
            lay2((CONV_WIDTH, d)),
            lay2((1, d)),
            lay3((HEADS, HEAD_DIM, HEAD_DIM)),
            lay3((HEADS, HEAD_DIM, HEAD_DIM)),
            lay3((HEADS, HEAD_DIM, HEAD_DIM)),
            lay2((3 * d, 128)),
            lay2((1, 128)),
            lay2((1, d)),
            lay2((1, d)),
        ],
        out_specs=seg_spec,
        scratch_shapes=[
            pltpu.VMEM((nc, HEADS, HEAD_DIM, M_AUG), F32),
            pltpu.VMEM((nc, 8, 128), F32),
            pltpu.VMEM((nc, 8, d), F32),
            pltpu.VMEM((nc, halo + L, d), F32),
            pltpu.VMEM((nc, L, 3 * d), BF16),
            pltpu.VMEM((nc, L, 3 * d), F32),
            pltpu.VMEM((nc, L, d), F32),
            pltpu.VMEM((nc, L, d), BF16),
            pltpu.VMEM((nc, L, 2 * d), BF16),
        ],
        compiler_params=pltpu.CompilerParams(
            dimension_semantics=("parallel", "arbitrary"),
            vmem_limit_bytes=VMEM_LIMIT),
        name="mlstm",
    )(xp, mod_all, gain_all, w3, cw, cb, wq, wk, wv, wg, bg, ng, skip)


def _gelu_tanh(x):
    inner = math.sqrt(2.0 / math.pi) * (x + 0.044715 * (x * x * x))
    return x * (0.5 * (1.0 + jnp.tanh(inner)))


def _out_kernel(yt_ref, gt_ref, my_ref, x_ref, mod_ref, bglu_ref, og_ref, fg_ref,
                wglut_ref, wout_ref, qt_ref, o_ref, *, final):
    d = D_MODEL
    rows = L_PAIR * BLK_LANES
    out_m = _dot(my_ref[...].reshape(rows, d), wout_ref[d:2 * d, :])
    parts = []
    for l in range(L_PAIR):
        yl = yt_ref[:, l * SSM_GROUP:(l + 1) * SSM_GROUP, :].reshape(d, BLK_LANES)
        parts.append(_dot(yl, qt_ref[...]))
    y = _gelu_tanh(jnp.concatenate(parts, axis=-1))
    glu = y * _sigmoid(_dot(wglut_ref[...], y.astype(BF16)) + bglu_ref[...])
    gate_s = jnp.concatenate([gt_ref[l] for l in range(L_PAIR)], axis=-1).astype(F32)
    ms = jnp.mean(glu * glu, axis=0, keepdims=True)
    ssm_y = glu * lax.rsqrt(ms + EPS) * og_ref[...] * gate_s
    out = _dot_tn(ssm_y.astype(BF16), wout_ref[0:d, :]) + out_m
    gate = mod_ref[:, 2 * d:3 * d]
    xn = x_ref[...].reshape(rows, d) + gate * out
    if final:
        xn = _rms(xn) * fg_ref[...]
    o_ref[...] = xn.reshape(L_PAIR, N_SEG, SUBS_PER_SEG, d)


def _out_stage(yt, gt, my, xp, mod_all, bglu_col, og_col, fg, wglut, wout, qperm_t, layer, final):
    bsz = xp.shape[0]
    d = D_MODEL
    n_lp = SUB // L_PAIR
    tok = pl.BlockSpec((None, L_PAIR, N_SEG, SUBS_PER_SEG, d), lambda b, p: (b, p, 0, 0, 0))
    col = pl.BlockSpec((None, d, 1), lambda b, p: (layer, 0, 0))
    return pl.pallas_call(
        functools.partial(_out_kernel, final=final),
        out_shape=jax.ShapeDtypeStruct(xp.shape, F32),
        grid=(bsz, n_lp),
        in_specs=[
            pl.BlockSpec((SSM_GROUPS, L_PAIR * SSM_GROUP, BLK_LANES), lambda b, p: (0, p, b)),
            pl.BlockSpec((None, L_PAIR, d, BLK_LANES), lambda b, p: (b, p, 0, 0)),
            tok, tok,
            pl.BlockSpec((None, None, 1, 3 * d), lambda b, p: (layer, b, 0, 0)),
            col, col,
            pl.BlockSpec((1, d), lambda b, p: (0, 0)),
            pl.BlockSpec((None, d, d), lambda b, p: (layer, 0, 0)),
            pl.BlockSpec((None, 2 * d, d), lambda b, p: (layer, 0, 0)),
            pl.BlockSpec((BLK_LANES, BLK_LANES), lambda b, p: (0, 0)),
        ],
        out_specs=tok,
        compiler_params=pltpu.CompilerParams(
            dimension_semantics=("parallel", "parallel"),
            vmem_limit_bytes=VMEM_LIMIT),
        name="out_stage",
    )(yt, gt, my, xp, mod_all, bglu_col, og_col, fg, wglut, wout, qperm_t)


def _lane_permutation():
    q = np.zeros((BLK_LANES, BLK_LANES), np.float32)
    for seg in range(N_SEG):
        for kl in range(SUBS_PER_SEG):
            q[seg * SUBS_PER_SEG + kl, kl * N_SEG + seg] = 1.0
    return q


def kernel(x, c, norm_gain, w_mod, b_mod, w_in, ssm_lambda_re, ssm_lambda_im, ssm_log_dt,
           ssm_b_re, ssm_b_im, ssm_c_re, ssm_c_im, ssm_d, ssm_w_glu, ssm_b_glu,
           ssm_out_gain, m_conv_w, m_conv_b, m_wq, m_wk, m_wv, m_w_gates, m_b_igate,
           m_b_fgate, m_norm_gain, m_skip, w_out, final_gain):
    bsz, seq, d = x.shape
    depth = w_in.shape[0]
    assert d == D_MODEL and seq == SEQ

    xp = x.reshape(bsz, N_SEG, SUBS_PER_SEG, SUB, d).transpose(0, 3, 1, 2, 4)
    mod_all = _modulation(c, w_mod, b_mod).reshape(depth, bsz, 1, 3 * d)
    rows = lambda v: v.reshape(depth, 1, -1)
    cols = lambda v: v.reshape(depth, -1, 1)
    qperm = jnp.asarray(_lane_permutation(), BF16)
    qperm_t = jnp.asarray(_lane_permutation().T, BF16)

    gain_all = rows(norm_gain)
    w_ssm_t = w_in[:, :, 0:2 * d].transpose(0, 2, 1).astype(BF16)
    w3 = w_in[:, :, 2 * d:5 * d].astype(BF16)
    toept, ws, wot, atab, segtab = _s5_prep(ssm_lambda_re, ssm_lambda_im, ssm_log_dt,
                                            ssm_b_re, ssm_b_im, ssm_c_re, ssm_c_im)
    dcol = jnp.broadcast_to(ssm_d.reshape(depth * SSM_GROUPS, 1, SSM_GROUP),
                            (depth * SSM_GROUPS, SUB, SSM_GROUP)).reshape(-1, S5_ROW, 1)
    wq, wk, wv = m_wq.astype(BF16), m_wk.astype(BF16), m_wv.astype(BF16)
    gate_pad = ((0, 0), (0, 0), (0, 128 - 2 * HEADS))
    wg = jnp.pad(m_w_gates, gate_pad).astype(BF16)
    bg = jnp.pad(jnp.concatenate([m_b_igate, m_b_fgate], axis=-1).reshape(depth, 1, 2 * HEADS),
                 gate_pad)
    wglut = ssm_w_glu.transpose(0, 2, 1).astype(BF16)
    wout = w_out.astype(BF16)
    fg = final_gain.reshape(1, d)

    for l in range(depth):
        zt, gt = _s5_in(xp, mod_all, gain_all, w_ssm_t, qperm, l)
        yt = _s5_core(zt, toept, ws, wot, atab, segtab, dcol, bsz, l)
        my = _mlstm2(xp, mod_all, gain_all, w3, m_conv_w, rows(m_conv_b), wq, wk, wv, wg, bg,
                    rows(m_norm_gain), rows(m_skip), l)
        xp = _out_stage(yt, gt, my, xp, mod_all, cols(ssm_b_glu), cols(ssm_out_gain), fg,
                        wglut, wout, qperm_t, l, final=(l == depth - 1))

    return xp.transpose(0, 2, 3, 1, 4).reshape(bsz, seq, d)
```

```python
import functools
import math

import numpy as np
import jax
import jax.numpy as jnp
from jax import lax
from jax.experimental import pallas as pl
from jax.experimental.pallas import tpu as pltpu

F32 = jnp.float32
BF16 = jnp.bfloat16

D_MODEL = 1024
SSM_GROUP = 16
SSM_GROUPS = D_MODEL // SSM_GROUP
SSM_STATE = 64
HEADS = 4
HEAD_DIM = D_MODEL // HEADS
CONV_WIDTH = 4
EPS = 1e-6

SUB = 16
SUBS_PER_SEG = 16
SEG_LEN = SUB * SUBS_PER_SEG
N_SEG = 8
SEQ = N_SEG * SEG_LEN
BLK_LANES = N_SEG * SUBS_PER_SEG
S5_ROW = SUB * SSM_GROUP
STATE_HALF = 128
STATE_LANES = 2 * STATE_HALF
S5_GROUPS_PER_STEP = 4
L_PAIR = 2
OUT_STREAMS = 2
M_AUG = HEAD_DIM + 128
VMEM_LIMIT = 56 * 1024 * 1024

_NT = (((1,), (1,)), ((), ()))
_TN = (((0,), (0,)), ((), ()))


def _dot(a, b):
    return jnp.dot(a, b, preferred_element_type=F32)


def _dot_nt(a, b):
    return lax.dot_general(a, b, _NT, preferred_element_type=F32)


def _dot_tn(a, b):
    return lax.dot_general(a, b, _TN, preferred_element_type=F32)


def _sigmoid(x):
    return jax.nn.sigmoid(x)


def _rms(x):
    return x * lax.rsqrt(jnp.mean(x * x, axis=-1, keepdims=True) + EPS)


def _modulated_norm(x, gain, mod):
    shift = mod[:, 0:D_MODEL]
    scale = mod[:, D_MODEL:2 * D_MODEL]
    return _rms(x) * gain * (1.0 + scale) + shift


def _split_hi_lo(v):
    hi = v.astype(BF16)
    lo = (v - hi.astype(F32)).astype(BF16)
    return hi, lo


def _dot_nt_f32(a, b):
    a_hi, a_lo = _split_hi_lo(a)
    b_hi, b_lo = _split_hi_lo(b)
    return _dot_nt(a_hi, b_hi) + _dot_nt(a_hi, b_lo) + _dot_nt(a_lo, b_hi)


def _cmul(a_re, a_im, b_re, b_im):
    return a_re * b_re - a_im * b_im, a_re * b_im + a_im * b_re


def _cmul_add(p_re, p_im, s_re, s_im, add_re, add_im):
    return (p_re * s_re - p_im * s_im + add_re,
            p_re * s_im + p_im * s_re + add_im)


def _mod_kernel(c_ref, w_ref, b_ref, o_ref):
    cv = c_ref[...]
    act = cv * _sigmoid(cv)
    o_ref[...] = _dot(act.astype(BF16), w_ref[...].astype(BF16)) + b_ref[...]


def _modulation(c, w_mod, b_mod):
    depth, d, n = w_mod.shape
    bsz = c.shape[0]
    return pl.pallas_call(
        _mod_kernel,
        out_shape=jax.ShapeDtypeStruct((depth, bsz, n), F32),
        grid=(depth, n // d),
        in_specs=[
            pl.BlockSpec((bsz, d), lambda l, j: (0, 0)),
            pl.BlockSpec((None, d, d), lambda l, j: (l, 0, j)),
            pl.BlockSpec((None, 1, d), lambda l, j: (l, 0, j)),
        ],
        out_specs=pl.BlockSpec((None, bsz, d), lambda l, j: (l, 0, j)),
        compiler_params=pltpu.CompilerParams(
            dimension_semantics=("parallel", "parallel"),
            vmem_limit_bytes=VMEM_LIMIT),
        name="adaln_mod",
    )(c, w_mod, b_mod.reshape(depth, 1, n))


def _s5_prep_kernel(lre_ref, lim_ref, ldt_ref, btre_ref, btim_ref, cre_ref, cim_ref,
                    toept_ref, ws_ref, wot_ref, atab_ref, seg_ref):
    gp = S5_GROUPS_PER_STEP
    lane_blk = lax.broadcasted_iota(jnp.int32, (S5_ROW, S5_ROW), 1) // SSM_GROUP
    row8 = lax.broadcasted_iota(jnp.int32, (8, STATE_HALF), 0)
    for j in range(gp):
        lr = lre_ref[j]
        li = lim_ref[j]
        dt = jnp.exp(ldt_ref[j])
        mag = jnp.exp(lr * dt)
        a_re = mag * jnp.cos(li * dt)
        a_im = mag * jnp.sin(li * dt)
        inv = 1.0 / (lr * lr + li * li)
        k_re = ((a_re - 1.0) * lr + a_im * li) * inv
        k_im = (a_im * lr - (a_re - 1.0) * li) * inv
        bb_re, bb_im = _cmul(k_re, k_im, btre_ref[j], btim_ref[j])
        c_re = cre_ref[j]
        c_im = cim_ref[j]

        pw = [(jnp.ones_like(a_re), jnp.zeros_like(a_re))]
        for _ in range(SUB):
            pw.append(_cmul(pw[-1][0], pw[-1][1], a_re, a_im))

        ws_rows, wot_rows, ca_re, ca_im = [], [], [], []
        for l in range(SUB):
            w_re, w_im = _cmul(pw[SUB - 1 - l][0], pw[SUB - 1 - l][1], bb_re, bb_im)
            ws_rows.append(jnp.concatenate([w_re, w_im], axis=-1))
            o_re, o_im = _cmul(c_re, c_im, pw[l + 1][0], pw[l + 1][1])
            wot_rows.append(jnp.concatenate([o_re, -o_im], axis=-1))
            g_re, g_im = _cmul(c_re, c_im, pw[l][0], pw[l][1])
            ca_re.append(g_re)
            ca_im.append(g_im)
        ws_ref[j] = jnp.concatenate(ws_rows, axis=0).astype(BF16)
        wot_ref[j] = jnp.concatenate(wot_rows, axis=0).astype(BF16)

        bbt_re = jnp.concatenate([bb_re] * SUB, axis=0)
        bbt_im = jnp.concatenate([bb_im] * SUB, axis=0)
        kw = (_dot_nt_f32(jnp.concatenate(ca_re, axis=0), bbt_re)
              - _dot_nt_f32(jnp.concatenate(ca_im, axis=0), bbt_im))
        toep = jnp.where(lane_blk == 0, kw, 0.0)
        for lp in range(1, SUB):
            shifted = jnp.concatenate(
                [jnp.zeros((SSM_GROUP * lp, S5_ROW), F32), kw[0:S5_ROW - SSM_GROUP * lp]], axis=0)
            toep = jnp.where(lane_blk == lp, shifted, toep)
        toept_ref[j] = toep.astype(BF16)

        s_re, s_im = pw[SUB]
        atab_ref[j] = jnp.concatenate([jnp.broadcast_to(s_re, (8, STATE_HALF)),
                                       jnp.broadcast_to(s_im, (8, STATE_HALF))], axis=0)
        for _ in range(4):
            s_re, s_im = _cmul(s_re, s_im, s_re, s_im)
        seg = jnp.zeros((8, STATE_HALF), F32)
        for i in range(3):
            seg = jnp.where(row8 == i, s_re, jnp.where(row8 == 3 + i, s_im, seg))
            s_re, s_im = _cmul(s_re, s_im, s_re, s_im)
        seg_ref[j] = seg


def _s5_prep(lam_re, lam_im, log_dt, b_re, b_im, c_re, c_im):
    depth, g, p = lam_re.shape
    cg = SSM_GROUP
    n = depth * g
    gp = S5_GROUPS_PER_STEP
    lane_pad = ((0, 0), (0, 0), (0, STATE_HALF - p))

    def state_rows(v, fill):
        v = v.reshape(n, -1, p)
        return jnp.pad(v, lane_pad, constant_values=fill)

    args = (state_rows(lam_re, -1.0), state_rows(lam_im, 0.0), log_dt.reshape(n, 1, 1),
            state_rows(b_re.transpose(0, 1, 3, 2), 0.0), state_rows(b_im.transpose(0, 1, 3, 2), 0.0),
            state_rows(c_re, 0.0), state_rows(c_im, 0.0))
    vec = pl.BlockSpec((gp, 1, STATE_HALF), lambda i: (i, 0, 0))
    mat = pl.BlockSpec((gp, cg, STATE_HALF), lambda i: (i, 0, 0))
    big = pl.BlockSpec((gp, S5_ROW, STATE_LANES), lambda i: (i, 0, 0))
    return pl.pallas_call(
        _s5_prep_kernel,
        out_shape=(jax.ShapeDtypeStruct((n, S5_ROW, S5_ROW), BF16),
                   jax.ShapeDtypeStruct((n, S5_ROW, STATE_LANES), BF16),
                   jax.ShapeDtypeStruct((n, S5_ROW, STATE_LANES), BF16),
                   jax.ShapeDtypeStruct((n, 16, STATE_HALF), F32),
                   jax.ShapeDtypeStruct((n, 8, STATE_HALF), F32)),
        grid=(n // gp,),
        in_specs=[vec, vec, pl.BlockSpec((gp, 1, 1), lambda i: (i, 0, 0)), mat, mat, mat, mat],
        out_specs=(big, big, big,
                   pl.BlockSpec((gp, 16, STATE_HALF), lambda i: (i, 0, 0)),
                   pl.BlockSpec((gp, 8, STATE_HALF), lambda i: (i, 0, 0))),
        compiler_params=pltpu.CompilerParams(
            dimension_semantics=("parallel",),
            vmem_limit_bytes=VMEM_LIMIT),
        name="s5_prep",
    )(*args)


def _s5_in_kernel(x_ref, mod_ref, gain_ref, wut_ref, wgt_ref, q_ref, zt_ref, gt_ref):
    d = D_MODEL

    def stage_norm(s):
        x = x_ref[s * L_PAIR:(s + 1) * L_PAIR].reshape(L_PAIR * BLK_LANES, d)
        return _modulated_norm(x, gain_ref[...], mod_ref[...]).astype(BF16)

    def stage_proj(s, h):
        ut = _dot_nt(wut_ref[...], h).astype(BF16)
        sgt = _dot_nt(wgt_ref[...], h)
        p = _dot(ut, q_ref[...]).astype(BF16)
        for l in range(L_PAIR):
            lanes = slice(l * BLK_LANES, (l + 1) * BLK_LANES)
            row0 = (s * L_PAIR + l) * SSM_GROUP
            zt_ref[:, row0:row0 + SSM_GROUP, :] = (
                p[:, lanes].reshape(SSM_GROUPS, SSM_GROUP, BLK_LANES))
            g = sgt[:, lanes]
            gt_ref[s * L_PAIR + l] = (g * _sigmoid(g)).astype(BF16)

    hs = [stage_norm(s) for s in range(OUT_STREAMS)]
    for s in range(OUT_STREAMS):
        stage_proj(s, hs[s])


def _s5_in(xp, mod_all, gain_all, w_ssm_t, qperm, layer):
    bsz = xp.shape[0]
    d = D_MODEL
    n_l = OUT_STREAMS * L_PAIR
    return pl.pallas_call(
        _s5_in_kernel,
        out_shape=(jax.ShapeDtypeStruct((SSM_GROUPS, S5_ROW, bsz * BLK_LANES), BF16),
                   jax.ShapeDtypeStruct((bsz, SUB, d, BLK_LANES), BF16)),
        grid=(bsz, SUB // n_l),
        in_specs=[
            pl.BlockSpec((None, n_l, N_SEG, SUBS_PER_SEG, d), lambda b, p: (b, p, 0, 0, 0)),
            pl.BlockSpec((None, None, 1, 3 * d), lambda b, p: (layer, b, 0, 0)),
            pl.BlockSpec((None, 1, d), lambda b, p: (layer, 0, 0)),
            pl.BlockSpec((None, d, d), lambda b, p: (layer, 0, 0)),
            pl.BlockSpec((None, d, d), lambda b, p: (layer, 1, 0)),
            pl.BlockSpec((L_PAIR * BLK_LANES, L_PAIR * BLK_LANES), lambda b, p: (0, 0)),
        ],
        out_specs=(
            pl.BlockSpec((SSM_GROUPS, n_l * SSM_GROUP, BLK_LANES), lambda b, p: (0, p, b)),
            pl.BlockSpec((None, n_l, d, BLK_LANES), lambda b, p: (b, p, 0, 0)),
        ),
        compiler_params=pltpu.CompilerParams(
            dimension_semantics=("parallel", "parallel"),
            vmem_limit_bytes=VMEM_LIMIT),
        name="s5_in",
    )(xp, mod_all, gain_all, w_ssm_t, w_ssm_t, qperm)


def _s5_core_kernel(zt_ref, toept_ref, ws_ref, wot_ref, atab_ref, seg_ref, dcol_ref,
                    yt_ref, loc_ref, sprev_ref, *, bsz):
    gs = S5_GROUPS_PER_STEP
    half = STATE_HALF
    row8 = lax.broadcasted_iota(jnp.int32, (N_SEG, half), 0)

    def shift_down(v, n):
        return jnp.where(row8 >= n, pltpu.roll(v, n, axis=0), 0.0)

    for j in range(gs):
        loc_ref[j] = _dot_tn(zt_ref[j], ws_ref[j])
        a_re = atab_ref[j, 0:8, :]
        a_im = atab_ref[j, 8:16, :]
        for b in range(bsz):
            base = b * BLK_LANES

            def rows(kl, base=base):
                return pl.ds(base + kl * N_SEG, N_SEG)

            e_re = jnp.zeros((N_SEG, half), F32)
            e_im = jnp.zeros((N_SEG, half), F32)
            for kl in range(SUBS_PER_SEG):
                e_re, e_im = _cmul_add(a_re, a_im, e_re, e_im,
                                       loc_ref[j, rows(kl), 0:half],
                                       loc_ref[j, rows(kl), half:STATE_LANES])
            for i, n in enumerate((1, 2, 4)):
                p_re = seg_ref[j, i:i + 1, :]
                p_im = seg_ref[j, 3 + i:4 + i, :]
                e_re, e_im = _cmul_add(p_re, p_im, shift_down(e_re, n), shift_down(e_im, n),
                                       e_re, e_im)
            s_re = shift_down(e_re, 1)
            s_im = shift_down(e_im, 1)
            for kl in range(SUBS_PER_SEG):
                sprev_ref[j, rows(kl), 0:half] = s_re
                sprev_ref[j, rows(kl), half:STATE_LANES] = s_im
                s_re, s_im = _cmul_add(a_re, a_im, s_re, s_im,
                                       loc_ref[j, rows(kl), 0:half],
                                       loc_ref[j, rows(kl), half:STATE_LANES])

        zt = zt_ref[j]
        out = (_dot(toept_ref[j], zt)
               + _dot_nt(wot_ref[j], sprev_ref[j].astype(BF16))
               + dcol_ref[j] * zt.astype(F32))
        yt_ref[j] = out.astype(BF16)


def _s5_core(zt, toept, ws, wot, atab, segtab, dcol, bsz, layer):
    g, rows, lanes = zt.shape
    gs = S5_GROUPS_PER_STEP
    off = layer * (g // gs)
    wspec = pl.BlockSpec((gs, S5_ROW, STATE_LANES), lambda i: (off + i, 0, 0))
    return pl.pallas_call(
        functools.partial(_s5_core_kernel, bsz=bsz),
        out_shape=jax.ShapeDtypeStruct((g, rows, lanes), BF16),
        grid=(g // gs,),
        in_specs=[
            pl.BlockSpec((gs, rows, lanes), lambda i: (i, 0, 0)),
            wspec, wspec, wspec,
            pl.BlockSpec((gs, 16, STATE_HALF), lambda i: (off + i, 0, 0)),
            pl.BlockSpec((gs, 8, STATE_HALF), lambda i: (off + i, 0, 0)),
            pl.BlockSpec((gs, S5_ROW, 1), lambda i: (off + i, 0, 0)),
        ],
        out_specs=pl.BlockSpec((gs, rows, lanes), lambda i: (i, 0, 0)),
        scratch_shapes=[pltpu.VMEM((gs, lanes, STATE_LANES), F32),
                        pltpu.VMEM((gs, lanes, STATE_LANES), F32)],
        compiler_params=pltpu.CompilerParams(
            dimension_semantics=("parallel",),
            vmem_limit_bytes=VMEM_LIMIT),
        name="s5_core",
    )(zt, toept, ws, wot, atab, segtab, dcol)


def _log_sigmoid(x):
    return -(jnp.maximum(-x, 0.0) + jnp.log1p(jnp.exp(-jnp.abs(x))))


def _chunk_time(idx):
    return ((idx & (SUBS_PER_SEG - 1)) * SUB) | (idx >> 4)


def _mlstm_kernel(x_ref, mod_ref, gain_ref, w3_ref, cw_ref, cb_ref,
                  wq_ref, wk_ref, wv_ref, wgt_ref, bgt_ref, ng_ref, skip_ref, o_ref,
                  ct_ref, m_ref, tail_ref, ext_ref, qkv_ref, proj_ref, xc_ref):
    L, dh, d = SEG_LEN, HEAD_DIM, D_MODEL
    halo = (CONV_WIDTH - 1) * SUBS_PER_SEG

    @pl.when(pl.program_id(1) == 0)
    def _():
        ct_ref[...] = jnp.zeros_like(ct_ref)
        m_ref[...] = jnp.zeros_like(m_ref)
        tail_ref[...] = jnp.zeros_like(tail_ref)

    x = x_ref[...].reshape(L, d)
    h = _modulated_norm(x, gain_ref[...], mod_ref[...]).astype(BF16)
    proj_ref[...] = _dot(h, w3_ref[...])
    m_in = proj_ref[:, 0:d]

    ext_ref[halo:halo + L, :] = m_in
    row16 = lax.broadcasted_iota(jnp.int32, (SUBS_PER_SEG, d), 0)
    for i in range(CONV_WIDTH - 1):
        lsrc = SUB - (CONV_WIDTH - 1) + i
        r0 = halo + lsrc * SUBS_PER_SEG
        shifted = ext_ref[pl.ds(r0 - 1, SUBS_PER_SEG), :]
        ext_ref[i * SUBS_PER_SEG:(i + 1) * SUBS_PER_SEG, :] = jnp.where(
            row16 == 0, tail_ref[i:i + 1, :], shifted)
        tail_ref[i:i + 1, :] = ext_ref[r0 + SUBS_PER_SEG - 1:r0 + SUBS_PER_SEG, :]
    acc = m_in * cw_ref[CONV_WIDTH - 1:CONV_WIDTH, :] + cb_ref[...]
    for j in range(CONV_WIDTH - 1):
        back = CONV_WIDTH - 1 - j
        acc = acc + ext_ref[pl.ds(halo - back * SUBS_PER_SEG, L), :] * cw_ref[j:j + 1, :]
    xc = acc * _sigmoid(acc)
    xc_ref[...] = xc

    xcb = xc.astype(BF16)
    minb = m_in.astype(BF16)
    for hd in range(HEADS):
        sl = slice(hd * dh, (hd + 1) * dh)
        qkv_ref[:, hd * dh:(hd + 1) * dh] = _dot(xcb[:, sl], wq_ref[hd]).astype(BF16)
        qkv_ref[:, d + hd * dh:d + (hd + 1) * dh] = (
            _dot(xcb[:, sl], wk_ref[hd]) * (dh ** -0.5)).astype(BF16)
        qkv_ref[:, 2 * d + hd * dh:2 * d + (hd + 1) * dh] = (
            _dot(minb[:, sl], wv_ref[hd]).astype(BF16))

    gates_t = _dot_nt(wgt_ref[...], qkv_ref[...]) + bgt_ref[...]

    t_row = _chunk_time(lax.broadcasted_iota(jnp.int32, (L, L), 0))
    t_col = _chunk_time(lax.broadcasted_iota(jnp.int32, (L, L), 1))
    causal = t_col <= t_row
    tri_u = jnp.where(t_row <= t_col, 1.0, 0.0).astype(BF16)

    lft_hi, lft_lo = _split_hi_lo(_log_sigmoid(gates_t))
    b_rows = _dot(lft_hi, tri_u) + _dot(lft_lo, tri_u)
    stacked = jnp.concatenate([gates_t, b_rows, jnp.zeros((128 - 16, L), F32)], axis=0)
    cols = stacked.T

    lane = lax.broadcasted_iota(jnp.int32, (L, 128), 1)
    ones_blk = jnp.where(lane == 0, 1.0, 0.0).astype(BF16)

    for hd in range(HEADS):
        sl = slice(hd * dh, (hd + 1) * dh)
        i_r = gates_t[hd:hd + 1, :]
        b_r = b_rows[HEADS + hd:HEADS + hd + 1, :]
        i_c = cols[:, hd:hd + 1]
        b_c = cols[:, 8 + HEADS + hd:8 + HEADS + hd + 1]
        m_prev = m_ref[hd:hd + 1, 0:1]

        log_d = jnp.where(causal, b_c + (i_r - b_r), -jnp.inf)
        m_inter = b_c + m_prev
        m_t = jnp.maximum(m_inter, jnp.max(log_d, axis=-1, keepdims=True))
        dmat = jnp.exp(log_d - m_t)
        qb = qkv_ref[:, sl]
        kb = qkv_ref[:, d + hd * dh:d + (hd + 1) * dh]
        s = _dot_nt(qb, kb)
        p = (s * dmat).astype(BF16)
        v_aug = jnp.concatenate([qkv_ref[:, 2 * d + hd * dh:2 * d + (hd + 1) * dh], ones_blk],
                                axis=-1)
        w_inter = jnp.exp(m_inter - m_t)
        inter = _dot(qb, ct_ref[hd].astype(BF16))
        intra = _dot(p, v_aug)
        nd = w_inter * inter + intra
        num = nd[:, 0:dh]
        den = nd[:, dh:dh + 1]
        hh = num / jnp.maximum(jnp.abs(den), jnp.exp(-m_t))

        b_tot = b_c[L - 1:L, :]
        log_w = b_tot - b_c + i_c
        m_next = jnp.maximum(b_tot + m_prev, jnp.max(log_w, axis=0, keepdims=True))
        decay = jnp.exp(b_tot + m_prev - m_next)
        w_c = jnp.exp(log_w - m_next)
        upd = _dot(kb.T, (w_c * v_aug.astype(F32)).astype(BF16))
        ct_ref[hd] = decay * ct_ref[hd] + upd
        m_ref[hd:hd + 1, :] = jnp.broadcast_to(m_next, (1, 128))

        hg = hh * _sigmoid(proj_ref[:, d + hd * dh:d + (hd + 1) * dh])
        mu = jnp.mean(hg, axis=-1, keepdims=True)
        dev = hg - mu
        var = jnp.mean(dev * dev, axis=-1, keepdims=True)
        hn = dev * lax.rsqrt(var + EPS) * ng_ref[:, sl] + skip_ref[:, sl] * xc_ref[:, sl]
        gate = proj_ref[:, 2 * d + hd * dh:2 * d + (hd + 1) * dh]
        y = hn * (gate * _sigmoid(gate))
        o_ref[:, :, sl] = y.reshape(SUB, SUBS_PER_SEG, dh).astype(o_ref.dtype)


def _mlstm(xp, mod_all, gain_all, w3, cw, cb, wq, wk, wv, wgt, bgt, ng, skip, layer):
    bsz = xp.shape[0]
    d = D_MODEL
    L = SEG_LEN
    lay2 = lambda shape: pl.BlockSpec((None,) + shape, lambda b, c: (layer, 0, 0))
    lay3 = lambda shape: pl.BlockSpec((None,) + shape, lambda b, c: (layer, 0, 0, 0))
    seg_spec = pl.BlockSpec((None, SUB, None, SUBS_PER_SEG, d), lambda b, c: (b, 0, c, 0, 0))
    halo = (CONV_WIDTH - 1) * SUBS_PER_SEG
    return pl.pallas_call(
        _mlstm_kernel,
        out_shape=jax.ShapeDtypeStruct(xp.shape, BF16),
        grid=(bsz, N_SEG),
        in_specs=[
            seg_spec,
            pl.BlockSpec((None, None, 1, 3 * d), lambda b, c: (layer, b, 0, 0)),
            lay2((1, d)),
            lay2((d, 3 * d)),
            lay2((CONV_WIDTH, d)),
            lay2((1, d)),
            lay3((HEADS, HEAD_DIM, HEAD_DIM)),
            lay3((HEADS, HEAD_DIM, HEAD_DIM)),
            lay3((HEADS, HEAD_DIM, HEAD_DIM)),
            lay2((8, 3 * d)),
            lay2((8, 1)),
            lay2((1, d)),
            lay2((1, d)),
        ],
        out_specs=seg_spec,
        scratch_shapes=[
            pltpu.VMEM((HEADS, HEAD_DIM, M_AUG), F32),
            pltpu.VMEM((8, 128), F32),
            pltpu.VMEM((8, d), F32),
            pltpu.VMEM((halo + L, d), F32),
            pltpu.VMEM((L, 3 * d), BF16),
            pltpu.VMEM((L, 3 * d), F32),
            pltpu.VMEM((L, d), F32),
        ],
        compiler_params=pltpu.CompilerParams(
            dimension_semantics=("parallel", "arbitrary"),
            vmem_limit_bytes=VMEM_LIMIT),
        name="mlstm",
    )(xp, mod_all, gain_all, w3, cw, cb, wq, wk, wv, wgt, bgt, ng, skip)


M_CHAINS = 2


def _mlstm2_kernel(x_ref, mod_ref, gain_ref, w3_ref, cw_ref, cb_ref,
                   wq_ref, wk_ref, wv_ref, wg_ref, bg_ref, ng_ref, skip_ref, o_ref,
                   ct_ref, m_ref, tail_ref, ext_ref, qkv_ref, proj_ref, xc_ref, hn_ref, cvb_ref):
    L, dh, d = SEG_LEN, HEAD_DIM, D_MODEL
    halo = (CONV_WIDTH - 1) * SUBS_PER_SEG

    @pl.when(pl.program_id(1) == 0)
    def _():
        ct_ref[...] = jnp.zeros_like(ct_ref)
        m_ref[...] = jnp.zeros_like(m_ref)
        tail_ref[...] = jnp.zeros_like(tail_ref)

    t_row = _chunk_time(lax.broadcasted_iota(jnp.int32, (L, L), 0))
    t_col = _chunk_time(lax.broadcasted_iota(jnp.int32, (L, L), 1))
    causal = t_col <= t_row
    tri_l = jnp.where(causal, 1.0, 0.0).astype(BF16)
    lane = lax.broadcasted_iota(jnp.int32, (L, 128), 1)
    ones_blk = jnp.where(lane == 0, 1.0, 0.0).astype(BF16)
    row16 = lax.broadcasted_iota(jnp.int32, (SUBS_PER_SEG, d), 0)

    def stage_norm_in(c):
        x = x_ref[c].reshape(L, d)
        hn_ref[c] = _modulated_norm(x, gain_ref[...], mod_ref[c]).astype(BF16)
        proj_ref[c, :, 0:d] = _dot(hn_ref[c], w3_ref[:, 0:d])

    def stage_gate_proj(c):
        proj_ref[c, :, d:2 * d] = _dot(hn_ref[c], w3_ref[:, d:2 * d])
        proj_ref[c, :, 2 * d:3 * d] = _dot(hn_ref[c], w3_ref[:, 2 * d:3 * d])

    def stage_conv(c):
        m_in = proj_ref[c, :, 0:d]
        ext_ref[c, halo:halo + L, :] = m_in
        cvb_ref[c, :, d:2 * d] = m_in.astype(BF16)
        for i in range(CONV_WIDTH - 1):
            lsrc = SUB - (CONV_WIDTH - 1) + i
            r0 = halo + lsrc * SUBS_PER_SEG
            shifted = ext_ref[c, pl.ds(r0 - 1, SUBS_PER_SEG), :]
            ext_ref[c, i * SUBS_PER_SEG:(i + 1) * SUBS_PER_SEG, :] = jnp.where(
                row16 == 0, tail_ref[c, i:i + 1, :], shifted)
            tail_ref[c, i:i + 1, :] = ext_ref[c, r0 + SUBS_PER_SEG - 1:r0 + SUBS_PER_SEG, :]
        acc = m_in * cw_ref[CONV_WIDTH - 1:CONV_WIDTH, :] + cb_ref[...]
        for j in range(CONV_WIDTH - 1):
            back = CONV_WIDTH - 1 - j
            acc = acc + ext_ref[c, pl.ds(halo - back * SUBS_PER_SEG, L), :] * cw_ref[j:j + 1, :]
        xc = acc * _sigmoid(acc)
        xc_ref[c] = xc
        cvb_ref[c, :, 0:d] = xc.astype(BF16)

    def stage_qkv(c):
        for hd in range(HEADS):
            sl = slice(hd * dh, (hd + 1) * dh)
            xcb = cvb_ref[c, :, hd * dh:(hd + 1) * dh]
            qkv_ref[c, :, hd * dh:(hd + 1) * dh] = _dot(xcb, wq_ref[hd]).astype(BF16)
            qkv_ref[c, :, d + hd * dh:d + (hd + 1) * dh] = (
                _dot(xcb, wk_ref[hd]) * (dh ** -0.5)).astype(BF16)
            qkv_ref[c, :, 2 * d + hd * dh:2 * d + (hd + 1) * dh] = (
                _dot(cvb_ref[c, :, d + hd * dh:d + (hd + 1) * dh], wv_ref[hd]).astype(BF16))
        gates = _dot(qkv_ref[c], wg_ref[...]) + bg_ref[...]
        log_f = _log_sigmoid(gates)
        lf_hi, lf_lo = _split_hi_lo(log_f)
        b_cols = _dot(tri_l, lf_hi) + _dot(tri_l, lf_lo)
        return gates.T, b_cols.T, gates, b_cols

    def stage_head(c, hd, gate_forms):
        gates_t, b_rows, gates, b_cols = gate_forms
        sl = slice(hd * dh, (hd + 1) * dh)
        i_r = gates_t[hd:hd + 1, :]
        b_r = b_rows[HEADS + hd:HEADS + hd + 1, :]
        i_c = gates[:, hd:hd + 1]
        b_c = b_cols[:, HEADS + hd:HEADS + hd + 1]
        m_prev = m_ref[c, hd:hd + 1, 0:1]

        log_d = jnp.where(causal, b_c + (i_r - b_r), -jnp.inf)
        m_inter = b_c + m_prev
        m_t = jnp.maximum(m_inter, jnp.max(log_d, axis=-1, keepdims=True))
        dmat = jnp.exp(log_d - m_t)
        qb = qkv_ref[c, :, sl]
        kb = qkv_ref[c, :, d + hd * dh:d + (hd + 1) * dh]
        s = _dot_nt(qb, kb)
        p = (s * dmat).astype(BF16)
        v_aug = jnp.concatenate([qkv_ref[c, :, 2 * d + hd * dh:2 * d + (hd + 1) * dh], ones_blk],
                                axis=-1)
        w_inter = jnp.exp(m_inter - m_t)
        inter = _dot(qb, ct_ref[c, hd].astype(BF16))
        intra = _dot(p, v_aug)
        nd = w_inter * inter + intra
        num = nd[:, 0:dh]
        den = nd[:, dh:dh + 1]
        hh = num / jnp.maximum(jnp.abs(den), jnp.exp(-m_t))

        b_tot = b_c[L - 1:L, :]
        log_w = b_tot - b_c + i_c
        m_next = jnp.maximum(b_tot + m_prev, jnp.max(log_w, axis=0, keepdims=True))
        decay = jnp.exp(b_tot + m_prev - m_next)
        w_c = jnp.exp(log_w - m_next)
        upd = _dot(kb.T, (w_c * v_aug.astype(F32)).astype(BF16))
        ct_ref[c, hd] = decay * ct_ref[c, hd] + upd
        m_ref[c, hd:hd + 1, :] = jnp.broadcast_to(m_next, (1, 128))

        hg = hh * _sigmoid(proj_ref[c, :, d + hd * dh:d + (hd + 1) * dh])
        mu = jnp.mean(hg, axis=-1, keepdims=True)
        dev = hg - mu
        var = jnp.mean(dev * dev, axis=-1, keepdims=True)
        hn = dev * lax.rsqrt(var + EPS) * ng_ref[:, sl] + skip_ref[:, sl] * xc_ref[c, :, sl]
        gate = proj_ref[c, :, 2 * d + hd * dh:2 * d + (hd + 1) * dh]
        y = hn * (gate * _sigmoid(gate))
        o_ref[c, :, :, sl] = y.reshape(SUB, SUBS_PER_SEG, dh).astype(o_ref.dtype)

    a, b = 0, 1
    stage_norm_in(a)
    stage_conv(a)
    stage_norm_in(b)
    stage_gate_proj(a)
    gf_a = stage_qkv(a)
    stage_conv(b)
    stage_gate_proj(b)
    stage_head(a, 0, gf_a)
    gf_b = stage_qkv(b)
    stage_head(a, 1, gf_a)
    stage_head(b, 0, gf_b)
    stage_head(a, 2, gf_a)
    stage_head(b, 1, gf_b)
    stage_head(a, 3, gf_a)
    stage_head(b, 2, gf_b)
    stage_head(b, 3, gf_b)


def _mlstm2(xp, mod_all, gain_all, w3, cw, cb, wq, wk, wv, wg, bg, ng, skip, layer):
    bsz = xp.shape[0]
    d = D_MODEL
    L = SEG_LEN
    nc = M_CHAINS
    lay2 = lambda shape: pl.BlockSpec((None,) + shape, lambda b, c: (layer, 0, 0))
    lay3 = lambda shape: pl.BlockSpec((None,) + shape, lambda b, c: (layer, 0, 0, 0))
    seg_spec = pl.BlockSpec((nc, SUB, None, SUBS_PER_SEG, d), lambda b, c: (b, 0, c, 0, 0))
    halo = (CONV_WIDTH - 1) * SUBS_PER_SEG
    return pl.pallas_call(
        _mlstm2_kernel,
        out_shape=jax.ShapeDtypeStruct(xp.shape, BF16),
        grid=(bsz // nc, N_SEG),
        in_specs=[
            seg_spec,
            pl.BlockSpec((None, nc, 1, 3 * d), lambda b, c: (layer, b, 0, 0)),
            lay2((1, d)),
            lay2((d, 3 * d)),
            lay2((CONV_WIDTH, d)),
            lay2((1, d)),
            lay3((HEADS, HEAD_DIM, HEAD_DIM)),
            lay3((HEADS, HEAD_DIM, HEAD_DIM)),
            lay3((HEADS, HEAD_DIM, HEAD_DIM)),
            lay2((3 * d, 128)),
            lay2((1, 128)),
            lay2((1, d)),
            lay2((1, d)),
        ],
        out_specs=seg_spec,
        scratch_shapes=[
            pltpu.VMEM((nc, HEADS, HEAD_DIM, M_AUG), F32),
            pltpu.VMEM((nc, 8, 128), F32),
            pltpu.VMEM((nc, 8, d), F32),
            pltpu.VMEM((nc, halo + L, d), F32),
            pltpu.VMEM((nc, L, 3 * d), BF16),
            pltpu.VMEM((nc, L, 3 * d), F32),
            pltpu.VMEM((nc, L, d), F32),
            pltpu.VMEM((nc, L, d), BF16),
            pltpu.VMEM((nc, L, 2 * d), BF16),
        ],
        compiler_params=pltpu.CompilerParams(
            dimension_semantics=("parallel", "arbitrary"),
            vmem_limit_bytes=VMEM_LIMIT),
        name="mlstm",
    )(xp, mod_all, gain_all, w3, cw, cb, wq, wk, wv, wg, bg, ng, skip)


def _gelu_tanh(x):
    inner = math.sqrt(2.0 / math.pi) * (x + 0.044715 * (x * x * x))
    return x * (0.5 * (1.0 + jnp.tanh(inner)))


def _out_kernel(yt_ref, gt_ref, my_ref, x_ref, mod_ref, bglu_ref, og_ref, fg_ref,
                wglut_ref, wout_ref, qt_ref, o_ref, *, final):
    d = D_MODEL
    rows = L_PAIR * BLK_LANES

    def stage_in(s):
        ls = range(s * L_PAIR, (s + 1) * L_PAIR)
        out_m = _dot(my_ref[s * L_PAIR:(s + 1) * L_PAIR].reshape(rows, d),
                     wout_ref[d:2 * d, :])
        yt = jnp.concatenate(
            [yt_ref[:, l * SSM_GROUP:(l + 1) * SSM_GROUP, :].reshape(d, BLK_LANES) for l in ls],
            axis=-1)
        y = _gelu_tanh(_dot(yt, qt_ref[...]))
        return out_m, y

    def stage_glu(s, y):
        ls = range(s * L_PAIR, (s + 1) * L_PAIR)
        glu = y * _sigmoid(_dot(wglut_ref[...], y.astype(BF16)) + bglu_ref[...])
        gate_s = jnp.concatenate([gt_ref[l] for l in ls], axis=-1).astype(F32)
        ms = jnp.mean(glu * glu, axis=0, keepdims=True)
        return (glu * lax.rsqrt(ms + EPS) * og_ref[...] * gate_s).astype(BF16)

    def stage_out(s, out_m, ssm_y):
        out = _dot_tn(ssm_y, wout_ref[0:d, :]) + out_m
        xn = x_ref[s * L_PAIR:(s + 1) * L_PAIR].reshape(rows, d) + mod_ref[:, 2 * d:3 * d] * out
        if final:
            xn = _rms(xn) * fg_ref[...]
        o_ref[s * L_PAIR:(s + 1) * L_PAIR] = xn.reshape(L_PAIR, N_SEG, SUBS_PER_SEG, d)

    om0, y0 = stage_in(0)
    om1, y1 = stage_in(1)
    s0 = stage_glu(0, y0)
    s1 = stage_glu(1, y1)
    stage_out(0, om0, s0)
    stage_out(1, om1, s1)


def _out_stage(yt, gt, my, xp, mod_all, bglu_col, og_col, fg, wglut, wout, qperm_t, layer, final):
    bsz = xp.shape[0]
    d = D_MODEL
    n_l = OUT_STREAMS * L_PAIR
    tok = pl.BlockSpec((None, n_l, N_SEG, SUBS_PER_SEG, d), lambda b, p: (b, p, 0, 0, 0))
    col = pl.BlockSpec((None, d, 1), lambda b, p: (layer, 0, 0))
    return pl.pallas_call(
        functools.partial(_out_kernel, final=final),
        out_shape=jax.ShapeDtypeStruct(xp.shape, F32),
        grid=(bsz, SUB // n_l),
        in_specs=[
            pl.BlockSpec((SSM_GROUPS, n_l * SSM_GROUP, BLK_LANES), lambda b, p: (0, p, b)),
            pl.BlockSpec((None, n_l, d, BLK_LANES), lambda b, p: (b, p, 0, 0)),
            tok, tok,
            pl.BlockSpec((None, None, 1, 3 * d), lambda b, p: (layer, b, 0, 0)),
            col, col,
            pl.BlockSpec((1, d), lambda b, p: (0, 0)),
            pl.BlockSpec((None, d, d), lambda b, p: (layer, 0, 0)),
            pl.BlockSpec((None, 2 * d, d), lambda b, p: (layer, 0, 0)),
            pl.BlockSpec((L_PAIR * BLK_LANES, L_PAIR * BLK_LANES), lambda b, p: (0, 0)),
        ],
        out_specs=tok,
        compiler_params=pltpu.CompilerParams(
            dimension_semantics=("parallel", "parallel"),
            vmem_limit_bytes=VMEM_LIMIT),
        name="out_stage",
    )(yt, gt, my, xp, mod_all, bglu_col, og_col, fg, wglut, wout, qperm_t)


def _lane_permutation():
    q = np.zeros((BLK_LANES, BLK_LANES), np.float32)
    for seg in range(N_SEG):
        for kl in range(SUBS_PER_SEG):
            q[seg * SUBS_PER_SEG + kl, kl * N_SEG + seg] = 1.0
    return np.kron(np.eye(L_PAIR, dtype=np.float32), q)


def kernel(x, c, norm_gain, w_mod, b_mod, w_in, ssm_lambda_re, ssm_lambda_im, ssm_log_dt,
           ssm_b_re, ssm_b_im, ssm_c_re, ssm_c_im, ssm_d, ssm_w_glu, ssm_b_glu,
           ssm_out_gain, m_conv_w, m_conv_b, m_wq, m_wk, m_wv, m_w_gates, m_b_igate,
           m_b_fgate, m_norm_gain, m_skip, w_out, final_gain):
    bsz, seq, d = x.shape
    depth = w_in.shape[0]
    assert d == D_MODEL and seq == SEQ

    xp = x.reshape(bsz, N_SEG, SUBS_PER_SEG, SUB, d).transpose(0, 3, 1, 2, 4)
    mod_all = _modulation(c, w_mod, b_mod).reshape(depth, bsz, 1, 3 * d)
    rows = lambda v: v.reshape(depth, 1, -1)
    cols = lambda v: v.reshape(depth, -1, 1)
    qperm = jnp.asarray(_lane_permutation(), BF16)
    qperm_t = jnp.asarray(_lane_permutation().T, BF16)

    gain_all = rows(norm_gain)
    w_ssm_t = w_in[:, :, 0:2 * d].transpose(0, 2, 1).astype(BF16)
    w3 = w_in[:, :, 2 * d:5 * d].astype(BF16)
    toept, ws, wot, atab, segtab = _s5_prep(ssm_lambda_re, ssm_lambda_im, ssm_log_dt,
                                            ssm_b_re, ssm_b_im, ssm_c_re, ssm_c_im)
    dcol = jnp.broadcast_to(ssm_d.reshape(depth * SSM_GROUPS, 1, SSM_GROUP),
                            (depth * SSM_GROUPS, SUB, SSM_GROUP)).reshape(-1, S5_ROW, 1)
    wq, wk, wv = m_wq.astype(BF16), m_wk.astype(BF16), m_wv.astype(BF16)
    gate_pad = ((0, 0), (0, 0), (0, 128 - 2 * HEADS))
    wg = jnp.pad(m_w_gates, gate_pad).astype(BF16)
    bg = jnp.pad(jnp.concatenate([m_b_igate, m_b_fgate], axis=-1).reshape(depth, 1, 2 * HEADS),
                 gate_pad)
    wglut = ssm_w_glu.transpose(0, 2, 1).astype(BF16)
    wout = w_out.astype(BF16)
    fg = final_gain.reshape(1, d)

    for l in range(depth):
        zt, gt = _s5_in(xp, mod_all, gain_all, w_ssm_t, qperm, l)
        yt = _s5_core(zt, toept, ws, wot, atab, segtab, dcol, bsz, l)
        my = _mlstm2(xp, mod_all, gain_all, w3, m_conv_w, rows(m_conv_b), wq, wk, wv, wg, bg,
                    rows(m_norm_gain), rows(m_skip), l)
        xp = _out_stage(yt, gt, my, xp, mod_all, cols(ssm_b_glu), cols(ssm_out_gain), fg,
                        wglut, wout, qperm_t, l, final=(l == depth - 1))

    return xp.transpose(0, 2, 3, 1, 4).reshape(bsz, seq, d)
```

```python
import functools
import math

import numpy as np
import jax
import jax.numpy as jnp
from jax import lax
from jax.experimental import pallas as pl
from jax.experimental.pallas import tpu as pltpu

F32 = jnp.float32
BF16 = jnp.bfloat16

D_MODEL = 1024
SSM_GROUP = 16
SSM_GROUPS = D_MODEL // SSM_GROUP
SSM_STATE = 64
HEADS = 4
HEAD_DIM = D_MODEL // HEADS
CONV_WIDTH = 4
EPS = 1e-6

SUB = 16
SUBS_PER_SEG = 16
SEG_LEN = SUB * SUBS_PER_SEG
N_SEG = 8
SEQ = N_SEG * SEG_LEN
BLK_LANES = N_SEG * SUBS_PER_SEG
S5_ROW = SUB * SSM_GROUP
STATE_HALF = 128
STATE_LANES = 2 * STATE_HALF
S5_GROUPS_PER_STEP = 4
L_PAIR = 2
OUT_STREAMS = 2
M_AUG = HEAD_DIM + 128
VMEM_LIMIT = 56 * 1024 * 1024

_NT = (((1,), (1,)), ((), ()))
_TN = (((0,), (0,)), ((), ()))


def _dot(a, b):
    return jnp.dot(a, b, preferred_element_type=F32)


def _dot_nt(a, b):
    return lax.dot_general(a, b, _NT, preferred_element_type=F32)


def _dot_tn(a, b):
    return lax.dot_general(a, b, _TN, preferred_element_type=F32)


def _sigmoid(x):
    return jax.nn.sigmoid(x)


def _rms(x):
    return x * lax.rsqrt(jnp.mean(x * x, axis=-1, keepdims=True) + EPS)


def _modulated_norm(x, gain, mod):
    shift = mod[:, 0:D_MODEL]
    scale = mod[:, D_MODEL:2 * D_MODEL]
    return _rms(x) * gain * (1.0 + scale) + shift


def _split_hi_lo(v):
    hi = v.astype(BF16)
    lo = (v - hi.astype(F32)).astype(BF16)
    return hi, lo


def _dot_nt_f32(a, b):
    a_hi, a_lo = _split_hi_lo(a)
    b_hi, b_lo = _split_hi_lo(b)
    return _dot_nt(a_hi, b_hi) + _dot_nt(a_hi, b_lo) + _dot_nt(a_lo, b_hi)


def _cmul(a_re, a_im, b_re, b_im):
    return a_re * b_re - a_im * b_im, a_re * b_im + a_im * b_re


def _cmul_add(p_re, p_im, s_re, s_im, add_re, add_im):
    return (p_re * s_re - p_im * s_im + add_re,
            p_re * s_im + p_im * s_re + add_im)


def _mod_kernel(c_ref, w_ref, b_ref, o_ref):
    cv = c_ref[...]
    act = cv * _sigmoid(cv)
    o_ref[...] = _dot(act.astype(BF16), w_ref[...].astype(BF16)) + b_ref[...]


def _modulation(c, w_mod, b_mod):
    depth, d, n = w_mod.shape
    bsz = c.shape[0]
    return pl.pallas_call(
        _mod_kernel,
        out_shape=jax.ShapeDtypeStruct((depth, bsz, n), F32),
        grid=(depth, n // d),
        in_specs=[
            pl.BlockSpec((bsz, d), lambda l, j: (0, 0)),
            pl.BlockSpec((None, d, d), lambda l, j: (l, 0, j)),
            pl.BlockSpec((None, 1, d), lambda l, j: (l, 0, j)),
        ],
        out_specs=pl.BlockSpec((None, bsz, d), lambda l, j: (l, 0, j)),
        compiler_params=pltpu.CompilerParams(
            dimension_semantics=("parallel", "parallel"),
            vmem_limit_bytes=VMEM_LIMIT),
        name="adaln_mod",
    )(c, w_mod, b_mod.reshape(depth, 1, n))


def _s5_prep_kernel(lre_ref, lim_ref, ldt_ref, btre_ref, btim_ref, cre_ref, cim_ref,
                    toept_ref, ws_ref, wot_ref, atab_ref, seg_ref):
    gp = S5_GROUPS_PER_STEP
    lane_blk = lax.broadcasted_iota(jnp.int32, (S5_ROW, S5_ROW), 1) // SSM_GROUP
    row8 = lax.broadcasted_iota(jnp.int32, (8, STATE_HALF), 0)
    for j in range(gp):
        lr = lre_ref[j]
        li = lim_ref[j]
        dt = jnp.exp(ldt_ref[j])
        mag = jnp.exp(lr * dt)
        a_re = mag * jnp.cos(li * dt)
        a_im = mag * jnp.sin(li * dt)
        inv = 1.0 / (lr * lr + li * li)
        k_re = ((a_re - 1.0) * lr + a_im * li) * inv
        k_im = (a_im * lr - (a_re - 1.0) * li) * inv
        bb_re, bb_im = _cmul(k_re, k_im, btre_ref[j], btim_ref[j])
        c_re = cre_ref[j]
        c_im = cim_ref[j]

        pw = [(jnp.ones_like(a_re), jnp.zeros_like(a_re))]
        for _ in range(SUB):
            pw.append(_cmul(pw[-1][0], pw[-1][1], a_re, a_im))

        ws_rows, wot_rows, ca_re, ca_im = [], [], [], []
        for l in range(SUB):
            w_re, w_im = _cmul(pw[SUB - 1 - l][0], pw[SUB - 1 - l][1], bb_re, bb_im)
            ws_rows.append(jnp.concatenate([w_re, w_im], axis=-1))
            o_re, o_im = _cmul(c_re, c_im, pw[l + 1][0], pw[l + 1][1])
            wot_rows.append(jnp.concatenate([o_re, -o_im], axis=-1))
            g_re, g_im = _cmul(c_re, c_im, pw[l][0], pw[l][1])
            ca_re.append(g_re)
            ca_im.append(g_im)
        ws_ref[j] = jnp.concatenate(ws_rows, axis=0).astype(BF16)
        wot_ref[j] = jnp.concatenate(wot_rows, axis=0).astype(BF16)

        bbt_re = jnp.concatenate([bb_re] * SUB, axis=0)
        bbt_im = jnp.concatenate([bb_im] * SUB, axis=0)
        kw = (_dot_nt_f32(jnp.concatenate(ca_re, axis=0), bbt_re)
              - _dot_nt_f32(jnp.concatenate(ca_im, axis=0), bbt_im))
        toep = jnp.where(lane_blk == 0, kw, 0.0)
        for lp in range(1, SUB):
            shifted = jnp.concatenate(
                [jnp.zeros((SSM_GROUP * lp, S5_ROW), F32), kw[0:S5_ROW - SSM_GROUP * lp]], axis=0)
            toep = jnp.where(lane_blk == lp, shifted, toep)
        toept_ref[j] = toep.astype(BF16)

        s_re, s_im = pw[SUB]
        atab_ref[j] = jnp.concatenate([jnp.broadcast_to(s_re, (8, STATE_HALF)),
                                       jnp.broadcast_to(s_im, (8, STATE_HALF))], axis=0)
        for _ in range(4):
            s_re, s_im = _cmul(s_re, s_im, s_re, s_im)
        seg = jnp.zeros((8, STATE_HALF), F32)
        for i in range(3):
            seg = jnp.where(row8 == i, s_re, jnp.where(row8 == 3 + i, s_im, seg))
            s_re, s_im = _cmul(s_re, s_im, s_re, s_im)
        seg_ref[j] = seg


def _s5_prep(lam_re, lam_im, log_dt, b_re, b_im, c_re, c_im):
    depth, g, p = lam_re.shape
    cg = SSM_GROUP
    n = depth * g
    gp = S5_GROUPS_PER_STEP
    lane_pad = ((0, 0), (0, 0), (0, STATE_HALF - p))

    def state_rows(v, fill):
        v = v.reshape(n, -1, p)
        return jnp.pad(v, lane_pad, constant_values=fill)

    args = (state_rows(lam_re, -1.0), state_rows(lam_im, 0.0), log_dt.reshape(n, 1, 1),
            state_rows(b_re.transpose(0, 1, 3, 2), 0.0), state_rows(b_im.transpose(0, 1, 3, 2), 0.0),
            state_rows(c_re, 0.0), state_rows(c_im, 0.0))
    vec = pl.BlockSpec((gp, 1, STATE_HALF), lambda i: (i, 0, 0))
    mat = pl.BlockSpec((gp, cg, STATE_HALF), lambda i: (i, 0, 0))
    big = pl.BlockSpec((gp, S5_ROW, STATE_LANES), lambda i: (i, 0, 0))
    return pl.pallas_call(
        _s5_prep_kernel,
        out_shape=(jax.ShapeDtypeStruct((n, S5_ROW, S5_ROW), BF16),
                   jax.ShapeDtypeStruct((n, S5_ROW, STATE_LANES), BF16),
                   jax.ShapeDtypeStruct((n, S5_ROW, STATE_LANES), BF16),
                   jax.ShapeDtypeStruct((n, 16, STATE_HALF), F32),
                   jax.ShapeDtypeStruct((n, 8, STATE_HALF), F32)),
        grid=(n // gp,),
        in_specs=[vec, vec, pl.BlockSpec((gp, 1, 1), lambda i: (i, 0, 0)), mat, mat, mat, mat],
        out_specs=(big, big, big,
                   pl.BlockSpec((gp, 16, STATE_HALF), lambda i: (i, 0, 0)),
                   pl.BlockSpec((gp, 8, STATE_HALF), lambda i: (i, 0, 0))),
        compiler_params=pltpu.CompilerParams(
            dimension_semantics=("parallel",),
            vmem_limit_bytes=VMEM_LIMIT),
        name="s5_prep",
    )(*args)


def _s5_in_kernel(x_ref, mod_ref, gain_ref, wut_ref, wgt_ref, q_ref, zt_ref, gt_ref):
    d = D_MODEL

    def stage_norm(s):
        x = x_ref[s * L_PAIR:(s + 1) * L_PAIR].reshape(L_PAIR * BLK_LANES, d)
        return _modulated_norm(x, gain_ref[...], mod_ref[...]).astype(BF16)

    def stage_proj(s, h):
        ut = _dot_nt(wut_ref[...], h).astype(BF16)
        sgt = _dot_nt(wgt_ref[...], h)
        p = _dot(ut, q_ref[...]).astype(BF16)
        for l in range(L_PAIR):
            lanes = slice(l * BLK_LANES, (l + 1) * BLK_LANES)
            row0 = (s * L_PAIR + l) * SSM_GROUP
            zt_ref[:, row0:row0 + SSM_GROUP, :] = (
                p[:, lanes].reshape(SSM_GROUPS, SSM_GROUP, BLK_LANES))
            g = sgt[:, lanes]
            gt_ref[s * L_PAIR + l] = (g * _sigmoid(g)).astype(BF16)

    hs = [stage_norm(s) for s in range(OUT_STREAMS)]
    for s in range(OUT_STREAMS):
        stage_proj(s, hs[s])


def _s5_in(xp, mod_all, gain_all, w_ssm_t, qperm, layer):
    bsz = xp.shape[0]
    d = D_MODEL
    n_l = OUT_STREAMS * L_PAIR
    return pl.pallas_call(
        _s5_in_kernel,
        out_shape=(jax.ShapeDtypeStruct((SSM_GROUPS, S5_ROW, bsz * BLK_LANES), BF16),
                   jax.ShapeDtypeStruct((bsz, SUB, d, BLK_LANES), BF16)),
        grid=(bsz, SUB // n_l),
        in_specs=[
            pl.BlockSpec((None, n_l, N_SEG, SUBS_PER_SEG, d), lambda b, p: (b, p, 0, 0, 0)),
            pl.BlockSpec((None, None, 1, 3 * d), lambda b, p: (layer, b, 0, 0)),
            pl.BlockSpec((None, 1, d), lambda b, p: (layer, 0, 0)),
            pl.BlockSpec((None, d, d), lambda b, p: (layer, 0, 0)),
            pl.BlockSpec((None, d, d), lambda b, p: (layer, 1, 0)),
            pl.BlockSpec((L_PAIR * BLK_LANES, L_PAIR * BLK_LANES), lambda b, p: (0, 0)),
        ],
        out_specs=(
            pl.BlockSpec((SSM_GROUPS, n_l * SSM_GROUP, BLK_LANES), lambda b, p: (0, p, b)),
            pl.BlockSpec((None, n_l, d, BLK_LANES), lambda b, p: (b, p, 0, 0)),
        ),
        compiler_params=pltpu.CompilerParams(
            dimension_semantics=("parallel", "parallel"),
            vmem_limit_bytes=VMEM_LIMIT),
        name="s5_in",
    )(xp, mod_all, gain_all, w_ssm_t, w_ssm_t, qperm)


def _s5_core_kernel(zt_ref, toept_ref, ws_ref, wot_ref, atab_ref, seg_ref, dcol_ref,
                    yt_ref, loc_ref, sprev_ref, *, bsz):
    gs = S5_GROUPS_PER_STEP
    half = STATE_HALF
    row8 = lax.broadcasted_iota(jnp.int32, (N_SEG, half), 0)

    def shift_down(v, n):
        return jnp.where(row8 >= n, pltpu.roll(v, n, axis=0), 0.0)

    for j in range(gs):
        loc_ref[j] = _dot_tn(zt_ref[j], ws_ref[j])
        a_re = atab_ref[j, 0:8, :]
        a_im = atab_ref[j, 8:16, :]
        for b in range(bsz):
            base = b * BLK_LANES

            def rows(kl, base=base):
                return pl.ds(base + kl * N_SEG, N_SEG)

            e_re = jnp.zeros((N_SEG, half), F32)
            e_im = jnp.zeros((N_SEG, half), F32)
            for kl in range(SUBS_PER_SEG):
                e_re, e_im = _cmul_add(a_re, a_im, e_re, e_im,
                                       loc_ref[j, rows(kl), 0:half],
                                       loc_ref[j, rows(kl), half:STATE_LANES])
            for i, n in enumerate((1, 2, 4)):
                p_re = seg_ref[j, i:i + 1, :]
                p_im = seg_ref[j, 3 + i:4 + i, :]
                e_re, e_im = _cmul_add(p_re, p_im, shift_down(e_re, n), shift_down(e_im, n),
                                       e_re, e_im)
            s_re = shift_down(e_re, 1)
            s_im = shift_down(e_im, 1)
            for kl in range(SUBS_PER_SEG):
                sprev_ref[j, rows(kl), 0:half] = s_re
                sprev_ref[j, rows(kl), half:STATE_LANES] = s_im
                s_re, s_im = _cmul_add(a_re, a_im, s_re, s_im,
                                       loc_ref[j, rows(kl), 0:half],
                                       loc_ref[j, rows(kl), half:STATE_LANES])

        zt = zt_ref[j]
        out = (_dot(toept_ref[j], zt)
               + _dot_nt(wot_ref[j], sprev_ref[j].astype(BF16))
               + dcol_ref[j] * zt.astype(F32))
        yt_ref[j] = out.astype(BF16)


def _s5_core(zt, toept, ws, wot, atab, segtab, dcol, bsz, layer):
    g, rows, lanes = zt.shape
    gs = S5_GROUPS_PER_STEP
    off = layer * (g // gs)
    wspec = pl.BlockSpec((gs, S5_ROW, STATE_LANES), lambda i: (off + i, 0, 0))
    return pl.pallas_call(
        functools.partial(_s5_core_kernel, bsz=bsz),
        out_shape=jax.ShapeDtypeStruct((g, rows, lanes), BF16),
        grid=(g // gs,),
        in_specs=[
            pl.BlockSpec((gs, rows, lanes), lambda i: (i, 0, 0)),
            wspec, wspec, wspec,
            pl.BlockSpec((gs, 16, STATE_HALF), lambda i: (off + i, 0, 0)),
            pl.BlockSpec((gs, 8, STATE_HALF), lambda i: (off + i, 0, 0)),
            pl.BlockSpec((gs, S5_ROW, 1), lambda i: (off + i, 0, 0)),
        ],
        out_specs=pl.BlockSpec((gs, rows, lanes), lambda i: (i, 0, 0)),
        scratch_shapes=[pltpu.VMEM((gs, lanes, STATE_LANES), F32),
                        pltpu.VMEM((gs, lanes, STATE_LANES), F32)],
        compiler_params=pltpu.CompilerParams(
            dimension_semantics=("parallel",),
            vmem_limit_bytes=VMEM_LIMIT),
        name="s5_core",
    )(zt, toept, ws, wot, atab, segtab, dcol)


def _log_sigmoid(x):
    return -(jnp.maximum(-x, 0.0) + jnp.log1p(jnp.exp(-jnp.abs(x))))


def _chunk_time(idx):
    return ((idx & (SUBS_PER_SEG - 1)) * SUB) | (idx >> 4)


def _mlstm_kernel(x_ref, mod_ref, gain_ref, w3_ref, cw_ref, cb_ref,
                  wq_ref, wk_ref, wv_ref, wgt_ref, bgt_ref, ng_ref, skip_ref, o_ref,
                  ct_ref, m_ref, tail_ref, ext_ref, qkv_ref, proj_ref, xc_ref):
    L, dh, d = SEG_LEN, HEAD_DIM, D_MODEL
    halo = (CONV_WIDTH - 1) * SUBS_PER_SEG

    @pl.when(pl.program_id(1) == 0)
    def _():
        ct_ref[...] = jnp.zeros_like(ct_ref)
        m_ref[...] = jnp.zeros_like(m_ref)
        tail_ref[...] = jnp.zeros_like(tail_ref)

    x = x_ref[...].reshape(L, d)
    h = _modulated_norm(x, gain_ref[...], mod_ref[...]).astype(BF16)
    proj_ref[...] = _dot(h, w3_ref[...])
    m_in = proj_ref[:, 0:d]

    ext_ref[halo:halo + L, :] = m_in
    row16 = lax.broadcasted_iota(jnp.int32, (SUBS_PER_SEG, d), 0)
    for i in range(CONV_WIDTH - 1):
        lsrc = SUB - (CONV_WIDTH - 1) + i
        r0 = halo + lsrc * SUBS_PER_SEG
        shifted = ext_ref[pl.ds(r0 - 1, SUBS_PER_SEG), :]
        ext_ref[i * SUBS_PER_SEG:(i + 1) * SUBS_PER_SEG, :] = jnp.where(
            row16 == 0, tail_ref[i:i + 1, :], shifted)
        tail_ref[i:i + 1, :] = ext_ref[r0 + SUBS_PER_SEG - 1:r0 + SUBS_PER_SEG, :]
    acc = m_in * cw_ref[CONV_WIDTH - 1:CONV_WIDTH, :] + cb_ref[...]
    for j in range(CONV_WIDTH - 1):
        back = CONV_WIDTH - 1 - j
        acc = acc + ext_ref[pl.ds(halo - back * SUBS_PER_SEG, L), :] * cw_ref[j:j + 1, :]
    xc = acc * _sigmoid(acc)
    xc_ref[...] = xc

    xcb = xc.astype(BF16)
    minb = m_in.astype(BF16)
    for hd in range(HEADS):
        sl = slice(hd * dh, (hd + 1) * dh)
        qkv_ref[:, hd * dh:(hd + 1) * dh] = _dot(xcb[:, sl], wq_ref[hd]).astype(BF16)
        qkv_ref[:, d + hd * dh:d + (hd + 1) * dh] = (
            _dot(xcb[:, sl], wk_ref[hd]) * (dh ** -0.5)).astype(BF16)
        qkv_ref[:, 2 * d + hd * dh:2 * d + (hd + 1) * dh] = (
            _dot(minb[:, sl], wv_ref[hd]).astype(BF16))

    gates_t = _dot_nt(wgt_ref[...], qkv_ref[...]) + bgt_ref[...]

    t_row = _chunk_time(lax.broadcasted_iota(jnp.int32, (L, L), 0))
    t_col = _chunk_time(lax.broadcasted_iota(jnp.int32, (L, L), 1))
    causal = t_col <= t_row
    tri_u = jnp.where(t_row <= t_col, 1.0, 0.0).astype(BF16)

    lft_hi, lft_lo = _split_hi_lo(_log_sigmoid(gates_t))
    b_rows = _dot(lft_hi, tri_u) + _dot(lft_lo, tri_u)
    stacked = jnp.concatenate([gates_t, b_rows, jnp.zeros((128 - 16, L), F32)], axis=0)
    cols = stacked.T

    lane = lax.broadcasted_iota(jnp.int32, (L, 128), 1)
    ones_blk = jnp.where(lane == 0, 1.0, 0.0).astype(BF16)

    for hd in range(HEADS):
        sl = slice(hd * dh, (hd + 1) * dh)
        i_r = gates_t[hd:hd + 1, :]
        b_r = b_rows[HEADS + hd:HEADS + hd + 1, :]
        i_c = cols[:, hd:hd + 1]
        b_c = cols[:, 8 + HEADS + hd:8 + HEADS + hd + 1]
        m_prev = m_ref[hd:hd + 1, 0:1]

        log_d = jnp.where(causal, b_c + (i_r - b_r), -jnp.inf)
        m_inter = b_c + m_prev
        m_t = jnp.maximum(m_inter, jnp.max(log_d, axis=-1, keepdims=True))
        dmat = jnp.exp(log_d - m_t)
        qb = qkv_ref[:, sl]
        kb = qkv_ref[:, d + hd * dh:d + (hd + 1) * dh]
        s = _dot_nt(qb, kb)
        p = (s * dmat).astype(BF16)
        v_aug = jnp.concatenate([qkv_ref[:, 2 * d + hd * dh:2 * d + (hd + 1) * dh], ones_blk],
                                axis=-1)
        w_inter = jnp.exp(m_inter - m_t)
        inter = _dot(qb, ct_ref[hd].astype(BF16))
        intra = _dot(p, v_aug)
        nd = w_inter * inter + intra
        num = nd[:, 0:dh]
        den = nd[:, dh:dh + 1]
        hh = num / jnp.maximum(jnp.abs(den), jnp.exp(-m_t))

        b_tot = b_c[L - 1:L, :]
        log_w = b_tot - b_c + i_c
        m_next = jnp.maximum(b_tot + m_prev, jnp.max(log_w, axis=0, keepdims=True))
        decay = jnp.exp(b_tot + m_prev - m_next)
        w_c = jnp.exp(log_w - m_next)
        upd = _dot(kb.T, (w_c * v_aug.astype(F32)).astype(BF16))
        ct_ref[hd] = decay * ct_ref[hd] + upd
        m_ref[hd:hd + 1, :] = jnp.broadcast_to(m_next, (1, 128))

        hg = hh * _sigmoid(proj_ref[:, d + hd * dh:d + (hd + 1) * dh])
        mu = jnp.mean(hg, axis=-1, keepdims=True)
        dev = hg - mu
        var = jnp.mean(dev * dev, axis=-1, keepdims=True)
        hn = dev * lax.rsqrt(var + EPS) * ng_ref[:, sl] + skip_ref[:, sl] * xc_ref[:, sl]
        gate = proj_ref[:, 2 * d + hd * dh:2 * d + (hd + 1) * dh]
        y = hn * (gate * _sigmoid(gate))
        o_ref[:, :, sl] = y.reshape(SUB, SUBS_PER_SEG, dh).astype(o_ref.dtype)


def _mlstm(xp, mod_all, gain_all, w3, cw, cb, wq, wk, wv, wgt, bgt, ng, skip, layer):
    bsz = xp.shape[0]
    d = D_MODEL
    L = SEG_LEN
    lay2 = lambda shape: pl.BlockSpec((None,) + shape, lambda b, c: (layer, 0, 0))
    lay3 = lambda shape: pl.BlockSpec((None,) + shape, lambda b, c: (layer, 0, 0, 0))
    seg_spec = pl.BlockSpec((None, SUB, None, SUBS_PER_SEG, d), lambda b, c: (b, 0, c, 0, 0))
    halo = (CONV_WIDTH - 1) * SUBS_PER_SEG
    return pl.pallas_call(
        _mlstm_kernel,
        out_shape=jax.ShapeDtypeStruct(xp.shape, BF16),
        grid=(bsz, N_SEG),
        in_specs=[
            seg_spec,
            pl.BlockSpec((None, None, 1, 3 * d), lambda b, c: (layer, b, 0, 0)),
            lay2((1, d)),
            lay2((d, 3 * d)),
            lay2((CONV_WIDTH, d)),
            lay2((1, d)),
            lay3((HEADS, HEAD_DIM, HEAD_DIM)),
            lay3((HEADS, HEAD_DIM, HEAD_DIM)),
            lay3((HEADS, HEAD_DIM, HEAD_DIM)),
            lay2((8, 3 * d)),
            lay2((8, 1)),
            lay2((1, d)),
            lay2((1, d)),
        ],
        out_specs=seg_spec,
        scratch_shapes=[
            pltpu.VMEM((HEADS, HEAD_DIM, M_AUG), F32),
            pltpu.VMEM((8, 128), F32),
            pltpu.VMEM((8, d), F32),
            pltpu.VMEM((halo + L, d), F32),
            pltpu.VMEM((L, 3 * d), BF16),
            pltpu.VMEM((L, 3 * d), F32),
            pltpu.VMEM((L, d), F32),
        ],
        compiler_params=pltpu.CompilerParams(
            dimension_semantics=("parallel", "arbitrary"),
            vmem_limit_bytes=VMEM_LIMIT),
        name="mlstm",
    )(xp, mod_all, gain_all, w3, cw, cb, wq, wk, wv, wgt, bgt, ng, skip)


M_CHAINS = 2


def _mlstm2_kernel(x_ref, mod_ref, gain_ref, w_min_ref, w_mo_ref, w_mg_ref, cw_ref, cb_ref,
                   wq_ref, wk_ref, wv_ref, wg_ref, bg_ref, ng_ref, skip_ref, o_ref,
                   ct_ref, m_ref, tail_ref, ext_ref, qkv_ref, proj_ref, xc_ref, hn_ref, cvb_ref):
    L, dh, d = SEG_LEN, HEAD_DIM, D_MODEL
    halo = (CONV_WIDTH - 1) * SUBS_PER_SEG

    @pl.when(pl.program_id(1) == 0)
    def _():
        ct_ref[...] = jnp.zeros_like(ct_ref)
        m_ref[...] = jnp.zeros_like(m_ref)
        tail_ref[...] = jnp.zeros_like(tail_ref)

    t_row = _chunk_time(lax.broadcasted_iota(jnp.int32, (L, L), 0))
    t_col = _chunk_time(lax.broadcasted_iota(jnp.int32, (L, L), 1))
    causal = t_col <= t_row
    tri_l = jnp.where(causal, 1.0, 0.0).astype(BF16)
    lane = lax.broadcasted_iota(jnp.int32, (L, 128), 1)
    ones_blk = jnp.where(lane == 0, 1.0, 0.0).astype(BF16)
    row16 = lax.broadcasted_iota(jnp.int32, (SUBS_PER_SEG, d), 0)

    def stage_norm_in(c):
        x = x_ref[c].reshape(L, d)
        hn_ref[c] = _modulated_norm(x, gain_ref[...], mod_ref[c]).astype(BF16)
        proj_ref[c, :, 0:d] = _dot(hn_ref[c], w_min_ref[...])

    def stage_gate_proj(c):
        proj_ref[c, :, d:2 * d] = _dot(hn_ref[c], w_mo_ref[...])
        proj_ref[c, :, 2 * d:3 * d] = _dot(hn_ref[c], w_mg_ref[...])

    def stage_conv(c):
        m_in = proj_ref[c, :, 0:d]
        ext_ref[c, halo:halo + L, :] = m_in
        cvb_ref[c, :, d:2 * d] = m_in.astype(BF16)
        for i in range(CONV_WIDTH - 1):
            lsrc = SUB - (CONV_WIDTH - 1) + i
            r0 = halo + lsrc * SUBS_PER_SEG
            shifted = ext_ref[c, pl.ds(r0 - 1, SUBS_PER_SEG), :]
            ext_ref[c, i * SUBS_PER_SEG:(i + 1) * SUBS_PER_SEG, :] = jnp.where(
                row16 == 0, tail_ref[c, i:i + 1, :], shifted)
            tail_ref[c, i:i + 1, :] = ext_ref[c, r0 + SUBS_PER_SEG - 1:r0 + SUBS_PER_SEG, :]
        acc = m_in * cw_ref[CONV_WIDTH - 1:CONV_WIDTH, :] + cb_ref[...]
        for j in range(CONV_WIDTH - 1):
            back = CONV_WIDTH - 1 - j
            acc = acc + ext_ref[c, pl.ds(halo - back * SUBS_PER_SEG, L), :] * cw_ref[j:j + 1, :]
        xc = acc * _sigmoid(acc)
        xc_ref[c] = xc
        cvb_ref[c, :, 0:d] = xc.astype(BF16)

    def stage_qkv(c):
        for hd in range(HEADS):
            sl = slice(hd * dh, (hd + 1) * dh)
            xcb = cvb_ref[c, :, hd * dh:(hd + 1) * dh]
            qkv_ref[c, :, hd * dh:(hd + 1) * dh] = _dot(xcb, wq_ref[hd]).astype(BF16)
            qkv_ref[c, :, d + hd * dh:d + (hd + 1) * dh] = (
                _dot(xcb, wk_ref[hd]) * (dh ** -0.5)).astype(BF16)
            qkv_ref[c, :, 2 * d + hd * dh:2 * d + (hd + 1) * dh] = (
                _dot(cvb_ref[c, :, d + hd * dh:d + (hd + 1) * dh], wv_ref[hd]).astype(BF16))
        gates = _dot(qkv_ref[c], wg_ref[...]) + bg_ref[...]
        log_f = _log_sigmoid(gates)
        lf_hi, lf_lo = _split_hi_lo(log_f)
        b_cols = _dot(tri_l, lf_hi) + _dot(tri_l, lf_lo)
        return gates.T, b_cols.T, gates, b_cols

    def stage_head(c, hd, gate_forms):
        gates_t, b_rows, gates, b_cols = gate_forms
        sl = slice(hd * dh, (hd + 1) * dh)
        i_r = gates_t[hd:hd + 1, :]
        b_r = b_rows[HEADS + hd:HEADS + hd + 1, :]
        i_c = gates[:, hd:hd + 1]
        b_c = b_cols[:, HEADS + hd:HEADS + hd + 1]
        m_prev = m_ref[c, hd:hd + 1, 0:1]

        log_d = jnp.where(causal, b_c + (i_r - b_r), -jnp.inf)
        m_inter = b_c + m_prev
        m_t = jnp.maximum(m_inter, jnp.max(log_d, axis=-1, keepdims=True))
        dmat = jnp.exp(log_d - m_t)
        qb = qkv_ref[c, :, sl]
        kb = qkv_ref[c, :, d + hd * dh:d + (hd + 1) * dh]
        s = _dot_nt(qb, kb)
        p = (s * dmat).astype(BF16)
        v_aug = jnp.concatenate([qkv_ref[c, :, 2 * d + hd * dh:2 * d + (hd + 1) * dh], ones_blk],
                                axis=-1)
        w_inter = jnp.exp(m_inter - m_t)
        inter = _dot(qb, ct_ref[c, hd].astype(BF16))
        intra = _dot(p, v_aug)
        nd = w_inter * inter + intra
        num = nd[:, 0:dh]
        den = nd[:, dh:dh + 1]
        hh = num / jnp.maximum(jnp.abs(den), jnp.exp(-m_t))

        b_tot = b_c[L - 1:L, :]
        log_w = b_tot - b_c + i_c
        m_next = jnp.maximum(b_tot + m_prev, jnp.max(log_w, axis=0, keepdims=True))
        decay = jnp.exp(b_tot + m_prev - m_next)
        w_c = jnp.exp(log_w - m_next)
        upd = _dot(kb.T, (w_c * v_aug.astype(F32)).astype(BF16))
        ct_ref[c, hd] = decay * ct_ref[c, hd] + upd
        m_ref[c, hd:hd + 1, :] = jnp.broadcast_to(m_next, (1, 128))

        hg = hh * _sigmoid(proj_ref[c, :, d + hd * dh:d + (hd + 1) * dh])
        mu = jnp.mean(hg, axis=-1, keepdims=True)
        dev = hg - mu
        var = jnp.mean(dev * dev, axis=-1, keepdims=True)
        hn = dev * lax.rsqrt(var + EPS) * ng_ref[:, sl] + skip_ref[:, sl] * xc_ref[c, :, sl]
        gate = proj_ref[c, :, 2 * d + hd * dh:2 * d + (hd + 1) * dh]
        y = hn * (gate * _sigmoid(gate))
        o_ref[c, :, :, sl] = y.reshape(SUB, SUBS_PER_SEG, dh).astype(o_ref.dtype)

    a, b = 0, 1
    stage_norm_in(a)
    stage_conv(a)
    stage_norm_in(b)
    stage_gate_proj(a)
    gf_a = stage_qkv(a)
    stage_conv(b)
    stage_gate_proj(b)
    stage_head(a, 0, gf_a)
    gf_b = stage_qkv(b)
    stage_head(a, 1, gf_a)
    stage_head(b, 0, gf_b)
    stage_head(a, 2, gf_a)
    stage_head(b, 1, gf_b)
    stage_head(a, 3, gf_a)
    stage_head(b, 2, gf_b)
    stage_head(b, 3, gf_b)


def _mlstm2(xp, mod_all, gain_all, w_in_b, cw, cb, wq, wk, wv, wg, bg, ng, skip, layer):
    bsz = xp.shape[0]
    d = D_MODEL
    L = SEG_LEN
    nc = M_CHAINS
    lay2 = lambda shape: pl.BlockSpec((None,) + shape, lambda b, c: (layer, 0, 0))
    lay3 = lambda shape: pl.BlockSpec((None,) + shape, lambda b, c: (layer, 0, 0, 0))
    seg_spec = pl.BlockSpec((nc, SUB, None, SUBS_PER_SEG, d), lambda b, c: (b, 0, c, 0, 0))
    w_cols = lambda j: pl.BlockSpec((None, d, d), lambda b, c: (layer, 0, j))
    halo = (CONV_WIDTH - 1) * SUBS_PER_SEG
    return pl.pallas_call(
        _mlstm2_kernel,
        out_shape=jax.ShapeDtypeStruct(xp.shape, BF16),
        grid=(bsz // nc, N_SEG),
        in_specs=[
            seg_spec,
            pl.BlockSpec((None, nc, 1, 3 * d), lambda b, c: (layer, b, 0, 0)),
            lay2((1, d)),
            w_cols(2), w_cols(3), w_cols(4),
            lay2((CONV_WIDTH, d)),
            lay2((1, d)),
            lay3((HEADS, HEAD_DIM, HEAD_DIM)),
            lay3((HEADS, HEAD_DIM, HEAD_DIM)),
            lay3((HEADS, HEAD_DIM, HEAD_DIM)),
            lay2((3 * d, 128)),
            lay2((1, 128)),
            lay2((1, d)),
            lay2((1, d)),
        ],
        out_specs=seg_spec,
        scratch_shapes=[
            pltpu.VMEM((nc, HEADS, HEAD_DIM, M_AUG), F32),
            pltpu.VMEM((nc, 8, 128), F32),
            pltpu.VMEM((nc, 8, d), F32),
            pltpu.VMEM((nc, halo + L, d), F32),
            pltpu.VMEM((nc, L, 3 * d), BF16),
            pltpu.VMEM((nc, L, 3 * d), F32),
            pltpu.VMEM((nc, L, d), F32),
            pltpu.VMEM((nc, L, d), BF16),
            pltpu.VMEM((nc, L, 2 * d), BF16),
        ],
        compiler_params=pltpu.CompilerParams(
            dimension_semantics=("parallel", "arbitrary"),
            vmem_limit_bytes=VMEM_LIMIT),
        name="mlstm",
    )(xp, mod_all, gain_all, w_in_b, w_in_b, w_in_b, cw, cb, wq, wk, wv, wg, bg, ng, skip)


def _gelu_tanh(x):
    inner = math.sqrt(2.0 / math.pi) * (x + 0.044715 * (x * x * x))
    return x * (0.5 * (1.0 + jnp.tanh(inner)))


def _out_kernel(yt_ref, gt_ref, my_ref, x_ref, mod_ref, bglu_ref, og_ref, fg_ref,
                wglut_ref, wout_ref, qt_ref, o_ref, *, final):
    d = D_MODEL
    rows = L_PAIR * BLK_LANES

    def stage_in(s):
        ls = range(s * L_PAIR, (s + 1) * L_PAIR)
        out_m = _dot(my_ref[s * L_PAIR:(s + 1) * L_PAIR].reshape(rows, d),
                     wout_ref[d:2 * d, :])
        yt = jnp.concatenate(
            [yt_ref[:, l * SSM_GROUP:(l + 1) * SSM_GROUP, :].reshape(d, BLK_LANES) for l in ls],
            axis=-1)
        y = _gelu_tanh(_dot(yt, qt_ref[...]))
        return out_m, y

    def stage_glu(s, y):
        ls = range(s * L_PAIR, (s + 1) * L_PAIR)
        glu = y * _sigmoid(_dot(wglut_ref[...], y.astype(BF16)) + bglu_ref[...])
        gate_s = jnp.concatenate([gt_ref[l] for l in ls], axis=-1).astype(F32)
        ms = jnp.mean(glu * glu, axis=0, keepdims=True)
        return (glu * lax.rsqrt(ms + EPS) * og_ref[...] * gate_s).astype(BF16)

    def stage_out(s, out_m, ssm_y):
        out = _dot_tn(ssm_y, wout_ref[0:d, :]) + out_m
        xn = x_ref[s * L_PAIR:(s + 1) * L_PAIR].reshape(rows, d) + mod_ref[:, 2 * d:3 * d] * out
        if final:
            xn = _rms(xn) * fg_ref[...]
        o_ref[s * L_PAIR:(s + 1) * L_PAIR] = xn.reshape(L_PAIR, N_SEG, SUBS_PER_SEG, d)

    om0, y0 = stage_in(0)
    om1, y1 = stage_in(1)
    s0 = stage_glu(0, y0)
    s1 = stage_glu(1, y1)
    stage_out(0, om0, s0)
    stage_out(1, om1, s1)


def _out_stage(yt, gt, my, xp, mod_all, bglu_col, og_col, fg, wglut, wout, qperm_t, layer, final):
    bsz = xp.shape[0]
    d = D_MODEL
    n_l = OUT_STREAMS * L_PAIR
    tok = pl.BlockSpec((None, n_l, N_SEG, SUBS_PER_SEG, d), lambda b, p: (b, p, 0, 0, 0))
    col = pl.BlockSpec((None, d, 1), lambda b, p: (layer, 0, 0))
    return pl.pallas_call(
        functools.partial(_out_kernel, final=final),
        out_shape=jax.ShapeDtypeStruct(xp.shape, F32),
        grid=(bsz, SUB // n_l),
        in_specs=[
            pl.BlockSpec((SSM_GROUPS, n_l * SSM_GROUP, BLK_LANES), lambda b, p: (0, p, b)),
            pl.BlockSpec((None, n_l, d, BLK_LANES), lambda b, p: (b, p, 0, 0)),
            tok, tok,
            pl.BlockSpec((None, None, 1, 3 * d), lambda b, p: (layer, b, 0, 0)),
            col, col,
            pl.BlockSpec((1, d), lambda b, p: (0, 0)),
            pl.BlockSpec((None, d, d), lambda b, p: (layer, 0, 0)),
            pl.BlockSpec((None, 2 * d, d), lambda b, p: (layer, 0, 0)),
            pl.BlockSpec((L_PAIR * BLK_LANES, L_PAIR * BLK_LANES), lambda b, p: (0, 0)),
        ],
        out_specs=tok,
        compiler_params=pltpu.CompilerParams(
            dimension_semantics=("parallel", "parallel"),
            vmem_limit_bytes=VMEM_LIMIT),
        name="out_stage",
    )(yt, gt, my, xp, mod_all, bglu_col, og_col, fg, wglut, wout, qperm_t)


def _lane_permutation():
    q = np.zeros((BLK_LANES, BLK_LANES), np.float32)
    for seg in range(N_SEG):
        for kl in range(SUBS_PER_SEG):
            q[seg * SUBS_PER_SEG + kl, kl * N_SEG + seg] = 1.0
    return np.kron(np.eye(L_PAIR, dtype=np.float32), q)


def kernel(x, c, norm_gain, w_mod, b_mod, w_in, ssm_lambda_re, ssm_lambda_im, ssm_log_dt,
           ssm_b_re, ssm_b_im, ssm_c_re, ssm_c_im, ssm_d, ssm_w_glu, ssm_b_glu,
           ssm_out_gain, m_conv_w, m_conv_b, m_wq, m_wk, m_wv, m_w_gates, m_b_igate,
           m_b_fgate, m_norm_gain, m_skip, w_out, final_gain):
    bsz, seq, d = x.shape
    depth = w_in.shape[0]
    assert d == D_MODEL and seq == SEQ

    xp = x.reshape(bsz, N_SEG, SUBS_PER_SEG, SUB, d).transpose(0, 3, 1, 2, 4)
    mod_all = _modulation(c, w_mod, b_mod).reshape(depth, bsz, 1, 3 * d)
    rows = lambda v: v.reshape(depth, 1, -1)
    cols = lambda v: v.reshape(depth, -1, 1)
    qperm = jnp.asarray(_lane_permutation(), BF16)
    qperm_t = jnp.asarray(_lane_permutation().T, BF16)

    gain_all = rows(norm_gain)
    w_in_b = w_in.astype(BF16)
    w_ssm_t = w_in_b[:, :, 0:2 * d].transpose(0, 2, 1)
    toept, ws, wot, atab, segtab = _s5_prep(ssm_lambda_re, ssm_lambda_im, ssm_log_dt,
                                            ssm_b_re, ssm_b_im, ssm_c_re, ssm_c_im)
    dcol = jnp.broadcast_to(ssm_d.reshape(depth * SSM_GROUPS, 1, SSM_GROUP),
                            (depth * SSM_GROUPS, SUB, SSM_GROUP)).reshape(-1, S5_ROW, 1)
    wq, wk, wv = m_wq.astype(BF16), m_wk.astype(BF16), m_wv.astype(BF16)
    gate_pad = ((0, 0), (0, 0), (0, 128 - 2 * HEADS))
    wg = jnp.pad(m_w_gates, gate_pad).astype(BF16)
    bg = jnp.pad(jnp.concatenate([m_b_igate, m_b_fgate], axis=-1).reshape(depth, 1, 2 * HEADS),
                 gate_pad)
    wglut = ssm_w_glu.astype(BF16).transpose(0, 2, 1)
    wout = w_out.astype(BF16)
    fg = final_gain.reshape(1, d)

    for l in range(depth):
        zt, gt = _s5_in(xp, mod_all, gain_all, w_ssm_t, qperm, l)
        yt = _s5_core(zt, toept, ws, wot, atab, segtab, dcol, bsz, l)
        my = _mlstm2(xp, mod_all, gain_all, w_in_b, m_conv_w, rows(m_conv_b), wq, wk, wv, wg, bg,
                    rows(m_norm_gain), rows(m_skip), l)
        xp = _out_stage(yt, gt, my, xp, mod_all, cols(ssm_b_glu), cols(ssm_out_gain), fg,
                        wglut, wout, qperm_t, l, final=(l == depth - 1))

    return xp.transpose(0, 2, 3, 1, 4).reshape(bsz, seq, d)
```

```python
import functools
import math

import numpy as np
import jax
import jax.numpy as jnp
from jax import lax
from jax.experimental import pallas as pl
from jax.experimental.pallas import tpu as pltpu

F32 = jnp.float32
BF16 = jnp.bfloat16

D_MODEL = 1024
SSM_GROUP = 16
SSM_GROUPS = D_MODEL // SSM_GROUP
SSM_STATE = 64
HEADS = 4
HEAD_DIM = D_MODEL // HEADS
CONV_WIDTH = 4
EPS = 1e-6

SUB = 16
SUBS_PER_SEG = 16
SEG_LEN = SUB * SUBS_PER_SEG
N_SEG = 8
SEQ = N_SEG * SEG_LEN
BLK_LANES = N_SEG * SUBS_PER_SEG
S5_ROW = SUB * SSM_GROUP
STATE_HALF = 128
STATE_LANES = 2 * STATE_HALF
S5_GROUPS_PER_STEP = 4
L_PAIR = 2
OUT_STREAMS = 4
M_AUG = HEAD_DIM + 128
VMEM_LIMIT = 56 * 1024 * 1024

_NT = (((1,), (1,)), ((), ()))
_TN = (((0,), (0,)), ((), ()))


def _dot(a, b):
    return jnp.dot(a, b, preferred_element_type=F32)


def _dot_nt(a, b):
    return lax.dot_general(a, b, _NT, preferred_element_type=F32)


def _dot_tn(a, b):
    return lax.dot_general(a, b, _TN, preferred_element_type=F32)


def _sigmoid(x):
    return jax.nn.sigmoid(x)


def _rms(x):
    return x * lax.rsqrt(jnp.mean(x * x, axis=-1, keepdims=True) + EPS)


def _modulated_norm(x, gain, mod):
    shift = mod[:, 0:D_MODEL]
    scale = mod[:, D_MODEL:2 * D_MODEL]
    return _rms(x) * gain * (1.0 + scale) + shift


def _split_hi_lo(v):
    hi = v.astype(BF16)
    lo = (v - hi.astype(F32)).astype(BF16)
    return hi, lo


def _dot_nt_f32(a, b):
    a_hi, a_lo = _split_hi_lo(a)
    b_hi, b_lo = _split_hi_lo(b)
    return _dot_nt(a_hi, b_hi) + _dot_nt(a_hi, b_lo) + _dot_nt(a_lo, b_hi)


def _cmul(a_re, a_im, b_re, b_im):
    return a_re * b_re - a_im * b_im, a_re * b_im + a_im * b_re


def _cmul_add(p_re, p_im, s_re, s_im, add_re, add_im):
    return (p_re * s_re - p_im * s_im + add_re,
            p_re * s_im + p_im * s_re + add_im)


def _mod_kernel(c_ref, w_ref, b_ref, o_ref):
    cv = c_ref[...]
    act = cv * _sigmoid(cv)
    o_ref[...] = _dot(act.astype(BF16), w_ref[...].astype(BF16)) + b_ref[...]


def _modulation(c, w_mod, b_mod):
    depth, d, n = w_mod.shape
    bsz = c.shape[0]
    return pl.pallas_call(
        _mod_kernel,
        out_shape=jax.ShapeDtypeStruct((depth, bsz, n), F32),
        grid=(depth, n // d),
        in_specs=[
            pl.BlockSpec((bsz, d), lambda l, j: (0, 0)),
            pl.BlockSpec((None, d, d), lambda l, j: (l, 0, j)),
            pl.BlockSpec((None, 1, d), lambda l, j: (l, 0, j)),
        ],
        out_specs=pl.BlockSpec((None, bsz, d), lambda l, j: (l, 0, j)),
        compiler_params=pltpu.CompilerParams(
            dimension_semantics=("parallel", "parallel"),
            vmem_limit_bytes=VMEM_LIMIT),
        name="adaln_mod",
    )(c, w_mod, b_mod.reshape(depth, 1, n))


def _s5_prep_kernel(lre_ref, lim_ref, ldt_ref, btre_ref, btim_ref, cre_ref, cim_ref,
                    toept_ref, ws_ref, wot_ref, atab_ref, seg_ref):
    gp = S5_GROUPS_PER_STEP
    lane_blk = lax.broadcasted_iota(jnp.int32, (S5_ROW, S5_ROW), 1) // SSM_GROUP
    row8 = lax.broadcasted_iota(jnp.int32, (8, STATE_HALF), 0)
    for j in range(gp):
        lr = lre_ref[j]
        li = lim_ref[j]
        dt = jnp.exp(ldt_ref[j])
        mag = jnp.exp(lr * dt)
        a_re = mag * jnp.cos(li * dt)
        a_im = mag * jnp.sin(li * dt)
        inv = 1.0 / (lr * lr + li * li)
        k_re = ((a_re - 1.0) * lr + a_im * li) * inv
        k_im = (a_im * lr - (a_re - 1.0) * li) * inv
        bb_re, bb_im = _cmul(k_re, k_im, btre_ref[j], btim_ref[j])
        c_re = cre_ref[j]
        c_im = cim_ref[j]

        pw = [(jnp.ones_like(a_re), jnp.zeros_like(a_re))]
        for _ in range(SUB):
            pw.append(_cmul(pw[-1][0], pw[-1][1], a_re, a_im))

        ws_rows, wot_rows, ca_re, ca_im = [], [], [], []
        for l in range(SUB):
            w_re, w_im = _cmul(pw[SUB - 1 - l][0], pw[SUB - 1 - l][1], bb_re, bb_im)
            ws_rows.append(jnp.concatenate([w_re, w_im], axis=-1))
            o_re, o_im = _cmul(c_re, c_im, pw[l + 1][0], pw[l + 1][1])
            wot_rows.append(jnp.concatenate([o_re, -o_im], axis=-1))
            g_re, g_im = _cmul(c_re, c_im, pw[l][0], pw[l][1])
            ca_re.append(g_re)
            ca_im.append(g_im)
        ws_ref[j] = jnp.concatenate(ws_rows, axis=0).astype(BF16)
        wot_ref[j] = jnp.concatenate(wot_rows, axis=0).astype(BF16)

        bbt_re = jnp.concatenate([bb_re] * SUB, axis=0)
        bbt_im = jnp.concatenate([bb_im] * SUB, axis=0)
        kw = (_dot_nt_f32(jnp.concatenate(ca_re, axis=0), bbt_re)
              - _dot_nt_f32(jnp.concatenate(ca_im, axis=0), bbt_im))
        toep = jnp.where(lane_blk == 0, kw, 0.0)
        for lp in range(1, SUB):
            shifted = jnp.concatenate(
                [jnp.zeros((SSM_GROUP * lp, S5_ROW), F32), kw[0:S5_ROW - SSM_GROUP * lp]], axis=0)
            toep = jnp.where(lane_blk == lp, shifted, toep)
        toept_ref[j] = toep.astype(BF16)

        s_re, s_im = pw[SUB]
        atab_ref[j] = jnp.concatenate([jnp.broadcast_to(s_re, (8, STATE_HALF)),
                                       jnp.broadcast_to(s_im, (8, STATE_HALF))], axis=0)
        for _ in range(4):
            s_re, s_im = _cmul(s_re, s_im, s_re, s_im)
        seg = jnp.zeros((8, STATE_HALF), F32)
        for i in range(3):
            seg = jnp.where(row8 == i, s_re, jnp.where(row8 == 3 + i, s_im, seg))
            s_re, s_im = _cmul(s_re, s_im, s_re, s_im)
        seg_ref[j] = seg


def _s5_prep(lam_re, lam_im, log_dt, b_re, b_im, c_re, c_im):
    depth, g, p = lam_re.shape
    cg = SSM_GROUP
    n = depth * g
    gp = S5_GROUPS_PER_STEP
    lane_pad = ((0, 0), (0, 0), (0, STATE_HALF - p))

    def state_rows(v, fill):
        v = v.reshape(n, -1, p)
        return jnp.pad(v, lane_pad, constant_values=fill)

    args = (state_rows(lam_re, -1.0), state_rows(lam_im, 0.0), log_dt.reshape(n, 1, 1),
            state_rows(b_re.transpose(0, 1, 3, 2), 0.0), state_rows(b_im.transpose(0, 1, 3, 2), 0.0),
            state_rows(c_re, 0.0), state_rows(c_im, 0.0))
    vec = pl.BlockSpec((gp, 1, STATE_HALF), lambda i: (i, 0, 0))
    mat = pl.BlockSpec((gp, cg, STATE_HALF), lambda i: (i, 0, 0))
    big = pl.BlockSpec((gp, S5_ROW, STATE_LANES), lambda i: (i, 0, 0))
    return pl.pallas_call(
        _s5_prep_kernel,
        out_shape=(jax.ShapeDtypeStruct((n, S5_ROW, S5_ROW), BF16),
                   jax.ShapeDtypeStruct((n, S5_ROW, STATE_LANES), BF16),
                   jax.ShapeDtypeStruct((n, S5_ROW, STATE_LANES), BF16),
                   jax.ShapeDtypeStruct((n, 16, STATE_HALF), F32),
                   jax.ShapeDtypeStruct((n, 8, STATE_HALF), F32)),
        grid=(n // gp,),
        in_specs=[vec, vec, pl.BlockSpec((gp, 1, 1), lambda i: (i, 0, 0)), mat, mat, mat, mat],
        out_specs=(big, big, big,
                   pl.BlockSpec((gp, 16, STATE_HALF), lambda i: (i, 0, 0)),
                   pl.BlockSpec((gp, 8, STATE_HALF), lambda i: (i, 0, 0))),
        compiler_params=pltpu.CompilerParams(
            dimension_semantics=("parallel",),
            vmem_limit_bytes=VMEM_LIMIT),
        name="s5_prep",
    )(*args)


def _s5_in_kernel(x_ref, mod_ref, gain_ref, wut_ref, wgt_ref, q_ref, zt_ref, gt_ref):
    d = D_MODEL

    def stage_norm(s):
        x = x_ref[s * L_PAIR:(s + 1) * L_PAIR].reshape(L_PAIR * BLK_LANES, d)
        return _modulated_norm(x, gain_ref[...], mod_ref[...]).astype(BF16)

    def stage_proj(s, h):
        ut = _dot_nt(wut_ref[...], h).astype(BF16)
        sgt = _dot_nt(wgt_ref[...], h)
        p = _dot(ut, q_ref[...]).astype(BF16)
        for l in range(L_PAIR):
            lanes = slice(l * BLK_LANES, (l + 1) * BLK_LANES)
            row0 = (s * L_PAIR + l) * SSM_GROUP
            zt_ref[:, row0:row0 + SSM_GROUP, :] = (
                p[:, lanes].reshape(SSM_GROUPS, SSM_GROUP, BLK_LANES))
            g = sgt[:, lanes]
            gt_ref[s * L_PAIR + l] = (g * _sigmoid(g)).astype(BF16)

    hs = [stage_norm(s) for s in range(OUT_STREAMS)]
    for s in range(OUT_STREAMS):
        stage_proj(s, hs[s])


def _s5_in(xp, mod_all, gain_all, w_ssm_t, qperm, layer):
    bsz = xp.shape[0]
    d = D_MODEL
    n_l = OUT_STREAMS * L_PAIR
    return pl.pallas_call(
        _s5_in_kernel,
        out_shape=(jax.ShapeDtypeStruct((SSM_GROUPS, S5_ROW, bsz * BLK_LANES), BF16),
                   jax.ShapeDtypeStruct((bsz, SUB, d, BLK_LANES), BF16)),
        grid=(bsz, SUB // n_l),
        in_specs=[
            pl.BlockSpec((None, n_l, N_SEG, SUBS_PER_SEG, d), lambda b, p: (b, p, 0, 0, 0)),
            pl.BlockSpec((None, None, 1, 3 * d), lambda b, p: (layer, b, 0, 0)),
            pl.BlockSpec((None, 1, d), lambda b, p: (layer, 0, 0)),
            pl.BlockSpec((None, d, d), lambda b, p: (layer, 0, 0)),
            pl.BlockSpec((None, d, d), lambda b, p: (layer, 1, 0)),
            pl.BlockSpec((L_PAIR * BLK_LANES, L_PAIR * BLK_LANES), lambda b, p: (0, 0)),
        ],
        out_specs=(
            pl.BlockSpec((SSM_GROUPS, n_l * SSM_GROUP, BLK_LANES), lambda b, p: (0, p, b)),
            pl.BlockSpec((None, n_l, d, BLK_LANES), lambda b, p: (b, p, 0, 0)),
        ),
        compiler_params=pltpu.CompilerParams(
            dimension_semantics=("parallel", "parallel"),
            vmem_limit_bytes=VMEM_LIMIT),
        name="s5_in",
    )(xp, mod_all, gain_all, w_ssm_t, w_ssm_t, qperm)


def _s5_core_kernel(zt_ref, toept_ref, ws_ref, wot_ref, atab_ref, seg_ref, dcol_ref,
                    yt_ref, loc_ref, sprev_ref, *, bsz):
    gs = S5_GROUPS_PER_STEP
    half = STATE_HALF
    row8 = lax.broadcasted_iota(jnp.int32, (N_SEG, half), 0)

    def shift_down(v, n):
        return jnp.where(row8 >= n, pltpu.roll(v, n, axis=0), 0.0)

    for j in range(gs):
        loc_ref[j] = _dot_tn(zt_ref[j], ws_ref[j])
        a_re = atab_ref[j, 0:8, :]
        a_im = atab_ref[j, 8:16, :]
        for b in range(bsz):
            base = b * BLK_LANES

            def rows(kl, base=base):
                return pl.ds(base + kl * N_SEG, N_SEG)

            e_re = jnp.zeros((N_SEG, half), F32)
            e_im = jnp.zeros((N_SEG, half), F32)
            for kl in range(SUBS_PER_SEG):
                e_re, e_im = _cmul_add(a_re, a_im, e_re, e_im,
                                       loc_ref[j, rows(kl), 0:half],
                                       loc_ref[j, rows(kl), half:STATE_LANES])
            for i, n in enumerate((1, 2, 4)):
                p_re = seg_ref[j, i:i + 1, :]
                p_im = seg_ref[j, 3 + i:4 + i, :]
                e_re, e_im = _cmul_add(p_re, p_im, shift_down(e_re, n), shift_down(e_im, n),
                                       e_re, e_im)
            s_re = shift_down(e_re, 1)
            s_im = shift_down(e_im, 1)
            for kl in range(SUBS_PER_SEG):
                sprev_ref[j, rows(kl), 0:half] = s_re
                sprev_ref[j, rows(kl), half:STATE_LANES] = s_im
                s_re, s_im = _cmul_add(a_re, a_im, s_re, s_im,
                                       loc_ref[j, rows(kl), 0:half],
                                       loc_ref[j, rows(kl), half:STATE_LANES])

        zt = zt_ref[j]
        out = (_dot(toept_ref[j], zt)
               + _dot_nt(wot_ref[j], sprev_ref[j].astype(BF16))
               + dcol_ref[j] * zt.astype(F32))
        yt_ref[j] = out.astype(BF16)


def _s5_core(zt, toept, ws, wot, atab, segtab, dcol, bsz, layer):
    g, rows, lanes = zt.shape
    gs = S5_GROUPS_PER_STEP
    off = layer * (g // gs)
    wspec = pl.BlockSpec((gs, S5_ROW, STATE_LANES), lambda i: (off + i, 0, 0))
    return pl.pallas_call(
        functools.partial(_s5_core_kernel, bsz=bsz),
        out_shape=jax.ShapeDtypeStruct((g, rows, lanes), BF16),
        grid=(g // gs,),
        in_specs=[
            pl.BlockSpec((gs, rows, lanes), lambda i: (i, 0, 0)),
            wspec, wspec, wspec,
            pl.BlockSpec((gs, 16, STATE_HALF), lambda i: (off + i, 0, 0)),
            pl.BlockSpec((gs, 8, STATE_HALF), lambda i: (off + i, 0, 0)),
            pl.BlockSpec((gs, S5_ROW, 1), lambda i: (off + i, 0, 0)),
        ],
        out_specs=pl.BlockSpec((gs, rows, lanes), lambda i: (i, 0, 0)),
        scratch_shapes=[pltpu.VMEM((gs, lanes, STATE_LANES), F32),
                        pltpu.VMEM((gs, lanes, STATE_LANES), F32)],
        compiler_params=pltpu.CompilerParams(
            dimension_semantics=("parallel",),
            vmem_limit_bytes=VMEM_LIMIT),
        name="s5_core",
    )(zt, toept, ws, wot, atab, segtab, dcol)


def _log_sigmoid(x):
    return -(jnp.maximum(-x, 0.0) + jnp.log1p(jnp.exp(-jnp.abs(x))))


def _chunk_time(idx):
    return ((idx & (SUBS_PER_SEG - 1)) * SUB) | (idx >> 4)


def _mlstm_kernel(x_ref, mod_ref, gain_ref, w3_ref, cw_ref, cb_ref,
                  wq_ref, wk_ref, wv_ref, wgt_ref, bgt_ref, ng_ref, skip_ref, o_ref,
                  ct_ref, m_ref, tail_ref, ext_ref, qkv_ref, proj_ref, xc_ref):
    L, dh, d = SEG_LEN, HEAD_DIM, D_MODEL
    halo = (CONV_WIDTH - 1) * SUBS_PER_SEG

    @pl.when(pl.program_id(1) == 0)
    def _():
        ct_ref[...] = jnp.zeros_like(ct_ref)
        m_ref[...] = jnp.zeros_like(m_ref)
        tail_ref[...] = jnp.zeros_like(tail_ref)

    x = x_ref[...].reshape(L, d)
    h = _modulated_norm(x, gain_ref[...], mod_ref[...]).astype(BF16)
    proj_ref[...] = _dot(h, w3_ref[...])
    m_in = proj_ref[:, 0:d]

    ext_ref[halo:halo + L, :] = m_in
    row16 = lax.broadcasted_iota(jnp.int32, (SUBS_PER_SEG, d), 0)
    for i in range(CONV_WIDTH - 1):
        lsrc = SUB - (CONV_WIDTH - 1) + i
        r0 = halo + lsrc * SUBS_PER_SEG
        shifted = ext_ref[pl.ds(r0 - 1, SUBS_PER_SEG), :]
        ext_ref[i * SUBS_PER_SEG:(i + 1) * SUBS_PER_SEG, :] = jnp.where(
            row16 == 0, tail_ref[i:i + 1, :], shifted)
        tail_ref[i:i + 1, :] = ext_ref[r0 + SUBS_PER_SEG - 1:r0 + SUBS_PER_SEG, :]
    acc = m_in * cw_ref[CONV_WIDTH - 1:CONV_WIDTH, :] + cb_ref[...]
    for j in range(CONV_WIDTH - 1):
        back = CONV_WIDTH - 1 - j
        acc = acc + ext_ref[pl.ds(halo - back * SUBS_PER_SEG, L), :] * cw_ref[j:j + 1, :]
    xc = acc * _sigmoid(acc)
    xc_ref[...] = xc

    xcb = xc.astype(BF16)
    minb = m_in.astype(BF16)
    for hd in range(HEADS):
        sl = slice(hd * dh, (hd + 1) * dh)
        qkv_ref[:, hd * dh:(hd + 1) * dh] = _dot(xcb[:, sl], wq_ref[hd]).astype(BF16)
        qkv_ref[:, d + hd * dh:d + (hd + 1) * dh] = (
            _dot(xcb[:, sl], wk_ref[hd]) * (dh ** -0.5)).astype(BF16)
        qkv_ref[:, 2 * d + hd * dh:2 * d + (hd + 1) * dh] = (
            _dot(minb[:, sl], wv_ref[hd]).astype(BF16))

    gates_t = _dot_nt(wgt_ref[...], qkv_ref[...]) + bgt_ref[...]

    t_row = _chunk_time(lax.broadcasted_iota(jnp.int32, (L, L), 0))
    t_col = _chunk_time(lax.broadcasted_iota(jnp.int32, (L, L), 1))
    causal = t_col <= t_row
    tri_u = jnp.where(t_row <= t_col, 1.0, 0.0).astype(BF16)

    lft_hi, lft_lo = _split_hi_lo(_log_sigmoid(gates_t))
    b_rows = _dot(lft_hi, tri_u) + _dot(lft_lo, tri_u)
    stacked = jnp.concatenate([gates_t, b_rows, jnp.zeros((128 - 16, L), F32)], axis=0)
    cols = stacked.T

    lane = lax.broadcasted_iota(jnp.int32, (L, 128), 1)
    ones_blk = jnp.where(lane == 0, 1.0, 0.0).astype(BF16)

    for hd in range(HEADS):
        sl = slice(hd * dh, (hd + 1) * dh)
        i_r = gates_t[hd:hd + 1, :]
        b_r = b_rows[HEADS + hd:HEADS + hd + 1, :]
        i_c = cols[:, hd:hd + 1]
        b_c = cols[:, 8 + HEADS + hd:8 + HEADS + hd + 1]
        m_prev = m_ref[hd:hd + 1, 0:1]

        log_d = jnp.where(causal, b_c + (i_r - b_r), -jnp.inf)
        m_inter = b_c + m_prev
        m_t = jnp.maximum(m_inter, jnp.max(log_d, axis=-1, keepdims=True))
        dmat = jnp.exp(log_d - m_t)
        qb = qkv_ref[:, sl]
        kb = qkv_ref[:, d + hd * dh:d + (hd + 1) * dh]
        s = _dot_nt(qb, kb)
        p = (s * dmat).astype(BF16)
        v_aug = jnp.concatenate([qkv_ref[:, 2 * d + hd * dh:2 * d + (hd + 1) * dh], ones_blk],
                                axis=-1)
        w_inter = jnp.exp(m_inter - m_t)
        inter = _dot(qb, ct_ref[hd].astype(BF16))
        intra = _dot(p, v_aug)
        nd = w_inter * inter + intra
        num = nd[:, 0:dh]
        den = nd[:, dh:dh + 1]
        hh = num / jnp.maximum(jnp.abs(den), jnp.exp(-m_t))

        b_tot = b_c[L - 1:L, :]
        log_w = b_tot - b_c + i_c
        m_next = jnp.maximum(b_tot + m_prev, jnp.max(log_w, axis=0, keepdims=True))
        decay = jnp.exp(b_tot + m_prev - m_next)
        w_c = jnp.exp(log_w - m_next)
        upd = _dot(kb.T, (w_c * v_aug.astype(F32)).astype(BF16))
        ct_ref[hd] = decay * ct_ref[hd] + upd
        m_ref[hd:hd + 1, :] = jnp.broadcast_to(m_next, (1, 128))

        hg = hh * _sigmoid(proj_ref[:, d + hd * dh:d + (hd + 1) * dh])
        mu = jnp.mean(hg, axis=-1, keepdims=True)
        dev = hg - mu
        var = jnp.mean(dev * dev, axis=-1, keepdims=True)
        hn = dev * lax.rsqrt(var + EPS) * ng_ref[:, sl] + skip_ref[:, sl] * xc_ref[:, sl]
        gate = proj_ref[:, 2 * d + hd * dh:2 * d + (hd + 1) * dh]
        y = hn * (gate * _sigmoid(gate))
        o_ref[:, :, sl] = y.reshape(SUB, SUBS_PER_SEG, dh).astype(o_ref.dtype)


def _mlstm(xp, mod_all, gain_all, w3, cw, cb, wq, wk, wv, wgt, bgt, ng, skip, layer):
    bsz = xp.shape[0]
    d = D_MODEL
    L = SEG_LEN
    lay2 = lambda shape: pl.BlockSpec((None,) + shape, lambda b, c: (layer, 0, 0))
    lay3 = lambda shape: pl.BlockSpec((None,) + shape, lambda b, c: (layer, 0, 0, 0))
    seg_spec = pl.BlockSpec((None, SUB, None, SUBS_PER_SEG, d), lambda b, c: (b, 0, c, 0, 0))
    halo = (CONV_WIDTH - 1) * SUBS_PER_SEG
    return pl.pallas_call(
        _mlstm_kernel,
        out_shape=jax.ShapeDtypeStruct(xp.shape, BF16),
        grid=(bsz, N_SEG),
        in_specs=[
            seg_spec,
            pl.BlockSpec((None, None, 1, 3 * d), lambda b, c: (layer, b, 0, 0)),
            lay2((1, d)),
            lay2((d, 3 * d)),
            lay2((CONV_WIDTH, d)),
            lay2((1, d)),
            lay3((HEADS, HEAD_DIM, HEAD_DIM)),
            lay3((HEADS, HEAD_DIM, HEAD_DIM)),
            lay3((HEADS, HEAD_DIM, HEAD_DIM)),
            lay2((8, 3 * d)),
            lay2((8, 1)),
            lay2((1, d)),
            lay2((1, d)),
        ],
        out_specs=seg_spec,
        scratch_shapes=[
            pltpu.VMEM((HEADS, HEAD_DIM, M_AUG), F32),
            pltpu.VMEM((8, 128), F32),
            pltpu.VMEM((8, d), F32),
            pltpu.VMEM((halo + L, d), F32),
            pltpu.VMEM((L, 3 * d), BF16),
            pltpu.VMEM((L, 3 * d), F32),
            pltpu.VMEM((L, d), F32),
        ],
        compiler_params=pltpu.CompilerParams(
            dimension_semantics=("parallel", "arbitrary"),
            vmem_limit_bytes=VMEM_LIMIT),
        name="mlstm",
    )(xp, mod_all, gain_all, w3, cw, cb, wq, wk, wv, wgt, bgt, ng, skip)


M_CHAINS = 2


def _mlstm2_kernel(x_ref, mod_ref, gain_ref, w_min_ref, w_mo_ref, w_mg_ref, cw_ref, cb_ref,
                   wq_ref, wk_ref, wv_ref, wg_ref, bg_ref, ng_ref, skip_ref, o_ref,
                   ct_ref, m_ref, tail_ref, ext_ref, qkv_ref, proj_ref, xc_ref, hn_ref, cvb_ref):
    L, dh, d = SEG_LEN, HEAD_DIM, D_MODEL
    halo = (CONV_WIDTH - 1) * SUBS_PER_SEG

    @pl.when(pl.program_id(1) == 0)
    def _():
        ct_ref[...] = jnp.zeros_like(ct_ref)
        m_ref[...] = jnp.zeros_like(m_ref)
        tail_ref[...] = jnp.zeros_like(tail_ref)

    t_row = _chunk_time(lax.broadcasted_iota(jnp.int32, (L, L), 0))
    t_col = _chunk_time(lax.broadcasted_iota(jnp.int32, (L, L), 1))
    causal = t_col <= t_row
    tri_l = jnp.where(causal, 1.0, 0.0).astype(BF16)
    lane = lax.broadcasted_iota(jnp.int32, (L, 128), 1)
    ones_blk = jnp.where(lane == 0, 1.0, 0.0).astype(BF16)
    row16 = lax.broadcasted_iota(jnp.int32, (SUBS_PER_SEG, d), 0)

    def stage_norm_in(c):
        x = x_ref[c].reshape(L, d)
        hn_ref[c] = _modulated_norm(x, gain_ref[...], mod_ref[c]).astype(BF16)
        proj_ref[c, :, 0:d] = _dot(hn_ref[c], w_min_ref[...])

    def stage_gate_proj(c):
        proj_ref[c, :, d:2 * d] = _dot(hn_ref[c], w_mo_ref[...])
        proj_ref[c, :, 2 * d:3 * d] = _dot(hn_ref[c], w_mg_ref[...])

    def stage_conv(c):
        m_in = proj_ref[c, :, 0:d]
        ext_ref[c, halo:halo + L, :] = m_in
        cvb_ref[c, :, d:2 * d] = m_in.astype(BF16)
        for i in range(CONV_WIDTH - 1):
            lsrc = SUB - (CONV_WIDTH - 1) + i
            r0 = halo + lsrc * SUBS_PER_SEG
            shifted = ext_ref[c, pl.ds(r0 - 1, SUBS_PER_SEG), :]
            ext_ref[c, i * SUBS_PER_SEG:(i + 1) * SUBS_PER_SEG, :] = jnp.where(
                row16 == 0, tail_ref[c, i:i + 1, :], shifted)
            tail_ref[c, i:i + 1, :] = ext_ref[c, r0 + SUBS_PER_SEG - 1:r0 + SUBS_PER_SEG, :]
        acc = m_in * cw_ref[CONV_WIDTH - 1:CONV_WIDTH, :] + cb_ref[...]
        for j in range(CONV_WIDTH - 1):
            back = CONV_WIDTH - 1 - j
            acc = acc + ext_ref[c, pl.ds(halo - back * SUBS_PER_SEG, L), :] * cw_ref[j:j + 1, :]
        xc = acc * _sigmoid(acc)
        xc_ref[c] = xc
        cvb_ref[c, :, 0:d] = xc.astype(BF16)

    def stage_qkv(c):
        for hd in range(HEADS):
            sl = slice(hd * dh, (hd + 1) * dh)
            xcb = cvb_ref[c, :, hd * dh:(hd + 1) * dh]
            qkv_ref[c, :, hd * dh:(hd + 1) * dh] = _dot(xcb, wq_ref[hd]).astype(BF16)
            qkv_ref[c, :, d + hd * dh:d + (hd + 1) * dh] = (
                _dot(xcb, wk_ref[hd]) * (dh ** -0.5)).astype(BF16)
            qkv_ref[c, :, 2 * d + hd * dh:2 * d + (hd + 1) * dh] = (
                _dot(cvb_ref[c, :, d + hd * dh:d + (hd + 1) * dh], wv_ref[hd]).astype(BF16))
        gates = _dot(qkv_ref[c], wg_ref[...]) + bg_ref[...]
        log_f = _log_sigmoid(gates)
        lf_hi, lf_lo = _split_hi_lo(log_f)
        b_cols = _dot(tri_l, lf_hi) + _dot(tri_l, lf_lo)
        return gates.T, b_cols.T, gates, b_cols

    def stage_head(c, hd, gate_forms):
        gates_t, b_rows, gates, b_cols = gate_forms
        sl = slice(hd * dh, (hd + 1) * dh)
        i_r = gates_t[hd:hd + 1, :]
        b_r = b_rows[HEADS + hd:HEADS + hd + 1, :]
        i_c = gates[:, hd:hd + 1]
        b_c = b_cols[:, HEADS + hd:HEADS + hd + 1]
        m_prev = m_ref[c, hd:hd + 1, 0:1]

        log_d = jnp.where(causal, b_c + (i_r - b_r), -jnp.inf)
        m_inter = b_c + m_prev
        m_t = jnp.maximum(m_inter, jnp.max(log_d, axis=-1, keepdims=True))
        dmat = jnp.exp(log_d - m_t)
        qb = qkv_ref[c, :, sl]
        kb = qkv_ref[c, :, d + hd * dh:d + (hd + 1) * dh]
        s = _dot_nt(qb, kb)
        p = (s * dmat).astype(BF16)
        v_aug = jnp.concatenate([qkv_ref[c, :, 2 * d + hd * dh:2 * d + (hd + 1) * dh], ones_blk],
                                axis=-1)
        w_inter = jnp.exp(m_inter - m_t)
        inter = _dot(qb, ct_ref[c, hd].astype(BF16))
        intra = _dot(p, v_aug)
        nd = w_inter * inter + intra
        num = nd[:, 0:dh]
        den = nd[:, dh:dh + 1]
        hh = num / jnp.maximum(jnp.abs(den), jnp.exp(-m_t))

        b_tot = b_c[L - 1:L, :]
        log_w = b_tot - b_c + i_c
        m_next = jnp.maximum(b_tot + m_prev, jnp.max(log_w, axis=0, keepdims=True))
        decay = jnp.exp(b_tot + m_prev - m_next)
        w_c = jnp.exp(log_w - m_next)
        upd = _dot(kb.T, (w_c * v_aug.astype(F32)).astype(BF16))
        ct_ref[c, hd] = decay * ct_ref[c, hd] + upd
        m_ref[c, hd:hd + 1, :] = jnp.broadcast_to(m_next, (1, 128))

        hg = hh * _sigmoid(proj_ref[c, :, d + hd * dh:d + (hd + 1) * dh])
        mu = jnp.mean(hg, axis=-1, keepdims=True)
        dev = hg - mu
        var = jnp.mean(dev * dev, axis=-1, keepdims=True)
        hn = dev * lax.rsqrt(var + EPS) * ng_ref[:, sl] + skip_ref[:, sl] * xc_ref[c, :, sl]
        gate = proj_ref[c, :, 2 * d + hd * dh:2 * d + (hd + 1) * dh]
        y = hn * (gate * _sigmoid(gate))
        o_ref[c, :, :, sl] = y.reshape(SUB, SUBS_PER_SEG, dh).astype(o_ref.dtype)

    a, b = 0, 1
    stage_norm_in(a)
    stage_conv(a)
    stage_norm_in(b)
    stage_gate_proj(a)
    gf_a = stage_qkv(a)
    stage_conv(b)
    stage_gate_proj(b)
    stage_head(a, 0, gf_a)
    gf_b = stage_qkv(b)
    stage_head(a, 1, gf_a)
    stage_head(b, 0, gf_b)
    stage_head(a, 2, gf_a)
    stage_head(b, 1, gf_b)
    stage_head(a, 3, gf_a)
    stage_head(b, 2, gf_b)
    stage_head(b, 3, gf_b)


def _mlstm2(xp, mod_all, gain_all, w_in_b, cw, cb, wq, wk, wv, wg, bg, ng, skip, layer):
    bsz = xp.shape[0]
    d = D_MODEL
    L = SEG_LEN
    nc = M_CHAINS
    lay2 = lambda shape: pl.BlockSpec((None,) + shape, lambda b, c: (layer, 0, 0))
    lay3 = lambda shape: pl.BlockSpec((None,) + shape, lambda b, c: (layer, 0, 0, 0))
    seg_spec = pl.BlockSpec((nc, SUB, None, SUBS_PER_SEG, d), lambda b, c: (b, 0, c, 0, 0))
    w_cols = lambda j: pl.BlockSpec((None, d, d), lambda b, c: (layer, 0, j))
    halo = (CONV_WIDTH - 1) * SUBS_PER_SEG
    return pl.pallas_call(
        _mlstm2_kernel,
        out_shape=jax.ShapeDtypeStruct(xp.shape, BF16),
        grid=(bsz // nc, N_SEG),
        in_specs=[
            seg_spec,
            pl.BlockSpec((None, nc, 1, 3 * d), lambda b, c: (layer, b, 0, 0)),
            lay2((1, d)),
            w_cols(2), w_cols(3), w_cols(4),
            lay2((CONV_WIDTH, d)),
            lay2((1, d)),
            lay3((HEADS, HEAD_DIM, HEAD_DIM)),
            lay3((HEADS, HEAD_DIM, HEAD_DIM)),
            lay3((HEADS, HEAD_DIM, HEAD_DIM)),
            lay2((3 * d, 128)),
            lay2((1, 128)),
            lay2((1, d)),
            lay2((1, d)),
        ],
        out_specs=seg_spec,
        scratch_shapes=[
            pltpu.VMEM((nc, HEADS, HEAD_DIM, M_AUG), F32),
            pltpu.VMEM((nc, 8, 128), F32),
            pltpu.VMEM((nc, 8, d), F32),
            pltpu.VMEM((nc, halo + L, d), F32),
            pltpu.VMEM((nc, L, 3 * d), BF16),
            pltpu.VMEM((nc, L, 3 * d), F32),
            pltpu.VMEM((nc, L, d), F32),
            pltpu.VMEM((nc, L, d), BF16),
            pltpu.VMEM((nc, L, 2 * d), BF16),
        ],
        compiler_params=pltpu.CompilerParams(
            dimension_semantics=("parallel", "arbitrary"),
            vmem_limit_bytes=VMEM_LIMIT),
        name="mlstm",
    )(xp, mod_all, gain_all, w_in_b, w_in_b, w_in_b, cw, cb, wq, wk, wv, wg, bg, ng, skip)


def _gelu_tanh(x):
    inner = math.sqrt(2.0 / math.pi) * (x + 0.044715 * (x * x * x))
    return x * (0.5 * (1.0 + jnp.tanh(inner)))


def _out_kernel(yt_ref, gt_ref, my_ref, x_ref, mod_ref, bglu_ref, og_ref, fg_ref,
                wglut_ref, wout_ref, qt_ref, o_ref, *, final):
    d = D_MODEL
    rows = L_PAIR * BLK_LANES

    def stage_in(s):
        ls = range(s * L_PAIR, (s + 1) * L_PAIR)
        out_m = _dot(my_ref[s * L_PAIR:(s + 1) * L_PAIR].reshape(rows, d),
                     wout_ref[d:2 * d, :])
        yt = jnp.concatenate(
            [yt_ref[:, l * SSM_GROUP:(l + 1) * SSM_GROUP, :].reshape(d, BLK_LANES) for l in ls],
            axis=-1)
        y = _gelu_tanh(_dot(yt, qt_ref[...]))
        return out_m, y

    def stage_glu(s, y):
        ls = range(s * L_PAIR, (s + 1) * L_PAIR)
        glu = y * _sigmoid(_dot(wglut_ref[...], y.astype(BF16)) + bglu_ref[...])
        gate_s = jnp.concatenate([gt_ref[l] for l in ls], axis=-1).astype(F32)
        ms = jnp.mean(glu * glu, axis=0, keepdims=True)
        return (glu * lax.rsqrt(ms + EPS) * og_ref[...] * gate_s).astype(BF16)

    def stage_out(s, out_m, ssm_y):
        out = _dot_tn(ssm_y, wout_ref[0:d, :]) + out_m
        xn = x_ref[s * L_PAIR:(s + 1) * L_PAIR].reshape(rows, d) + mod_ref[:, 2 * d:3 * d] * out
        if final:
            xn = _rms(xn) * fg_ref[...]
        o_ref[s * L_PAIR:(s + 1) * L_PAIR] = xn.reshape(L_PAIR, N_SEG, SUBS_PER_SEG, d)

    streams = range(OUT_STREAMS)
    ins = [stage_in(s) for s in streams]
    ssm = [stage_glu(s, ins[s][1]) for s in streams]
    for s in streams:
        stage_out(s, ins[s][0], ssm[s])


def _out_stage(yt, gt, my, xp, mod_all, bglu_col, og_col, fg, wglut, wout, qperm_t, layer, final):
    bsz = xp.shape[0]
    d = D_MODEL
    n_l = OUT_STREAMS * L_PAIR
    tok = pl.BlockSpec((None, n_l, N_SEG, SUBS_PER_SEG, d), lambda b, p: (b, p, 0, 0, 0))
    col = pl.BlockSpec((None, d, 1), lambda b, p: (layer, 0, 0))
    return pl.pallas_call(
        functools.partial(_out_kernel, final=final),
        out_shape=jax.ShapeDtypeStruct(xp.shape, F32),
        grid=(bsz, SUB // n_l),
        in_specs=[
            pl.BlockSpec((SSM_GROUPS, n_l * SSM_GROUP, BLK_LANES), lambda b, p: (0, p, b)),
            pl.BlockSpec((None, n_l, d, BLK_LANES), lambda b, p: (b, p, 0, 0)),
            tok, tok,
            pl.BlockSpec((None, None, 1, 3 * d), lambda b, p: (layer, b, 0, 0)),
            col, col,
            pl.BlockSpec((1, d), lambda b, p: (0, 0)),
            pl.BlockSpec((None, d, d), lambda b, p: (layer, 0, 0)),
            pl.BlockSpec((None, 2 * d, d), lambda b, p: (layer, 0, 0)),
            pl.BlockSpec((L_PAIR * BLK_LANES, L_PAIR * BLK_LANES), lambda b, p: (0, 0)),
        ],
        out_specs=tok,
        compiler_params=pltpu.CompilerParams(
            dimension_semantics=("parallel", "parallel"),
            vmem_limit_bytes=VMEM_LIMIT),
        name="out_stage",
    )(yt, gt, my, xp, mod_all, bglu_col, og_col, fg, wglut, wout, qperm_t)


def _lane_permutation():
    q = np.zeros((BLK_LANES, BLK_LANES), np.float32)
    for seg in range(N_SEG):
        for kl in range(SUBS_PER_SEG):
            q[seg * SUBS_PER_SEG + kl, kl * N_SEG + seg] = 1.0
    return np.kron(np.eye(L_PAIR, dtype=np.float32), q)


def kernel(x, c, norm_gain, w_mod, b_mod, w_in, ssm_lambda_re, ssm_lambda_im, ssm_log_dt,
           ssm_b_re, ssm_b_im, ssm_c_re, ssm_c_im, ssm_d, ssm_w_glu, ssm_b_glu,
           ssm_out_gain, m_conv_w, m_conv_b, m_wq, m_wk, m_wv, m_w_gates, m_b_igate,
           m_b_fgate, m_norm_gain, m_skip, w_out, final_gain):
    bsz, seq, d = x.shape
    depth = w_in.shape[0]
    assert d == D_MODEL and seq == SEQ

    xp = x.reshape(bsz, N_SEG, SUBS_PER_SEG, SUB, d).transpose(0, 3, 1, 2, 4)
    mod_all = _modulation(c, w_mod, b_mod).reshape(depth, bsz, 1, 3 * d)
    rows = lambda v: v.reshape(depth, 1, -1)
    cols = lambda v: v.reshape(depth, -1, 1)
    qperm = jnp.asarray(_lane_permutation(), BF16)
    qperm_t = jnp.asarray(_lane_permutation().T, BF16)

    gain_all = rows(norm_gain)
    w_in_b = w_in.astype(BF16)
    w_ssm_t = w_in_b[:, :, 0:2 * d].transpose(0, 2, 1)
    toept, ws, wot, atab, segtab = _s5_prep(ssm_lambda_re, ssm_lambda_im, ssm_log_dt,
                                            ssm_b_re, ssm_b_im, ssm_c_re, ssm_c_im)
    dcol = jnp.broadcast_to(ssm_d.reshape(depth * SSM_GROUPS, 1, SSM_GROUP),
                            (depth * SSM_GROUPS, SUB, SSM_GROUP)).reshape(-1, S5_ROW, 1)
    wq, wk, wv = m_wq.astype(BF16), m_wk.astype(BF16), m_wv.astype(BF16)
    gate_pad = ((0, 0), (0, 0), (0, 128 - 2 * HEADS))
    wg = jnp.pad(m_w_gates, gate_pad).astype(BF16)
    bg = jnp.pad(jnp.concatenate([m_b_igate, m_b_fgate], axis=-1).reshape(depth, 1, 2 * HEADS),
                 gate_pad)
    wglut = ssm_w_glu.astype(BF16).transpose(0, 2, 1)
    wout = w_out.astype(BF16)
    fg = final_gain.reshape(1, d)

    for l in range(depth):
        zt, gt = _s5_in(xp, mod_all, gain_all, w_ssm_t, qperm, l)
        yt = _s5_core(zt, toept, ws, wot, atab, segtab, dcol, bsz, l)
        my = _mlstm2(xp, mod_all, gain_all, w_in_b, m_conv_w, rows(m_conv_b), wq, wk, wv, wg, bg,
                    rows(m_norm_gain), rows(m_skip), l)
        xp = _out_stage(yt, gt, my, xp, mod_all, cols(ssm_b_glu), cols(ssm_out_gain), fg,
                        wglut, wout, qperm_t, l, final=(l == depth - 1))

    return xp.transpose(0, 2, 3, 1, 4).reshape(bsz, seq, d)
```

```python
import functools
import math

import numpy as np
import jax
import jax.numpy as jnp
from jax import lax
from jax.experimental import pallas as pl
from jax.experimental.pallas import tpu as pltpu

F32 = jnp.float32
BF16 = jnp.bfloat16

D_MODEL = 1024
SSM_GROUP = 16
SSM_GROUPS = D_MODEL // SSM_GROUP
SSM_STATE = 64
HEADS = 4
HEAD_DIM = D_MODEL // HEADS
CONV_WIDTH = 4
EPS = 1e-6

SUB = 16
SUBS_PER_SEG = 16
SEG_LEN = SUB * SUBS_PER_SEG
N_SEG = 8
SEQ = N_SEG * SEG_LEN
BLK_LANES = N_SEG * SUBS_PER_SEG
S5_ROW = SUB * SSM_GROUP
STATE_HALF = 128
STATE_LANES = 2 * STATE_HALF
S5_GROUPS_PER_STEP = 4
L_PAIR = 2
OUT_STREAMS = 4
M_AUG = HEAD_DIM + 128
VMEM_LIMIT = 56 * 1024 * 1024

_NT = (((1,), (1,)), ((), ()))
_TN = (((0,), (0,)), ((), ()))


def _dot(a, b):
    return jnp.dot(a, b, preferred_element_type=F32)


def _dot_nt(a, b):
    return lax.dot_general(a, b, _NT, preferred_element_type=F32)


def _dot_tn(a, b):
    return lax.dot_general(a, b, _TN, preferred_element_type=F32)


def _sigmoid(x):
    return jax.nn.sigmoid(x)


def _rms(x):
    return x * lax.rsqrt(jnp.mean(x * x, axis=-1, keepdims=True) + EPS)


def _modulated_norm(x, gain, mod):
    shift = mod[:, 0:D_MODEL]
    scale = mod[:, D_MODEL:2 * D_MODEL]
    return _rms(x) * gain * (1.0 + scale) + shift


def _split_hi_lo(v):
    hi = v.astype(BF16)
    lo = (v - hi.astype(F32)).astype(BF16)
    return hi, lo


def _dot_nt_f32(a, b):
    a_hi, a_lo = _split_hi_lo(a)
    b_hi, b_lo = _split_hi_lo(b)
    return _dot_nt(a_hi, b_hi) + _dot_nt(a_hi, b_lo) + _dot_nt(a_lo, b_hi)


def _cmul(a_re, a_im, b_re, b_im):
    return a_re * b_re - a_im * b_im, a_re * b_im + a_im * b_re


def _cmul_add(p_re, p_im, s_re, s_im, add_re, add_im):
    return (p_re * s_re - p_im * s_im + add_re,
            p_re * s_im + p_im * s_re + add_im)


def _mod_kernel(c_ref, w_ref, b_ref, o_ref):
    cv = c_ref[...]
    act = cv * _sigmoid(cv)
    o_ref[...] = _dot(act.astype(BF16), w_ref[...].astype(BF16)) + b_ref[...]


def _modulation(c, w_mod, b_mod):
    depth, d, n = w_mod.shape
    bsz = c.shape[0]
    return pl.pallas_call(
        _mod_kernel,
        out_shape=jax.ShapeDtypeStruct((depth, bsz, n), F32),
        grid=(depth, n // d),
        in_specs=[
            pl.BlockSpec((bsz, d), lambda l, j: (0, 0)),
            pl.BlockSpec((None, d, d), lambda l, j: (l, 0, j)),
            pl.BlockSpec((None, 1, d), lambda l, j: (l, 0, j)),
        ],
        out_specs=pl.BlockSpec((None, bsz, d), lambda l, j: (l, 0, j)),
        compiler_params=pltpu.CompilerParams(
            dimension_semantics=("parallel", "parallel"),
            vmem_limit_bytes=VMEM_LIMIT),
        name="adaln_mod",
    )(c, w_mod, b_mod.reshape(depth, 1, n))


def _s5_prep_kernel(lre_ref, lim_ref, ldt_ref, btre_ref, btim_ref, cre_ref, cim_ref,
                    toept_ref, ws_ref, wot_ref, atab_ref, seg_ref):
    gp = S5_GROUPS_PER_STEP
    lane_blk = lax.broadcasted_iota(jnp.int32, (S5_ROW, S5_ROW), 1) // SSM_GROUP
    row8 = lax.broadcasted_iota(jnp.int32, (8, STATE_HALF), 0)
    for j in range(gp):
        lr = lre_ref[j]
        li = lim_ref[j]
        dt = jnp.exp(ldt_ref[j])
        mag = jnp.exp(lr * dt)
        a_re = mag * jnp.cos(li * dt)
        a_im = mag * jnp.sin(li * dt)
        inv = 1.0 / (lr * lr + li * li)
        k_re = ((a_re - 1.0) * lr + a_im * li) * inv
        k_im = (a_im * lr - (a_re - 1.0) * li) * inv
        bb_re, bb_im = _cmul(k_re, k_im, btre_ref[j], btim_ref[j])
        c_re = cre_ref[j]
        c_im = cim_ref[j]

        pw = [(jnp.ones_like(a_re), jnp.zeros_like(a_re))]
        for _ in range(SUB):
            pw.append(_cmul(pw[-1][0], pw[-1][1], a_re, a_im))

        ws_rows, wot_rows, ca_re, ca_im = [], [], [], []
        for l in range(SUB):
            w_re, w_im = _cmul(pw[SUB - 1 - l][0], pw[SUB - 1 - l][1], bb_re, bb_im)
            ws_rows.append(jnp.concatenate([w_re, w_im], axis=-1))
            o_re, o_im = _cmul(c_re, c_im, pw[l + 1][0], pw[l + 1][1])
            wot_rows.append(jnp.concatenate([o_re, -o_im], axis=-1))
            g_re, g_im = _cmul(c_re, c_im, pw[l][0], pw[l][1])
            ca_re.append(g_re)
            ca_im.append(g_im)
        ws_ref[j] = jnp.concatenate(ws_rows, axis=0).astype(BF16)
        wot_ref[j] = jnp.concatenate(wot_rows, axis=0).astype(BF16)

        bbt_re = jnp.concatenate([bb_re] * SUB, axis=0)
        bbt_im = jnp.concatenate([bb_im] * SUB, axis=0)
        kw = (_dot_nt_f32(jnp.concatenate(ca_re, axis=0), bbt_re)
              - _dot_nt_f32(jnp.concatenate(ca_im, axis=0), bbt_im))
        toep = jnp.where(lane_blk == 0, kw, 0.0)
        for lp in range(1, SUB):
            shifted = jnp.concatenate(
                [jnp.zeros((SSM_GROUP * lp, S5_ROW), F32), kw[0:S5_ROW - SSM_GROUP * lp]], axis=0)
            toep = jnp.where(lane_blk == lp, shifted, toep)
        toept_ref[j] = toep.astype(BF16)

        s_re, s_im = pw[SUB]
        atab_ref[j] = jnp.concatenate([jnp.broadcast_to(s_re, (8, STATE_HALF)),
                                       jnp.broadcast_to(s_im, (8, STATE_HALF))], axis=0)
        for _ in range(4):
            s_re, s_im = _cmul(s_re, s_im, s_re, s_im)
        seg = jnp.zeros((8, STATE_HALF), F32)
        for i in range(3):
            seg = jnp.where(row8 == i, s_re, jnp.where(row8 == 3 + i, s_im, seg))
            s_re, s_im = _cmul(s_re, s_im, s_re, s_im)
        seg_ref[j] = seg


def _s5_prep(lam_re, lam_im, log_dt, b_re, b_im, c_re, c_im):
    depth, g, p = lam_re.shape
    cg = SSM_GROUP
    n = depth * g
    gp = S5_GROUPS_PER_STEP
    lane_pad = ((0, 0), (0, 0), (0, STATE_HALF - p))

    def state_rows(v, fill):
        v = v.reshape(n, -1, p)
        return jnp.pad(v, lane_pad, constant_values=fill)

    args = (state_rows(lam_re, -1.0), state_rows(lam_im, 0.0), log_dt.reshape(n, 1, 1),
            state_rows(b_re.transpose(0, 1, 3, 2), 0.0), state_rows(b_im.transpose(0, 1, 3, 2), 0.0),
            state_rows(c_re, 0.0), state_rows(c_im, 0.0))
    vec = pl.BlockSpec((gp, 1, STATE_HALF), lambda i: (i, 0, 0))
    mat = pl.BlockSpec((gp, cg, STATE_HALF), lambda i: (i, 0, 0))
    big = pl.BlockSpec((gp, S5_ROW, STATE_LANES), lambda i: (i, 0, 0))
    return pl.pallas_call(
        _s5_prep_kernel,
        out_shape=(jax.ShapeDtypeStruct((n, S5_ROW, S5_ROW), BF16),
                   jax.ShapeDtypeStruct((n, S5_ROW, STATE_LANES), BF16),
                   jax.ShapeDtypeStruct((n, S5_ROW, STATE_LANES), BF16),
                   jax.ShapeDtypeStruct((n, 16, STATE_HALF), F32),
                   jax.ShapeDtypeStruct((n, 8, STATE_HALF), F32)),
        grid=(n // gp,),
        in_specs=[vec, vec, pl.BlockSpec((gp, 1, 1), lambda i: (i, 0, 0)), mat, mat, mat, mat],
        out_specs=(big, big, big,
                   pl.BlockSpec((gp, 16, STATE_HALF), lambda i: (i, 0, 0)),
                   pl.BlockSpec((gp, 8, STATE_HALF), lambda i: (i, 0, 0))),
        compiler_params=pltpu.CompilerParams(
            dimension_semantics=("parallel",),
            vmem_limit_bytes=VMEM_LIMIT),
        name="s5_prep",
    )(*args)


def _s5_in_kernel(x_ref, mod_ref, gain_ref, wut_ref, wgt_ref, q_ref, zt_ref, gt_ref):
    d = D_MODEL

    def stage_norm(s):
        x = x_ref[s * L_PAIR:(s + 1) * L_PAIR].reshape(L_PAIR * BLK_LANES, d)
        return _modulated_norm(x, gain_ref[...], mod_ref[...]).astype(BF16)

    def stage_proj(s, h):
        ut = _dot_nt(wut_ref[...], h).astype(BF16)
        sgt = _dot_nt(wgt_ref[...], h)
        p = _dot(ut, q_ref[...]).astype(BF16)
        for l in range(L_PAIR):
            lanes = slice(l * BLK_LANES, (l + 1) * BLK_LANES)
            row0 = (s * L_PAIR + l) * SSM_GROUP
            zt_ref[:, row0:row0 + SSM_GROUP, :] = (
                p[:, lanes].reshape(SSM_GROUPS, SSM_GROUP, BLK_LANES))
            g = sgt[:, lanes]
            gt_ref[s * L_PAIR + l] = (g * _sigmoid(g)).astype(BF16)

    hs = [stage_norm(s) for s in range(OUT_STREAMS)]
    for s in range(OUT_STREAMS):
        stage_proj(s, hs[s])


def _s5_in(xp, mod_all, gain_all, w_ssm_t, qperm, layer):
    bsz = xp.shape[0]
    d = D_MODEL
    n_l = OUT_STREAMS * L_PAIR
    return pl.pallas_call(
        _s5_in_kernel,
        out_shape=(jax.ShapeDtypeStruct((SSM_GROUPS, S5_ROW, bsz * BLK_LANES), BF16),
                   jax.ShapeDtypeStruct((bsz, SUB, d, BLK_LANES), BF16)),
        grid=(bsz, SUB // n_l),
        in_specs=[
            pl.BlockSpec((None, n_l, N_SEG, SUBS_PER_SEG, d), lambda b, p: (b, p, 0, 0, 0)),
            pl.BlockSpec((None, None, 1, 3 * d), lambda b, p: (layer, b, 0, 0)),
            pl.BlockSpec((None, 1, d), lambda b, p: (layer, 0, 0)),
            pl.BlockSpec((None, d, d), lambda b, p: (layer, 0, 0)),
            pl.BlockSpec((None, d, d), lambda b, p: (layer, 1, 0)),
            pl.BlockSpec((L_PAIR * BLK_LANES, L_PAIR * BLK_LANES), lambda b, p: (0, 0)),
        ],
        out_specs=(
            pl.BlockSpec((SSM_GROUPS, n_l * SSM_GROUP, BLK_LANES), lambda b, p: (0, p, b)),
            pl.BlockSpec((None, n_l, d, BLK_LANES), lambda b, p: (b, p, 0, 0)),
        ),
        compiler_params=pltpu.CompilerParams(
            dimension_semantics=("parallel", "parallel"),
            vmem_limit_bytes=VMEM_LIMIT),
        name="s5_in",
    )(xp, mod_all, gain_all, w_ssm_t, w_ssm_t, qperm)


def _s5_core_kernel(zt_ref, toept_ref, ws_ref, wot_ref, atab_ref, seg_ref, dcol_ref,
                    yt_ref, loc_ref, sprev_ref, *, bsz):
    gs = S5_GROUPS_PER_STEP
    half = STATE_HALF
    row8 = lax.broadcasted_iota(jnp.int32, (N_SEG, half), 0)

    def shift_down(v, n):
        return jnp.where(row8 >= n, pltpu.roll(v, n, axis=0), 0.0)

    for j in range(gs):
        loc_ref[j] = _dot_tn(zt_ref[j], ws_ref[j])
        a_re = atab_ref[j, 0:8, :]
        a_im = atab_ref[j, 8:16, :]
        for b in range(bsz):
            base = b * BLK_LANES

            def rows(kl, base=base):
                return pl.ds(base + kl * N_SEG, N_SEG)

            e_re = jnp.zeros((N_SEG, half), F32)
            e_im = jnp.zeros((N_SEG, half), F32)
            for kl in range(SUBS_PER_SEG):
                e_re, e_im = _cmul_add(a_re, a_im, e_re, e_im,
                                       loc_ref[j, rows(kl), 0:half],
                                       loc_ref[j, rows(kl), half:STATE_LANES])
            for i, n in enumerate((1, 2, 4)):
                p_re = seg_ref[j, i:i + 1, :]
                p_im = seg_ref[j, 3 + i:4 + i, :]
                e_re, e_im = _cmul_add(p_re, p_im, shift_down(e_re, n), shift_down(e_im, n),
                                       e_re, e_im)
            s_re = shift_down(e_re, 1)
            s_im = shift_down(e_im, 1)
            for kl in range(SUBS_PER_SEG):
                sprev_ref[j, rows(kl), 0:half] = s_re
                sprev_ref[j, rows(kl), half:STATE_LANES] = s_im
                s_re, s_im = _cmul_add(a_re, a_im, s_re, s_im,
                                       loc_ref[j, rows(kl), 0:half],
                                       loc_ref[j, rows(kl), half:STATE_LANES])

        zt = zt_ref[j]
        out = (_dot(toept_ref[j], zt)
               + _dot_nt(wot_ref[j], sprev_ref[j].astype(BF16))
               + dcol_ref[j] * zt.astype(F32))
        yt_ref[j] = out.astype(BF16)


def _s5_core(zt, toept, ws, wot, atab, segtab, dcol, bsz, layer):
    g, rows, lanes = zt.shape
    gs = S5_GROUPS_PER_STEP
    off = layer * (g // gs)
    wspec = pl.BlockSpec((gs, S5_ROW, STATE_LANES), lambda i: (off + i, 0, 0))
    return pl.pallas_call(
        functools.partial(_s5_core_kernel, bsz=bsz),
        out_shape=jax.ShapeDtypeStruct((g, rows, lanes), BF16),
        grid=(g // gs,),
        in_specs=[
            pl.BlockSpec((gs, rows, lanes), lambda i: (i, 0, 0)),
            wspec, wspec, wspec,
            pl.BlockSpec((gs, 16, STATE_HALF), lambda i: (off + i, 0, 0)),
            pl.BlockSpec((gs, 8, STATE_HALF), lambda i: (off + i, 0, 0)),
            pl.BlockSpec((gs, S5_ROW, 1), lambda i: (off + i, 0, 0)),
        ],
        out_specs=pl.BlockSpec((gs, rows, lanes), lambda i: (i, 0, 0)),
        scratch_shapes=[pltpu.VMEM((gs, lanes, STATE_LANES), F32),
                        pltpu.VMEM((gs, lanes, STATE_LANES), F32)],
        compiler_params=pltpu.CompilerParams(
            dimension_semantics=("parallel",),
            vmem_limit_bytes=VMEM_LIMIT),
        name="s5_core",
    )(zt, toept, ws, wot, atab, segtab, dcol)


def _log_sigmoid(x):
    return -(jnp.maximum(-x, 0.0) + jnp.log1p(jnp.exp(-jnp.abs(x))))


def _chunk_time(idx):
    return ((idx & (SUBS_PER_SEG - 1)) * SUB) | (idx >> 4)


def _mlstm_kernel(x_ref, mod_ref, gain_ref, w3_ref, cw_ref, cb_ref,
                  wq_ref, wk_ref, wv_ref, wgt_ref, bgt_ref, ng_ref, skip_ref, o_ref,
                  ct_ref, m_ref, tail_ref, ext_ref, qkv_ref, proj_ref, xc_ref):
    L, dh, d = SEG_LEN, HEAD_DIM, D_MODEL
    halo = (CONV_WIDTH - 1) * SUBS_PER_SEG

    @pl.when(pl.program_id(1) == 0)
    def _():
        ct_ref[...] = jnp.zeros_like(ct_ref)
        m_ref[...] = jnp.zeros_like(m_ref)
        tail_ref[...] = jnp.zeros_like(tail_ref)

    x = x_ref[...].reshape(L, d)
    h = _modulated_norm(x, gain_ref[...], mod_ref[...]).astype(BF16)
    proj_ref[...] = _dot(h, w3_ref[...])
    m_in = proj_ref[:, 0:d]

    ext_ref[halo:halo + L, :] = m_in
    row16 = lax.broadcasted_iota(jnp.int32, (SUBS_PER_SEG, d), 0)
    for i in range(CONV_WIDTH - 1):
        lsrc = SUB - (CONV_WIDTH - 1) + i
        r0 = halo + lsrc * SUBS_PER_SEG
        shifted = ext_ref[pl.ds(r0 - 1, SUBS_PER_SEG), :]
        ext_ref[i * SUBS_PER_SEG:(i + 1) * SUBS_PER_SEG, :] = jnp.where(
            row16 == 0, tail_ref[i:i + 1, :], shifted)
        tail_ref[i:i + 1, :] = ext_ref[r0 + SUBS_PER_SEG - 1:r0 + SUBS_PER_SEG, :]
    acc = m_in * cw_ref[CONV_WIDTH - 1:CONV_WIDTH, :] + cb_ref[...]
    for j in range(CONV_WIDTH - 1):
        back = CONV_WIDTH - 1 - j
        acc = acc + ext_ref[pl.ds(halo - back * SUBS_PER_SEG, L), :] * cw_ref[j:j + 1, :]
    xc = acc * _sigmoid(acc)
    xc_ref[...] = xc

    xcb = xc.astype(BF16)
    minb = m_in.astype(BF16)
    for hd in range(HEADS):
        sl = slice(hd * dh, (hd + 1) * dh)
        qkv_ref[:, hd * dh:(hd + 1) * dh] = _dot(xcb[:, sl], wq_ref[hd]).astype(BF16)
        qkv_ref[:, d + hd * dh:d + (hd + 1) * dh] = (
            _dot(xcb[:, sl], wk_ref[hd]) * (dh ** -0.5)).astype(BF16)
        qkv_ref[:, 2 * d + hd * dh:2 * d + (hd + 1) * dh] = (
            _dot(minb[:, sl], wv_ref[hd]).astype(BF16))

    gates_t = _dot_nt(wgt_ref[...], qkv_ref[...]) + bgt_ref[...]

    t_row = _chunk_time(lax.broadcasted_iota(jnp.int32, (L, L), 0))
    t_col = _chunk_time(lax.broadcasted_iota(jnp.int32, (L, L), 1))
    causal = t_col <= t_row
    tri_u = jnp.where(t_row <= t_col, 1.0, 0.0).astype(BF16)

    lft_hi, lft_lo = _split_hi_lo(_log_sigmoid(gates_t))
    b_rows = _dot(lft_hi, tri_u) + _dot(lft_lo, tri_u)
    stacked = jnp.concatenate([gates_t, b_rows, jnp.zeros((128 - 16, L), F32)], axis=0)
    cols = stacked.T

    lane = lax.broadcasted_iota(jnp.int32, (L, 128), 1)
    ones_blk = jnp.where(lane == 0, 1.0, 0.0).astype(BF16)

    for hd in range(HEADS):
        sl = slice(hd * dh, (hd + 1) * dh)
        i_r = gates_t[hd:hd + 1, :]
        b_r = b_rows[HEADS + hd:HEADS + hd + 1, :]
        i_c = cols[:, hd:hd + 1]
        b_c = cols[:, 8 + HEADS + hd:8 + HEADS + hd + 1]
        m_prev = m_ref[hd:hd + 1, 0:1]

        log_d = jnp.where(causal, b_c + (i_r - b_r), -jnp.inf)
        m_inter = b_c + m_prev
        m_t = jnp.maximum(m_inter, jnp.max(log_d, axis=-1, keepdims=True))
        dmat = jnp.exp(log_d - m_t)
        qb = qkv_ref[:, sl]
        kb = qkv_ref[:, d + hd * dh:d + (hd + 1) * dh]
        s = _dot_nt(qb, kb)
        p = (s * dmat).astype(BF16)
        v_aug = jnp.concatenate([qkv_ref[:, 2 * d + hd * dh:2 * d + (hd + 1) * dh], ones_blk],
                                axis=-1)
        w_inter = jnp.exp(m_inter - m_t)
        inter = _dot(qb, ct_ref[hd].astype(BF16))
        intra = _dot(p, v_aug)
        nd = w_inter * inter + intra
        num = nd[:, 0:dh]
        den = nd[:, dh:dh + 1]
        hh = num / jnp.maximum(jnp.abs(den), jnp.exp(-m_t))

        b_tot = b_c[L - 1:L, :]
        log_w = b_tot - b_c + i_c
        m_next = jnp.maximum(b_tot + m_prev, jnp.max(log_w, axis=0, keepdims=True))
        decay = jnp.exp(b_tot + m_prev - m_next)
        w_c = jnp.exp(log_w - m_next)
        upd = _dot(kb.T, (w_c * v_aug.astype(F32)).astype(BF16))
        ct_ref[hd] = decay * ct_ref[hd] + upd
        m_ref[hd:hd + 1, :] = jnp.broadcast_to(m_next, (1, 128))

        hg = hh * _sigmoid(proj_ref[:, d + hd * dh:d + (hd + 1) * dh])
        mu = jnp.mean(hg, axis=-1, keepdims=True)
        dev = hg - mu
        var = jnp.mean(dev * dev, axis=-1, keepdims=True)
        hn = dev * lax.rsqrt(var + EPS) * ng_ref[:, sl] + skip_ref[:, sl] * xc_ref[:, sl]
        gate = proj_ref[:, 2 * d + hd * dh:2 * d + (hd + 1) * dh]
        y = hn * (gate * _sigmoid(gate))
        o_ref[:, :, sl] = y.reshape(SUB, SUBS_PER_SEG, dh).astype(o_ref.dtype)


def _mlstm(xp, mod_all, gain_all, w3, cw, cb, wq, wk, wv, wgt, bgt, ng, skip, layer):
    bsz = xp.shape[0]
    d = D_MODEL
    L = SEG_LEN
    lay2 = lambda shape: pl.BlockSpec((None,) + shape, lambda b, c: (layer, 0, 0))
    lay3 = lambda shape: pl.BlockSpec((None,) + shape, lambda b, c: (layer, 0, 0, 0))
    seg_spec = pl.BlockSpec((None, SUB, None, SUBS_PER_SEG, d), lambda b, c: (b, 0, c, 0, 0))
    halo = (CONV_WIDTH - 1) * SUBS_PER_SEG
    return pl.pallas_call(
        _mlstm_kernel,
        out_shape=jax.ShapeDtypeStruct(xp.shape, BF16),
        grid=(bsz, N_SEG),
        in_specs=[
            seg_spec,
            pl.BlockSpec((None, None, 1, 3 * d), lambda b, c: (layer, b, 0, 0)),
            lay2((1, d)),
            lay2((d, 3 * d)),
            lay2((CONV_WIDTH, d)),
            lay2((1, d)),
            lay3((HEADS, HEAD_DIM, HEAD_DIM)),
            lay3((HEADS, HEAD_DIM, HEAD_DIM)),
            lay3((HEADS, HEAD_DIM, HEAD_DIM)),
            lay2((8, 3 * d)),
            lay2((8, 1)),
            lay2((1, d)),
            lay2((1, d)),
        ],
        out_specs=seg_spec,
        scratch_shapes=[
            pltpu.VMEM((HEADS, HEAD_DIM, M_AUG), F32),
            pltpu.VMEM((8, 128), F32),
            pltpu.VMEM((8, d), F32),
            pltpu.VMEM((halo + L, d), F32),
            pltpu.VMEM((L, 3 * d), BF16),
            pltpu.VMEM((L, 3 * d), F32),
            pltpu.VMEM((L, d), F32),
        ],
        compiler_params=pltpu.CompilerParams(
            dimension_semantics=("parallel", "arbitrary"),
            vmem_limit_bytes=VMEM_LIMIT),
        name="mlstm",
    )(xp, mod_all, gain_all, w3, cw, cb, wq, wk, wv, wgt, bgt, ng, skip)


M_CHAINS = 2


def _mlstm2_kernel(x_ref, mod_ref, gain_ref, w_min_ref, w_mo_ref, w_mg_ref, cw_ref, cb_ref,
                   wq_ref, wk_ref, wv_ref, wg_ref, bg_ref, ng_ref, skip_ref, o_ref,
                   ct_ref, m_ref, tail_ref, ext_ref, qkv_ref, proj_ref, xc_ref, hn_ref, cvb_ref):
    L, dh, d = SEG_LEN, HEAD_DIM, D_MODEL
    halo = (CONV_WIDTH - 1) * SUBS_PER_SEG

    @pl.when(pl.program_id(1) == 0)
    def _():
        ct_ref[...] = jnp.zeros_like(ct_ref)
        m_ref[...] = jnp.zeros_like(m_ref)
        tail_ref[...] = jnp.zeros_like(tail_ref)

    t_row = _chunk_time(lax.broadcasted_iota(jnp.int32, (L, L), 0))
    t_col = _chunk_time(lax.broadcasted_iota(jnp.int32, (L, L), 1))
    causal = t_col <= t_row
    tri_l = jnp.where(causal, 1.0, 0.0).astype(BF16)
    lane = lax.broadcasted_iota(jnp.int32, (L, 128), 1)
    ones_blk = jnp.where(lane == 0, 1.0, 0.0).astype(BF16)
    row16 = lax.broadcasted_iota(jnp.int32, (SUBS_PER_SEG, d), 0)

    def stage_norm_in(c):
        x = x_ref[c].reshape(L, d)
        hn_ref[c] = _modulated_norm(x, gain_ref[...], mod_ref[c]).astype(BF16)
        proj_ref[c, :, 0:d] = _dot(hn_ref[c], w_min_ref[...])

    def stage_gate_proj(c):
        proj_ref[c, :, d:2 * d] = _dot(hn_ref[c], w_mo_ref[...])
        proj_ref[c, :, 2 * d:3 * d] = _dot(hn_ref[c], w_mg_ref[...])

    def stage_conv(c):
        m_in = proj_ref[c, :, 0:d]
        ext_ref[c, halo:halo + L, :] = m_in
        cvb_ref[c, :, d:2 * d] = m_in.astype(BF16)
        for i in range(CONV_WIDTH - 1):
            lsrc = SUB - (CONV_WIDTH - 1) + i
            r0 = halo + lsrc * SUBS_PER_SEG
            shifted = ext_ref[c, pl.ds(r0 - 1, SUBS_PER_SEG), :]
            ext_ref[c, i * SUBS_PER_SEG:(i + 1) * SUBS_PER_SEG, :] = jnp.where(
                row16 == 0, tail_ref[c, i:i + 1, :], shifted)
            tail_ref[c, i:i + 1, :] = ext_ref[c, r0 + SUBS_PER_SEG - 1:r0 + SUBS_PER_SEG, :]
        acc = m_in * cw_ref[CONV_WIDTH - 1:CONV_WIDTH, :] + cb_ref[...]
        for j in range(CONV_WIDTH - 1):
            back = CONV_WIDTH - 1 - j
            acc = acc + ext_ref[c, pl.ds(halo - back * SUBS_PER_SEG, L), :] * cw_ref[j:j + 1, :]
        xc = acc * _sigmoid(acc)
        xc_ref[c] = xc
        cvb_ref[c, :, 0:d] = xc.astype(BF16)

    def stage_qkv(c):
        for hd in range(HEADS):
            sl = slice(hd * dh, (hd + 1) * dh)
            xcb = cvb_ref[c, :, hd * dh:(hd + 1) * dh]
            qkv_ref[c, :, hd * dh:(hd + 1) * dh] = _dot(xcb, wq_ref[hd]).astype(BF16)
            qkv_ref[c, :, d + hd * dh:d + (hd + 1) * dh] = (
                _dot(xcb, wk_ref[hd]) * (dh ** -0.5)).astype(BF16)
            qkv_ref[c, :, 2 * d + hd * dh:2 * d + (hd + 1) * dh] = (
                _dot(cvb_ref[c, :, d + hd * dh:d + (hd + 1) * dh], wv_ref[hd]).astype(BF16))
        gates = _dot(qkv_ref[c], wg_ref[...]) + bg_ref[...]
        log_f = _log_sigmoid(gates)
        lf_hi, lf_lo = _split_hi_lo(log_f)
        b_cols = _dot(tri_l, lf_hi) + _dot(tri_l, lf_lo)
        return gates.T, b_cols.T, gates, b_cols

    def stage_head(c, hd, gate_forms):
        gates_t, b_rows, gates, b_cols = gate_forms
        sl = slice(hd * dh, (hd + 1) * dh)
        i_r = gates_t[hd:hd + 1, :]
        b_r = b_rows[HEADS + hd:HEADS + hd + 1, :]
        i_c = gates[:, hd:hd + 1]
        b_c = b_cols[:, HEADS + hd:HEADS + hd + 1]
        m_prev = m_ref[c, hd:hd + 1, 0:1]

        log_d = jnp.where(causal, b_c + (i_r - b_r), -jnp.inf)
        m_inter = b_c + m_prev
        m_t = jnp.maximum(m_inter, jnp.max(log_d, axis=-1, keepdims=True))
        dmat = jnp.exp(log_d - m_t)
        qb = qkv_ref[c, :, sl]
        kb = qkv_ref[c, :, d + hd * dh:d + (hd + 1) * dh]
        s = _dot_nt(qb, kb)
        p = (s * dmat).astype(BF16)
        v_aug = jnp.concatenate([qkv_ref[c, :, 2 * d + hd * dh:2 * d + (hd + 1) * dh], ones_blk],
                                axis=-1)
        w_inter = jnp.exp(m_inter - m_t)
        inter = _dot(qb, ct_ref[c, hd].astype(BF16))
        intra = _dot(p, v_aug)
        nd = w_inter * inter + intra
        num = nd[:, 0:dh]
        den = nd[:, dh:dh + 1]
        hh = num / jnp.maximum(jnp.abs(den), jnp.exp(-m_t))

        b_tot = b_c[L - 1:L, :]
        log_w = b_tot - b_c + i_c
        m_next = jnp.maximum(b_tot + m_prev, jnp.max(log_w, axis=0, keepdims=True))
        decay = jnp.exp(b_tot + m_prev - m_next)
        w_c = jnp.exp(log_w - m_next)
        upd = _dot(kb.T, (w_c * v_aug.astype(F32)).astype(BF16))
        ct_ref[c, hd] = decay * ct_ref[c, hd] + upd
        m_ref[c, hd:hd + 1, :] = jnp.broadcast_to(m_next, (1, 128))

        hg = hh * _sigmoid(proj_ref[c, :, d + hd * dh:d + (hd + 1) * dh])
        mu = jnp.mean(hg, axis=-1, keepdims=True)
        dev = hg - mu
        var = jnp.mean(dev * dev, axis=-1, keepdims=True)
        hn = dev * lax.rsqrt(var + EPS) * ng_ref[:, sl] + skip_ref[:, sl] * xc_ref[c, :, sl]
        gate = proj_ref[c, :, 2 * d + hd * dh:2 * d + (hd + 1) * dh]
        y = hn * (gate * _sigmoid(gate))
        o_ref[c, :, :, sl] = y.reshape(SUB, SUBS_PER_SEG, dh).astype(o_ref.dtype)

    a, b = 0, 1
    stage_norm_in(a)
    stage_conv(a)
    stage_norm_in(b)
    stage_gate_proj(a)
    gf_a = stage_qkv(a)
    stage_conv(b)
    stage_gate_proj(b)
    stage_head(a, 0, gf_a)
    gf_b = stage_qkv(b)
    stage_head(a, 1, gf_a)
    stage_head(b, 0, gf_b)
    stage_head(a, 2, gf_a)
    stage_head(b, 1, gf_b)
    stage_head(a, 3, gf_a)
    stage_head(b, 2, gf_b)
    stage_head(b, 3, gf_b)


def _mlstm2(xp, mod_all, gain_all, w_in_b, cw, cb, wq, wk, wv, wg, bg, ng, skip, layer):
    bsz = xp.shape[0]
    d = D_MODEL
    L = SEG_LEN
    nc = M_CHAINS
    lay2 = lambda shape: pl.BlockSpec((None,) + shape, lambda b, c: (layer, 0, 0))
    lay3 = lambda shape: pl.BlockSpec((None,) + shape, lambda b, c: (layer, 0, 0, 0))
    seg_spec = pl.BlockSpec((nc, SUB, None, SUBS_PER_SEG, d), lambda b, c: (b, 0, c, 0, 0))
    w_cols = lambda j: pl.BlockSpec((None, d, d), lambda b, c: (layer, 0, j))
    halo = (CONV_WIDTH - 1) * SUBS_PER_SEG
    return pl.pallas_call(
        _mlstm2_kernel,
        out_shape=jax.ShapeDtypeStruct(xp.shape, BF16),
        grid=(bsz // nc, N_SEG),
        in_specs=[
            seg_spec,
            pl.BlockSpec((None, nc, 1, 3 * d), lambda b, c: (layer, b, 0, 0)),
            lay2((1, d)),
            w_cols(2), w_cols(3), w_cols(4),
            lay2((CONV_WIDTH, d)),
            lay2((1, d)),
            lay3((HEADS, HEAD_DIM, HEAD_DIM)),
            lay3((HEADS, HEAD_DIM, HEAD_DIM)),
            lay3((HEADS, HEAD_DIM, HEAD_DIM)),
            lay2((3 * d, 128)),
            lay2((1, 128)),
            lay2((1, d)),
            lay2((1, d)),
        ],
        out_specs=seg_spec,
        scratch_shapes=[
            pltpu.VMEM((nc, HEADS, HEAD_DIM, M_AUG), F32),
            pltpu.VMEM((nc, 8, 128), F32),
            pltpu.VMEM((nc, 8, d), F32),
            pltpu.VMEM((nc, halo + L, d), F32),
            pltpu.VMEM((nc, L, 3 * d), BF16),
            pltpu.VMEM((nc, L, 3 * d), F32),
            pltpu.VMEM((nc, L, d), F32),
            pltpu.VMEM((nc, L, d), BF16),
            pltpu.VMEM((nc, L, 2 * d), BF16),
        ],
        compiler_params=pltpu.CompilerParams(
            dimension_semantics=("parallel", "arbitrary"),
            vmem_limit_bytes=VMEM_LIMIT),
        name="mlstm",
    )(xp, mod_all, gain_all, w_in_b, w_in_b, w_in_b, cw, cb, wq, wk, wv, wg, bg, ng, skip)


def _gelu_tanh(x):
    inner = math.sqrt(2.0 / math.pi) * (x + 0.044715 * (x * x * x))
    return x * (0.5 * (1.0 + jnp.tanh(inner)))


def _out_kernel(yt_ref, gt_ref, my_ref, x_ref, mod_ref, bglu_ref, og_ref, fg_ref,
                wglut_ref, wout_ref, qt_ref, o_ref, *, final):
    d = D_MODEL
    rows = L_PAIR * BLK_LANES

    def stage_in(s):
        ls = range(s * L_PAIR, (s + 1) * L_PAIR)
        out_m = _dot(my_ref[s * L_PAIR:(s + 1) * L_PAIR].reshape(rows, d),
                     wout_ref[d:2 * d, :])
        yt = jnp.concatenate(
            [yt_ref[:, l * SSM_GROUP:(l + 1) * SSM_GROUP, :].reshape(d, BLK_LANES) for l in ls],
            axis=-1)
        y = _gelu_tanh(_dot(yt, qt_ref[...]))
        return out_m, y

    def stage_glu(s, y):
        ls = range(s * L_PAIR, (s + 1) * L_PAIR)
        glu = y * _sigmoid(_dot(wglut_ref[...], y.astype(BF16)) + bglu_ref[...])
        gate_s = jnp.concatenate([gt_ref[l] for l in ls], axis=-1).astype(F32)
        ms = jnp.mean(glu * glu, axis=0, keepdims=True)
        return (glu * lax.rsqrt(ms + EPS) * og_ref[...] * gate_s).astype(BF16)

    def stage_out(s, out_m, ssm_y):
        out = _dot_tn(ssm_y, wout_ref[0:d, :]) + out_m
        xn = x_ref[s * L_PAIR:(s + 1) * L_PAIR].reshape(rows, d) + mod_ref[:, 2 * d:3 * d] * out
        if final:
            return _rms(xn) * fg_ref[...]
        o_ref[s * L_PAIR:(s + 1) * L_PAIR] = xn.reshape(L_PAIR, N_SEG, SUBS_PER_SEG, d)

    streams = range(OUT_STREAMS)
    ins = [stage_in(s) for s in streams]
    ssm = [stage_glu(s, ins[s][1]) for s in streams]
    outs = [stage_out(s, ins[s][0], ssm[s]) for s in streams]
    if final:
        n_l = OUT_STREAMS * L_PAIR
        by_l = jnp.concatenate(outs, axis=0).reshape(n_l, BLK_LANES, d)
        o_ref[...] = jnp.swapaxes(by_l, 0, 1).reshape(N_SEG, SUBS_PER_SEG, n_l, d)


def _out_stage(yt, gt, my, xp, mod_all, bglu_col, og_col, fg, wglut, wout, qperm_t, layer, final):
    bsz = xp.shape[0]
    d = D_MODEL
    n_l = OUT_STREAMS * L_PAIR
    tok = pl.BlockSpec((None, n_l, N_SEG, SUBS_PER_SEG, d), lambda b, p: (b, p, 0, 0, 0))
    col = pl.BlockSpec((None, d, 1), lambda b, p: (layer, 0, 0))
    if final:
        assert n_l == 8
        out_shape = jax.ShapeDtypeStruct((bsz, N_SEG, SUBS_PER_SEG, SUB, d), F32)
        out_spec = pl.BlockSpec((None, N_SEG, SUBS_PER_SEG, n_l, d), lambda b, p: (b, 0, 0, p, 0))
    else:
        out_shape = jax.ShapeDtypeStruct(xp.shape, F32)
        out_spec = tok
    return pl.pallas_call(
        functools.partial(_out_kernel, final=final),
        out_shape=out_shape,
        grid=(bsz, SUB // n_l),
        in_specs=[
            pl.BlockSpec((SSM_GROUPS, n_l * SSM_GROUP, BLK_LANES), lambda b, p: (0, p, b)),
            pl.BlockSpec((None, n_l, d, BLK_LANES), lambda b, p: (b, p, 0, 0)),
            tok, tok,
            pl.BlockSpec((None, None, 1, 3 * d), lambda b, p: (layer, b, 0, 0)),
            col, col,
            pl.BlockSpec((1, d), lambda b, p: (0, 0)),
            pl.BlockSpec((None, d, d), lambda b, p: (layer, 0, 0)),
            pl.BlockSpec((None, 2 * d, d), lambda b, p: (layer, 0, 0)),
            pl.BlockSpec((L_PAIR * BLK_LANES, L_PAIR * BLK_LANES), lambda b, p: (0, 0)),
        ],
        out_specs=out_spec,
        compiler_params=pltpu.CompilerParams(
            dimension_semantics=("parallel", "parallel"),
            vmem_limit_bytes=VMEM_LIMIT),
        name="out_stage",
    )(yt, gt, my, xp, mod_all, bglu_col, og_col, fg, wglut, wout, qperm_t)


def _lane_permutation():
    q = np.zeros((BLK_LANES, BLK_LANES), np.float32)
    for seg in range(N_SEG):
        for kl in range(SUBS_PER_SEG):
            q[seg * SUBS_PER_SEG + kl, kl * N_SEG + seg] = 1.0
    return np.kron(np.eye(L_PAIR, dtype=np.float32), q)


def kernel(x, c, norm_gain, w_mod, b_mod, w_in, ssm_lambda_re, ssm_lambda_im, ssm_log_dt,
           ssm_b_re, ssm_b_im, ssm_c_re, ssm_c_im, ssm_d, ssm_w_glu, ssm_b_glu,
           ssm_out_gain, m_conv_w, m_conv_b, m_wq, m_wk, m_wv, m_w_gates, m_b_igate,
           m_b_fgate, m_norm_gain, m_skip, w_out, final_gain):
    bsz, seq, d = x.shape
    depth = w_in.shape[0]
    assert d == D_MODEL and seq == SEQ

    xp = x.reshape(bsz, N_SEG, SUBS_PER_SEG, SUB, d).transpose(0, 3, 1, 2, 4)
    mod_all = _modulation(c, w_mod, b_mod).reshape(depth, bsz, 1, 3 * d)
    rows = lambda v: v.reshape(depth, 1, -1)
    cols = lambda v: v.reshape(depth, -1, 1)
    qperm = jnp.asarray(_lane_permutation(), BF16)
    qperm_t = jnp.asarray(_lane_permutation().T, BF16)

    gain_all = rows(norm_gain)
    w_in_b = w_in.astype(BF16)
    w_ssm_t = w_in_b[:, :, 0:2 * d].transpose(0, 2, 1)
    toept, ws, wot, atab, segtab = _s5_prep(ssm_lambda_re, ssm_lambda_im, ssm_log_dt,
                                            ssm_b_re, ssm_b_im, ssm_c_re, ssm_c_im)
    dcol = jnp.broadcast_to(ssm_d.reshape(depth * SSM_GROUPS, 1, SSM_GROUP),
                            (depth * SSM_GROUPS, SUB, SSM_GROUP)).reshape(-1, S5_ROW, 1)
    wq, wk, wv = m_wq.astype(BF16), m_wk.astype(BF16), m_wv.astype(BF16)
    gate_pad = ((0, 0), (0, 0), (0, 128 - 2 * HEADS))
    wg = jnp.pad(m_w_gates, gate_pad).astype(BF16)
    bg = jnp.pad(jnp.concatenate([m_b_igate, m_b_fgate], axis=-1).reshape(depth, 1, 2 * HEADS),
                 gate_pad)
    wglut = ssm_w_glu.astype(BF16).transpose(0, 2, 1)
    wout = w_out.astype(BF16)
    fg = final_gain.reshape(1, d)

    for l in range(depth):
        zt, gt = _s5_in(xp, mod_all, gain_all, w_ssm_t, qperm, l)
        yt = _s5_core(zt, toept, ws, wot, atab, segtab, dcol, bsz, l)
        my = _mlstm2(xp, mod_all, gain_all, w_in_b, m_conv_w, rows(m_conv_b), wq, wk, wv, wg, bg,
                    rows(m_norm_gain), rows(m_skip), l)
        xp = _out_stage(yt, gt, my, xp, mod_all, cols(ssm_b_glu), cols(ssm_out_gain), fg,
                        wglut, wout, qperm_t, l, final=(l == depth - 1))

    return xp.reshape(bsz, seq, d)
```

```python
import functools
import math

import numpy as np
import jax
import jax.numpy as jnp
from jax import lax
from jax.experimental import pallas as pl
from jax.experimental.pallas import tpu as pltpu

F32 = jnp.float32
BF16 = jnp.bfloat16

D_MODEL = 1024
SSM_GROUP = 16
SSM_GROUPS = D_MODEL // SSM_GROUP
SSM_STATE = 64
HEADS = 4
HEAD_DIM = D_MODEL // HEADS
CONV_WIDTH = 4
EPS = 1e-6

SUB = 16
SUBS_PER_SEG = 16
SEG_LEN = SUB * SUBS_PER_SEG
N_SEG = 8
SEQ = N_SEG * SEG_LEN
BLK_LANES = N_SEG * SUBS_PER_SEG
S5_ROW = SUB * SSM_GROUP
STATE_HALF = 128
STATE_LANES = 2 * STATE_HALF
S5_GROUPS_PER_STEP = 4
L_PAIR = 2
OUT_STREAMS = 4
M_AUG = HEAD_DIM + 128
VMEM_LIMIT = 56 * 1024 * 1024

_NT = (((1,), (1,)), ((), ()))
_TN = (((0,), (0,)), ((), ()))


def _dot(a, b):
    return jnp.dot(a, b, preferred_element_type=F32)


def _dot_nt(a, b):
    return lax.dot_general(a, b, _NT, preferred_element_type=F32)


def _dot_tn(a, b):
    return lax.dot_general(a, b, _TN, preferred_element_type=F32)


def _sigmoid(x):
    return jax.nn.sigmoid(x)


def _rms(x):
    return x * lax.rsqrt(jnp.mean(x * x, axis=-1, keepdims=True) + EPS)


def _modulated_norm(x, gain, mod):
    shift = mod[:, 0:D_MODEL]
    scale = mod[:, D_MODEL:2 * D_MODEL]
    return _rms(x) * gain * (1.0 + scale) + shift


def _split_hi_lo(v):
    hi = v.astype(BF16)
    lo = (v - hi.astype(F32)).astype(BF16)
    return hi, lo


def _dot_nt_f32(a, b):
    a_hi, a_lo = _split_hi_lo(a)
    b_hi, b_lo = _split_hi_lo(b)
    return _dot_nt(a_hi, b_hi) + _dot_nt(a_hi, b_lo) + _dot_nt(a_lo, b_hi)


def _cmul(a_re, a_im, b_re, b_im):
    return a_re * b_re - a_im * b_im, a_re * b_im + a_im * b_re


def _cmul_add(p_re, p_im, s_re, s_im, add_re, add_im):
    return (p_re * s_re - p_im * s_im + add_re,
            p_re * s_im + p_im * s_re + add_im)


def _mod_kernel(c_ref, w_ref, b_ref, o_ref):
    cv = c_ref[...]
    act = cv * _sigmoid(cv)
    o_ref[...] = _dot(act.astype(BF16), w_ref[...].astype(BF16)) + b_ref[...]


def _modulation(c, w_mod, b_mod):
    depth, d, n = w_mod.shape
    bsz = c.shape[0]
    return pl.pallas_call(
        _mod_kernel,
        out_shape=jax.ShapeDtypeStruct((depth, bsz, n), F32),
        grid=(depth, n // d),
        in_specs=[
            pl.BlockSpec((bsz, d), lambda l, j: (0, 0)),
            pl.BlockSpec((None, d, d), lambda l, j: (l, 0, j)),
            pl.BlockSpec((None, 1, d), lambda l, j: (l, 0, j)),
        ],
        out_specs=pl.BlockSpec((None, bsz, d), lambda l, j: (l, 0, j)),
        compiler_params=pltpu.CompilerParams(
            dimension_semantics=("parallel", "parallel"),
            vmem_limit_bytes=VMEM_LIMIT),
        name="adaln_mod",
    )(c, w_mod, b_mod.reshape(depth, 1, n))


def _s5_prep_kernel(lre_ref, lim_ref, ldt_ref, btre_ref, btim_ref, cre_ref, cim_ref,
                    toept_ref, ws_ref, wot_ref, atab_ref, seg_ref):
    gp = S5_GROUPS_PER_STEP
    lane_blk = lax.broadcasted_iota(jnp.int32, (S5_ROW, S5_ROW), 1) // SSM_GROUP
    row8 = lax.broadcasted_iota(jnp.int32, (8, STATE_HALF), 0)
    for j in range(gp):
        lr = lre_ref[j]
        li = lim_ref[j]
        dt = jnp.exp(ldt_ref[j])
        mag = jnp.exp(lr * dt)
        a_re = mag * jnp.cos(li * dt)
        a_im = mag * jnp.sin(li * dt)
        inv = 1.0 / (lr * lr + li * li)
        k_re = ((a_re - 1.0) * lr + a_im * li) * inv
        k_im = (a_im * lr - (a_re - 1.0) * li) * inv
        bb_re, bb_im = _cmul(k_re, k_im, btre_ref[j], btim_ref[j])
        c_re = cre_ref[j]
        c_im = cim_ref[j]

        pw = [(jnp.ones_like(a_re), jnp.zeros_like(a_re))]
        for _ in range(SUB):
            pw.append(_cmul(pw[-1][0], pw[-1][1], a_re, a_im))

        ws_rows, wot_rows, ca_re, ca_im = [], [], [], []
        for l in range(SUB):
            w_re, w_im = _cmul(pw[SUB - 1 - l][0], pw[SUB - 1 - l][1], bb_re, bb_im)
            ws_rows.append(jnp.concatenate([w_re, w_im], axis=-1))
            o_re, o_im = _cmul(c_re, c_im, pw[l + 1][0], pw[l + 1][1])
            wot_rows.append(jnp.concatenate([o_re, -o_im], axis=-1))
            g_re, g_im = _cmul(c_re, c_im, pw[l][0], pw[l][1])
            ca_re.append(g_re)
            ca_im.append(g_im)
        ws_ref[j] = jnp.concatenate(ws_rows, axis=0).astype(BF16)
        wot_ref[j] = jnp.concatenate(wot_rows, axis=0).astype(BF16)

        bbt_re = jnp.concatenate([bb_re] * SUB, axis=0)
        bbt_im = jnp.concatenate([bb_im] * SUB, axis=0)
        kw = (_dot_nt_f32(jnp.concatenate(ca_re, axis=0), bbt_re)
              - _dot_nt_f32(jnp.concatenate(ca_im, axis=0), bbt_im))
        toep = jnp.where(lane_blk == 0, kw, 0.0)
        for lp in range(1, SUB):
            shifted = jnp.concatenate(
                [jnp.zeros((SSM_GROUP * lp, S5_ROW), F32), kw[0:S5_ROW - SSM_GROUP * lp]], axis=0)
            toep = jnp.where(lane_blk == lp, shifted, toep)
        toept_ref[j] = toep.astype(BF16)

        s_re, s_im = pw[SUB]
        atab_ref[j] = jnp.concatenate([jnp.broadcast_to(s_re, (8, STATE_HALF)),
                                       jnp.broadcast_to(s_im, (8, STATE_HALF))], axis=0)
        for _ in range(4):
            s_re, s_im = _cmul(s_re, s_im, s_re, s_im)
        seg = jnp.zeros((8, STATE_HALF), F32)
        for i in range(3):
            seg = jnp.where(row8 == i, s_re, jnp.where(row8 == 3 + i, s_im, seg))
            s_re, s_im = _cmul(s_re, s_im, s_re, s_im)
        seg_ref[j] = seg


def _s5_prep(lam_re, lam_im, log_dt, b_re, b_im, c_re, c_im):
    depth, g, p = lam_re.shape
    cg = SSM_GROUP
    n = depth * g
    gp = S5_GROUPS_PER_STEP
    lane_pad = ((0, 0), (0, 0), (0, STATE_HALF - p))

    def state_rows(v, fill):
        v = v.reshape(n, -1, p)
        return jnp.pad(v, lane_pad, constant_values=fill)

    args = (state_rows(lam_re, -1.0), state_rows(lam_im, 0.0), log_dt.reshape(n, 1, 1),
            state_rows(b_re.transpose(0, 1, 3, 2), 0.0), state_rows(b_im.transpose(0, 1, 3, 2), 0.0),
            state_rows(c_re, 0.0), state_rows(c_im, 0.0))
    vec = pl.BlockSpec((gp, 1, STATE_HALF), lambda i: (i, 0, 0))
    mat = pl.BlockSpec((gp, cg, STATE_HALF), lambda i: (i, 0, 0))
    big = pl.BlockSpec((gp, S5_ROW, STATE_LANES), lambda i: (i, 0, 0))
    return pl.pallas_call(
        _s5_prep_kernel,
        out_shape=(jax.ShapeDtypeStruct((n, S5_ROW, S5_ROW), BF16),
                   jax.ShapeDtypeStruct((n, S5_ROW, STATE_LANES), BF16),
                   jax.ShapeDtypeStruct((n, S5_ROW, STATE_LANES), BF16),
                   jax.ShapeDtypeStruct((n, 16, STATE_HALF), F32),
                   jax.ShapeDtypeStruct((n, 8, STATE_HALF), F32)),
        grid=(n // gp,),
        in_specs=[vec, vec, pl.BlockSpec((gp, 1, 1), lambda i: (i, 0, 0)), mat, mat, mat, mat],
        out_specs=(big, big, big,
                   pl.BlockSpec((gp, 16, STATE_HALF), lambda i: (i, 0, 0)),
                   pl.BlockSpec((gp, 8, STATE_HALF), lambda i: (i, 0, 0))),
        compiler_params=pltpu.CompilerParams(
            dimension_semantics=("parallel",),
            vmem_limit_bytes=VMEM_LIMIT),
        name="s5_prep",
    )(*args)


def _stream_rows(x_ref, natural):
    n_l = OUT_STREAMS * L_PAIR
    rows = L_PAIR * BLK_LANES
    if natural:
        by_l = jnp.swapaxes(x_ref[...].reshape(BLK_LANES, n_l, D_MODEL), 0, 1)
        return lambda s: by_l[s * L_PAIR:(s + 1) * L_PAIR].reshape(rows, D_MODEL)
    return lambda s: x_ref[s * L_PAIR:(s + 1) * L_PAIR].reshape(rows, D_MODEL)


def _s5_in_kernel(x_ref, mod_ref, gain_ref, wut_ref, wgt_ref, q_ref, zt_ref, gt_ref, *,
                  natural):
    d = D_MODEL
    x_rows = _stream_rows(x_ref, natural)

    def stage_norm(s):
        return _modulated_norm(x_rows(s), gain_ref[...], mod_ref[...]).astype(BF16)

    def stage_proj(s, h):
        ut = _dot_nt(wut_ref[...], h).astype(BF16)
        sgt = _dot_nt(wgt_ref[...], h)
        p = _dot(ut, q_ref[...]).astype(BF16)
        for l in range(L_PAIR):
            lanes = slice(l * BLK_LANES, (l + 1) * BLK_LANES)
            row0 = (s * L_PAIR + l) * SSM_GROUP
            zt_ref[:, row0:row0 + SSM_GROUP, :] = (
                p[:, lanes].reshape(SSM_GROUPS, SSM_GROUP, BLK_LANES))
            g = sgt[:, lanes]
            gt_ref[s * L_PAIR + l] = (g * _sigmoid(g)).astype(BF16)

    hs = [stage_norm(s) for s in range(OUT_STREAMS)]
    for s in range(OUT_STREAMS):
        stage_proj(s, hs[s])


def _token_spec(natural):
    n_l = OUT_STREAMS * L_PAIR
    if natural:
        assert n_l == 8
        return pl.BlockSpec((None, N_SEG, SUBS_PER_SEG, n_l, D_MODEL), lambda b, p: (b, 0, 0, p, 0))
    return pl.BlockSpec((None, n_l, N_SEG, SUBS_PER_SEG, D_MODEL), lambda b, p: (b, p, 0, 0, 0))


def _s5_in(x, mod_all, gain_all, w_ssm_t, qperm, layer, natural):
    bsz = x.shape[0]
    d = D_MODEL
    n_l = OUT_STREAMS * L_PAIR
    return pl.pallas_call(
        functools.partial(_s5_in_kernel, natural=natural),
        out_shape=(jax.ShapeDtypeStruct((SSM_GROUPS, S5_ROW, bsz * BLK_LANES), BF16),
                   jax.ShapeDtypeStruct((bsz, SUB, d, BLK_LANES), BF16)),
        grid=(bsz, SUB // n_l),
        in_specs=[
            _token_spec(natural),
            pl.BlockSpec((None, None, 1, 3 * d), lambda b, p: (layer, b, 0, 0)),
            pl.BlockSpec((None, 1, d), lambda b, p: (layer, 0, 0)),
            pl.BlockSpec((None, d, d), lambda b, p: (layer, 0, 0)),
            pl.BlockSpec((None, d, d), lambda b, p: (layer, 1, 0)),
            pl.BlockSpec((L_PAIR * BLK_LANES, L_PAIR * BLK_LANES), lambda b, p: (0, 0)),
        ],
        out_specs=(
            pl.BlockSpec((SSM_GROUPS, n_l * SSM_GROUP, BLK_LANES), lambda b, p: (0, p, b)),
            pl.BlockSpec((None, n_l, d, BLK_LANES), lambda b, p: (b, p, 0, 0)),
        ),
        compiler_params=pltpu.CompilerParams(
            dimension_semantics=("parallel", "parallel"),
            vmem_limit_bytes=VMEM_LIMIT),
        name="s5_in",
    )(x, mod_all, gain_all, w_ssm_t, w_ssm_t, qperm)


def _s5_core_kernel(zt_ref, toept_ref, ws_ref, wot_ref, atab_ref, seg_ref, dcol_ref,
                    yt_ref, loc_ref, sprev_ref, *, bsz):
    gs = S5_GROUPS_PER_STEP
    half = STATE_HALF
    row8 = lax.broadcasted_iota(jnp.int32, (N_SEG, half), 0)

    def shift_down(v, n):
        return jnp.where(row8 >= n, pltpu.roll(v, n, axis=0), 0.0)

    for j in range(gs):
        loc_ref[j] = _dot_tn(zt_ref[j], ws_ref[j])
        a_re = atab_ref[j, 0:8, :]
        a_im = atab_ref[j, 8:16, :]
        for b in range(bsz):
            base = b * BLK_LANES

            def rows(kl, base=base):
                return pl.ds(base + kl * N_SEG, N_SEG)

            e_re = jnp.zeros((N_SEG, half), F32)
            e_im = jnp.zeros((N_SEG, half), F32)
            for kl in range(SUBS_PER_SEG):
                e_re, e_im = _cmul_add(a_re, a_im, e_re, e_im,
                                       loc_ref[j, rows(kl), 0:half],
                                       loc_ref[j, rows(kl), half:STATE_LANES])
            for i, n in enumerate((1, 2, 4)):
                p_re = seg_ref[j, i:i + 1, :]
                p_im = seg_ref[j, 3 + i:4 + i, :]
                e_re, e_im = _cmul_add(p_re, p_im, shift_down(e_re, n), shift_down(e_im, n),
                                       e_re, e_im)
            s_re = shift_down(e_re, 1)
            s_im = shift_down(e_im, 1)
            for kl in range(SUBS_PER_SEG):
                sprev_ref[j, rows(kl), 0:half] = s_re
                sprev_ref[j, rows(kl), half:STATE_LANES] = s_im
                s_re, s_im = _cmul_add(a_re, a_im, s_re, s_im,
                                       loc_ref[j, rows(kl), 0:half],
                                       loc_ref[j, rows(kl), half:STATE_LANES])

        zt = zt_ref[j]
        out = (_dot(toept_ref[j], zt)
               + _dot_nt(wot_ref[j], sprev_ref[j].astype(BF16))
               + dcol_ref[j] * zt.astype(F32))
        yt_ref[j] = out.astype(BF16)


def _s5_core(zt, toept, ws, wot, atab, segtab, dcol, bsz, layer):
    g, rows, lanes = zt.shape
    gs = S5_GROUPS_PER_STEP
    off = layer * (g // gs)
    wspec = pl.BlockSpec((gs, S5_ROW, STATE_LANES), lambda i: (off + i, 0, 0))
    return pl.pallas_call(
        functools.partial(_s5_core_kernel, bsz=bsz),
        out_shape=jax.ShapeDtypeStruct((g, rows, lanes), BF16),
        grid=(g // gs,),
        in_specs=[
            pl.BlockSpec((gs, rows, lanes), lambda i: (i, 0, 0)),
            wspec, wspec, wspec,
            pl.BlockSpec((gs, 16, STATE_HALF), lambda i: (off + i, 0, 0)),
            pl.BlockSpec((gs, 8, STATE_HALF), lambda i: (off + i, 0, 0)),
            pl.BlockSpec((gs, S5_ROW, 1), lambda i: (off + i, 0, 0)),
        ],
        out_specs=pl.BlockSpec((gs, rows, lanes), lambda i: (i, 0, 0)),
        scratch_shapes=[pltpu.VMEM((gs, lanes, STATE_LANES), F32),
                        pltpu.VMEM((gs, lanes, STATE_LANES), F32)],
        compiler_params=pltpu.CompilerParams(
            dimension_semantics=("parallel",),
            vmem_limit_bytes=VMEM_LIMIT),
        name="s5_core",
    )(zt, toept, ws, wot, atab, segtab, dcol)


def _log_sigmoid(x):
    return -(jnp.maximum(-x, 0.0) + jnp.log1p(jnp.exp(-jnp.abs(x))))


def _chunk_time(idx):
    return ((idx & (SUBS_PER_SEG - 1)) * SUB) | (idx >> 4)


def _mlstm_kernel(x_ref, mod_ref, gain_ref, w3_ref, cw_ref, cb_ref,
                  wq_ref, wk_ref, wv_ref, wgt_ref, bgt_ref, ng_ref, skip_ref, o_ref,
                  ct_ref, m_ref, tail_ref, ext_ref, qkv_ref, proj_ref, xc_ref):
    L, dh, d = SEG_LEN, HEAD_DIM, D_MODEL
    halo = (CONV_WIDTH - 1) * SUBS_PER_SEG

    @pl.when(pl.program_id(1) == 0)
    def _():
        ct_ref[...] = jnp.zeros_like(ct_ref)
        m_ref[...] = jnp.zeros_like(m_ref)
        tail_ref[...] = jnp.zeros_like(tail_ref)

    x = x_ref[...].reshape(L, d)
    h = _modulated_norm(x, gain_ref[...], mod_ref[...]).astype(BF16)
    proj_ref[...] = _dot(h, w3_ref[...])
    m_in = proj_ref[:, 0:d]

    ext_ref[halo:halo + L, :] = m_in
    row16 = lax.broadcasted_iota(jnp.int32, (SUBS_PER_SEG, d), 0)
    for i in range(CONV_WIDTH - 1):
        lsrc = SUB - (CONV_WIDTH - 1) + i
        r0 = halo + lsrc * SUBS_PER_SEG
        shifted = ext_ref[pl.ds(r0 - 1, SUBS_PER_SEG), :]
        ext_ref[i * SUBS_PER_SEG:(i + 1) * SUBS_PER_SEG, :] = jnp.where(
            row16 == 0, tail_ref[i:i + 1, :], shifted)
        tail_ref[i:i + 1, :] = ext_ref[r0 + SUBS_PER_SEG - 1:r0 + SUBS_PER_SEG, :]
    acc = m_in * cw_ref[CONV_WIDTH - 1:CONV_WIDTH, :] + cb_ref[...]
    for j in range(CONV_WIDTH - 1):
        back = CONV_WIDTH - 1 - j
        acc = acc + ext_ref[pl.ds(halo - back * SUBS_PER_SEG, L), :] * cw_ref[j:j + 1, :]
    xc = acc * _sigmoid(acc)
    xc_ref[...] = xc

    xcb = xc.astype(BF16)
    minb = m_in.astype(BF16)
    for hd in range(HEADS):
        sl = slice(hd * dh, (hd + 1) * dh)
        qkv_ref[:, hd * dh:(hd + 1) * dh] = _dot(xcb[:, sl], wq_ref[hd]).astype(BF16)
        qkv_ref[:, d + hd * dh:d + (hd + 1) * dh] = (
            _dot(xcb[:, sl], wk_ref[hd]) * (dh ** -0.5)).astype(BF16)
        qkv_ref[:, 2 * d + hd * dh:2 * d + (hd + 1) * dh] = (
            _dot(minb[:, sl], wv_ref[hd]).astype(BF16))

    gates_t = _dot_nt(wgt_ref[...], qkv_ref[...]) + bgt_ref[...]

    t_row = _chunk_time(lax.broadcasted_iota(jnp.int32, (L, L), 0))
    t_col = _chunk_time(lax.broadcasted_iota(jnp.int32, (L, L), 1))
    causal = t_col <= t_row
    tri_u = jnp.where(t_row <= t_col, 1.0, 0.0).astype(BF16)

    lft_hi, lft_lo = _split_hi_lo(_log_sigmoid(gates_t))
    b_rows = _dot(lft_hi, tri_u) + _dot(lft_lo, tri_u)
    stacked = jnp.concatenate([gates_t, b_rows, jnp.zeros((128 - 16, L), F32)], axis=0)
    cols = stacked.T

    lane = lax.broadcasted_iota(jnp.int32, (L, 128), 1)
    ones_blk = jnp.where(lane == 0, 1.0, 0.0).astype(BF16)

    for hd in range(HEADS):
        sl = slice(hd * dh, (hd + 1) * dh)
        i_r = gates_t[hd:hd + 1, :]
        b_r = b_rows[HEADS + hd:HEADS + hd + 1, :]
        i_c = cols[:, hd:hd + 1]
        b_c = cols[:, 8 + HEADS + hd:8 + HEADS + hd + 1]
        m_prev = m_ref[hd:hd + 1, 0:1]

        log_d = jnp.where(causal, b_c + (i_r - b_r), -jnp.inf)
        m_inter = b_c + m_prev
        m_t = jnp.maximum(m_inter, jnp.max(log_d, axis=-1, keepdims=True))
        dmat = jnp.exp(log_d - m_t)
        qb = qkv_ref[:, sl]
        kb = qkv_ref[:, d + hd * dh:d + (hd + 1) * dh]
        s = _dot_nt(qb, kb)
        p = (s * dmat).astype(BF16)
        v_aug = jnp.concatenate([qkv_ref[:, 2 * d + hd * dh:2 * d + (hd + 1) * dh], ones_blk],
                                axis=-1)
        w_inter = jnp.exp(m_inter - m_t)
        inter = _dot(qb, ct_ref[hd].astype(BF16))
        intra = _dot(p, v_aug)
        nd = w_inter * inter + intra
        num = nd[:, 0:dh]
        den = nd[:, dh:dh + 1]
        hh = num / jnp.maximum(jnp.abs(den), jnp.exp(-m_t))

        b_tot = b_c[L - 1:L, :]
        log_w = b_tot - b_c + i_c
        m_next = jnp.maximum(b_tot + m_prev, jnp.max(log_w, axis=0, keepdims=True))
        decay = jnp.exp(b_tot + m_prev - m_next)
        w_c = jnp.exp(log_w - m_next)
        upd = _dot(kb.T, (w_c * v_aug.astype(F32)).astype(BF16))
        ct_ref[hd] = decay * ct_ref[hd] + upd
        m_ref[hd:hd + 1, :] = jnp.broadcast_to(m_next, (1, 128))

        hg = hh * _sigmoid(proj_ref[:, d + hd * dh:d + (hd + 1) * dh])
        mu = jnp.mean(hg, axis=-1, keepdims=True)
        dev = hg - mu
        var = jnp.mean(dev * dev, axis=-1, keepdims=True)
        hn = dev * lax.rsqrt(var + EPS) * ng_ref[:, sl] + skip_ref[:, sl] * xc_ref[:, sl]
        gate = proj_ref[:, 2 * d + hd * dh:2 * d + (hd + 1) * dh]
        y = hn * (gate * _sigmoid(gate))
        o_ref[:, :, sl] = y.reshape(SUB, SUBS_PER_SEG, dh).astype(o_ref.dtype)


def _mlstm(xp, mod_all, gain_all, w3, cw, cb, wq, wk, wv, wgt, bgt, ng, skip, layer):
    bsz = xp.shape[0]
    d = D_MODEL
    L = SEG_LEN
    lay2 = lambda shape: pl.BlockSpec((None,) + shape, lambda b, c: (layer, 0, 0))
    lay3 = lambda shape: pl.BlockSpec((None,) + shape, lambda b, c: (layer, 0, 0, 0))
    seg_spec = pl.BlockSpec((None, SUB, None, SUBS_PER_SEG, d), lambda b, c: (b, 0, c, 0, 0))
    halo = (CONV_WIDTH - 1) * SUBS_PER_SEG
    return pl.pallas_call(
        _mlstm_kernel,
        out_shape=jax.ShapeDtypeStruct(xp.shape, BF16),
        grid=(bsz, N_SEG),
        in_specs=[
            seg_spec,
            pl.BlockSpec((None, None, 1, 3 * d), lambda b, c: (layer, b, 0, 0)),
            lay2((1, d)),
            lay2((d, 3 * d)),
            lay2((CONV_WIDTH, d)),
            lay2((1, d)),
            lay3((HEADS, HEAD_DIM, HEAD_DIM)),
            lay3((HEADS, HEAD_DIM, HEAD_DIM)),
            lay3((HEADS, HEAD_DIM, HEAD_DIM)),
            lay2((8, 3 * d)),
            lay2((8, 1)),
            lay2((1, d)),
            lay2((1, d)),
        ],
        out_specs=seg_spec,
        scratch_shapes=[
            pltpu.VMEM((HEADS, HEAD_DIM, M_AUG), F32),
            pltpu.VMEM((8, 128), F32),
            pltpu.VMEM((8, d), F32),
            pltpu.VMEM((halo + L, d), F32),
            pltpu.VMEM((L, 3 * d), BF16),
            pltpu.VMEM((L, 3 * d), F32),
            pltpu.VMEM((L, d), F32),
        ],
        compiler_params=pltpu.CompilerParams(
            dimension_semantics=("parallel", "arbitrary"),
            vmem_limit_bytes=VMEM_LIMIT),
        name="mlstm",
    )(xp, mod_all, gain_all, w3, cw, cb, wq, wk, wv, wgt, bgt, ng, skip)


M_CHAINS = 2


def _mlstm2_kernel(x_ref, mod_ref, gain_ref, w_min_ref, w_mo_ref, w_mg_ref, cw_ref, cb_ref,
                   wq_ref, wk_ref, wv_ref, wg_ref, bg_ref, ng_ref, skip_ref, o_ref,
                   ct_ref, m_ref, tail_ref, ext_ref, qkv_ref, proj_ref, xc_ref, hn_ref, cvb_ref,
                   *, natural):
    L, dh, d = SEG_LEN, HEAD_DIM, D_MODEL
    halo = (CONV_WIDTH - 1) * SUBS_PER_SEG

    @pl.when(pl.program_id(1) == 0)
    def _():
        ct_ref[...] = jnp.zeros_like(ct_ref)
        m_ref[...] = jnp.zeros_like(m_ref)
        tail_ref[...] = jnp.zeros_like(tail_ref)

    t_row = _chunk_time(lax.broadcasted_iota(jnp.int32, (L, L), 0))
    t_col = _chunk_time(lax.broadcasted_iota(jnp.int32, (L, L), 1))
    causal = t_col <= t_row
    tri_l = jnp.where(causal, 1.0, 0.0).astype(BF16)
    lane = lax.broadcasted_iota(jnp.int32, (L, 128), 1)
    ones_blk = jnp.where(lane == 0, 1.0, 0.0).astype(BF16)
    row16 = lax.broadcasted_iota(jnp.int32, (SUBS_PER_SEG, d), 0)

    def stage_norm_in(c):
        x = x_ref[c]
        if natural:
            x = jnp.swapaxes(x, 0, 1)
        hn_ref[c] = _modulated_norm(x.reshape(L, d), gain_ref[...], mod_ref[c]).astype(BF16)
        proj_ref[c, :, 0:d] = _dot(hn_ref[c], w_min_ref[...])

    def stage_gate_proj(c):
        proj_ref[c, :, d:2 * d] = _dot(hn_ref[c], w_mo_ref[...])
        proj_ref[c, :, 2 * d:3 * d] = _dot(hn_ref[c], w_mg_ref[...])

    def stage_conv(c):
        m_in = proj_ref[c, :, 0:d]
        ext_ref[c, halo:halo + L, :] = m_in
        cvb_ref[c, :, d:2 * d] = m_in.astype(BF16)
        for i in range(CONV_WIDTH - 1):
            lsrc = SUB - (CONV_WIDTH - 1) + i
            r0 = halo + lsrc * SUBS_PER_SEG
            shifted = ext_ref[c, pl.ds(r0 - 1, SUBS_PER_SEG), :]
            ext_ref[c, i * SUBS_PER_SEG:(i + 1) * SUBS_PER_SEG, :] = jnp.where(
                row16 == 0, tail_ref[c, i:i + 1, :], shifted)
            tail_ref[c, i:i + 1, :] = ext_ref[c, r0 + SUBS_PER_SEG - 1:r0 + SUBS_PER_SEG, :]
        acc = m_in * cw_ref[CONV_WIDTH - 1:CONV_WIDTH, :] + cb_ref[...]
        for j in range(CONV_WIDTH - 1):
            back = CONV_WIDTH - 1 - j
            acc = acc + ext_ref[c, pl.ds(halo - back * SUBS_PER_SEG, L), :] * cw_ref[j:j + 1, :]
        xc = acc * _sigmoid(acc)
        xc_ref[c] = xc
        cvb_ref[c, :, 0:d] = xc.astype(BF16)

    def stage_qkv(c):
        for hd in range(HEADS):
            sl = slice(hd * dh, (hd + 1) * dh)
            xcb = cvb_ref[c, :, hd * dh:(hd + 1) * dh]
            qkv_ref[c, :, hd * dh:(hd + 1) * dh] = _dot(xcb, wq_ref[hd]).astype(BF16)
            qkv_ref[c, :, d + hd * dh:d + (hd + 1) * dh] = (
                _dot(xcb, wk_ref[hd]) * (dh ** -0.5)).astype(BF16)
            qkv_ref[c, :, 2 * d + hd * dh:2 * d + (hd + 1) * dh] = (
                _dot(cvb_ref[c, :, d + hd * dh:d + (hd + 1) * dh], wv_ref[hd]).astype(BF16))
        gates = _dot(qkv_ref[c], wg_ref[...]) + bg_ref[...]
        log_f = _log_sigmoid(gates)
        lf_hi, lf_lo = _split_hi_lo(log_f)
        b_cols = _dot(tri_l, lf_hi) + _dot(tri_l, lf_lo)
        return gates.T, b_cols.T, gates, b_cols

    def stage_head(c, hd, gate_forms):
        gates_t, b_rows, gates, b_cols = gate_forms
        sl = slice(hd * dh, (hd + 1) * dh)
        i_r = gates_t[hd:hd + 1, :]
        b_r = b_rows[HEADS + hd:HEADS + hd + 1, :]
        i_c = gates[:, hd:hd + 1]
        b_c = b_cols[:, HEADS + hd:HEADS + hd + 1]
        m_prev = m_ref[c, hd:hd + 1, 0:1]

        log_d = jnp.where(causal, b_c + (i_r - b_r), -jnp.inf)
        m_inter = b_c + m_prev
        m_t = jnp.maximum(m_inter, jnp.max(log_d, axis=-1, keepdims=True))
        dmat = jnp.exp(log_d - m_t)
        qb = qkv_ref[c, :, sl]
        kb = qkv_ref[c, :, d + hd * dh:d + (hd + 1) * dh]
        s = _dot_nt(qb, kb)
        p = (s * dmat).astype(BF16)
        v_aug = jnp.concatenate([qkv_ref[c, :, 2 * d + hd * dh:2 * d + (hd + 1) * dh], ones_blk],
                                axis=-1)
        w_inter = jnp.exp(m_inter - m_t)
        inter = _dot(qb, ct_ref[c, hd].astype(BF16))
        intra = _dot(p, v_aug)
        nd = w_inter * inter + intra
        num = nd[:, 0:dh]
        den = nd[:, dh:dh + 1]
        hh = num / jnp.maximum(jnp.abs(den), jnp.exp(-m_t))

        b_tot = b_c[L - 1:L, :]
        log_w = b_tot - b_c + i_c
        m_next = jnp.maximum(b_tot + m_prev, jnp.max(log_w, axis=0, keepdims=True))
        decay = jnp.exp(b_tot + m_prev - m_next)
        w_c = jnp.exp(log_w - m_next)
        upd = _dot(kb.T, (w_c * v_aug.astype(F32)).astype(BF16))
        ct_ref[c, hd] = decay * ct_ref[c, hd] + upd
        m_ref[c, hd:hd + 1, :] = jnp.broadcast_to(m_next, (1, 128))

        hg = hh * _sigmoid(proj_ref[c, :, d + hd * dh:d + (hd + 1) * dh])
        mu = jnp.mean(hg, axis=-1, keepdims=True)
        dev = hg - mu
        var = jnp.mean(dev * dev, axis=-1, keepdims=True)
        hn = dev * lax.rsqrt(var + EPS) * ng_ref[:, sl] + skip_ref[:, sl] * xc_ref[c, :, sl]
        gate = proj_ref[c, :, 2 * d + hd * dh:2 * d + (hd + 1) * dh]
        y = hn * (gate * _sigmoid(gate))
        o_ref[c, :, :, sl] = y.reshape(SUB, SUBS_PER_SEG, dh).astype(o_ref.dtype)

    a, b = 0, 1
    stage_norm_in(a)
    stage_conv(a)
    stage_norm_in(b)
    stage_gate_proj(a)
    gf_a = stage_qkv(a)
    stage_conv(b)
    stage_gate_proj(b)
    stage_head(a, 0, gf_a)
    gf_b = stage_qkv(b)
    stage_head(a, 1, gf_a)
    stage_head(b, 0, gf_b)
    stage_head(a, 2, gf_a)
    stage_head(b, 1, gf_b)
    stage_head(a, 3, gf_a)
    stage_head(b, 2, gf_b)
    stage_head(b, 3, gf_b)


def _mlstm2(x, mod_all, gain_all, w_in_b, cw, cb, wq, wk, wv, wg, bg, ng, skip, layer, natural):
    bsz = x.shape[0]
    d = D_MODEL
    L = SEG_LEN
    nc = M_CHAINS
    lay2 = lambda shape: pl.BlockSpec((None,) + shape, lambda b, c: (layer, 0, 0))
    lay3 = lambda shape: pl.BlockSpec((None,) + shape, lambda b, c: (layer, 0, 0, 0))
    seg_spec = pl.BlockSpec((nc, SUB, None, SUBS_PER_SEG, d), lambda b, c: (b, 0, c, 0, 0))
    x_spec = (pl.BlockSpec((nc, None, SUBS_PER_SEG, SUB, d), lambda b, c: (b, c, 0, 0, 0))
              if natural else seg_spec)
    w_cols = lambda j: pl.BlockSpec((None, d, d), lambda b, c: (layer, 0, j))
    halo = (CONV_WIDTH - 1) * SUBS_PER_SEG
    return pl.pallas_call(
        functools.partial(_mlstm2_kernel, natural=natural),
        out_shape=jax.ShapeDtypeStruct((bsz, SUB, N_SEG, SUBS_PER_SEG, d), BF16),
        grid=(bsz // nc, N_SEG),
        in_specs=[
            x_spec,
            pl.BlockSpec((None, nc, 1, 3 * d), lambda b, c: (layer, b, 0, 0)),
            lay2((1, d)),
            w_cols(2), w_cols(3), w_cols(4),
            lay2((CONV_WIDTH, d)),
            lay2((1, d)),
            lay3((HEADS, HEAD_DIM, HEAD_DIM)),
            lay3((HEADS, HEAD_DIM, HEAD_DIM)),
            lay3((HEADS, HEAD_DIM, HEAD_DIM)),
            lay2((3 * d, 128)),
            lay2((1, 128)),
            lay2((1, d)),
            lay2((1, d)),
        ],
        out_specs=seg_spec,
        scratch_shapes=[
            pltpu.VMEM((nc, HEADS, HEAD_DIM, M_AUG), F32),
            pltpu.VMEM((nc, 8, 128), F32),
            pltpu.VMEM((nc, 8, d), F32),
            pltpu.VMEM((nc, halo + L, d), F32),
            pltpu.VMEM((nc, L, 3 * d), BF16),
            pltpu.VMEM((nc, L, 3 * d), F32),
            pltpu.VMEM((nc, L, d), F32),
            pltpu.VMEM((nc, L, d), BF16),
            pltpu.VMEM((nc, L, 2 * d), BF16),
        ],
        compiler_params=pltpu.CompilerParams(
            dimension_semantics=("parallel", "arbitrary"),
            vmem_limit_bytes=VMEM_LIMIT),
        name="mlstm",
    )(x, mod_all, gain_all, w_in_b, w_in_b, w_in_b, cw, cb, wq, wk, wv, wg, bg, ng, skip)


def _gelu_tanh(x):
    inner = math.sqrt(2.0 / math.pi) * (x + 0.044715 * (x * x * x))
    return x * (0.5 * (1.0 + jnp.tanh(inner)))


def _out_kernel(yt_ref, gt_ref, my_ref, x_ref, mod_ref, bglu_ref, og_ref, fg_ref,
                wglut_ref, wout_ref, qt_ref, o_ref, *, final, natural):
    d = D_MODEL
    rows = L_PAIR * BLK_LANES
    x_rows = _stream_rows(x_ref, natural)

    def stage_in(s):
        ls = range(s * L_PAIR, (s + 1) * L_PAIR)
        out_m = _dot(my_ref[s * L_PAIR:(s + 1) * L_PAIR].reshape(rows, d),
                     wout_ref[d:2 * d, :])
        yt = jnp.concatenate(
            [yt_ref[:, l * SSM_GROUP:(l + 1) * SSM_GROUP, :].reshape(d, BLK_LANES) for l in ls],
            axis=-1)
        y = _gelu_tanh(_dot(yt, qt_ref[...]))
        return out_m, y

    def stage_glu(s, y):
        ls = range(s * L_PAIR, (s + 1) * L_PAIR)
        glu = y * _sigmoid(_dot(wglut_ref[...], y.astype(BF16)) + bglu_ref[...])
        gate_s = jnp.concatenate([gt_ref[l] for l in ls], axis=-1).astype(F32)
        ms = jnp.mean(glu * glu, axis=0, keepdims=True)
        return (glu * lax.rsqrt(ms + EPS) * og_ref[...] * gate_s).astype(BF16)

    def stage_out(s, out_m, ssm_y):
        out = _dot_tn(ssm_y, wout_ref[0:d, :]) + out_m
        xn = x_rows(s) + mod_ref[:, 2 * d:3 * d] * out
        if final:
            return _rms(xn) * fg_ref[...]
        o_ref[s * L_PAIR:(s + 1) * L_PAIR] = xn.reshape(L_PAIR, N_SEG, SUBS_PER_SEG, d)

    streams = range(OUT_STREAMS)
    ins = [stage_in(s) for s in streams]
    ssm = [stage_glu(s, ins[s][1]) for s in streams]
    outs = [stage_out(s, ins[s][0], ssm[s]) for s in streams]
    if final:
        n_l = OUT_STREAMS * L_PAIR
        by_l = jnp.concatenate(outs, axis=0).reshape(n_l, BLK_LANES, d)
        o_ref[...] = jnp.swapaxes(by_l, 0, 1).reshape(N_SEG, SUBS_PER_SEG, n_l, d)


def _out_stage(yt, gt, my, x, mod_all, bglu_col, og_col, fg, wglut, wout, qperm_t, layer, final,
               natural):
    bsz = x.shape[0]
    d = D_MODEL
    n_l = OUT_STREAMS * L_PAIR
    col = pl.BlockSpec((None, d, 1), lambda b, p: (layer, 0, 0))
    out_dims = (bsz, N_SEG, SUBS_PER_SEG, SUB, d) if final else (bsz, SUB, N_SEG, SUBS_PER_SEG, d)
    return pl.pallas_call(
        functools.partial(_out_kernel, final=final, natural=natural),
        out_shape=jax.ShapeDtypeStruct(out_dims, F32),
        grid=(bsz, SUB // n_l),
        in_specs=[
            pl.BlockSpec((SSM_GROUPS, n_l * SSM_GROUP, BLK_LANES), lambda b, p: (0, p, b)),
            pl.BlockSpec((None, n_l, d, BLK_LANES), lambda b, p: (b, p, 0, 0)),
            _token_spec(False), _token_spec(natural),
            pl.BlockSpec((None, None, 1, 3 * d), lambda b, p: (layer, b, 0, 0)),
            col, col,
            pl.BlockSpec((1, d), lambda b, p: (0, 0)),
            pl.BlockSpec((None, d, d), lambda b, p: (layer, 0, 0)),
            pl.BlockSpec((None, 2 * d, d), lambda b, p: (layer, 0, 0)),
            pl.BlockSpec((L_PAIR * BLK_LANES, L_PAIR * BLK_LANES), lambda b, p: (0, 0)),
        ],
        out_specs=_token_spec(final),
        compiler_params=pltpu.CompilerParams(
            dimension_semantics=("parallel", "parallel"),
            vmem_limit_bytes=VMEM_LIMIT),
        name="out_stage",
    )(yt, gt, my, x, mod_all, bglu_col, og_col, fg, wglut, wout, qperm_t)


def _lane_permutation():
    q = np.zeros((BLK_LANES, BLK_LANES), np.float32)
    for seg in range(N_SEG):
        for kl in range(SUBS_PER_SEG):
            q[seg * SUBS_PER_SEG + kl, kl * N_SEG + seg] = 1.0
    return np.kron(np.eye(L_PAIR, dtype=np.float32), q)


def kernel(x, c, norm_gain, w_mod, b_mod, w_in, ssm_lambda_re, ssm_lambda_im, ssm_log_dt,
           ssm_b_re, ssm_b_im, ssm_c_re, ssm_c_im, ssm_d, ssm_w_glu, ssm_b_glu,
           ssm_out_gain, m_conv_w, m_conv_b, m_wq, m_wk, m_wv, m_w_gates, m_b_igate,
           m_b_fgate, m_norm_gain, m_skip, w_out, final_gain):
    bsz, seq, d = x.shape
    depth = w_in.shape[0]
    assert d == D_MODEL and seq == SEQ

    h = x.reshape(bsz, N_SEG, SUBS_PER_SEG, SUB, d)
    mod_all = _modulation(c, w_mod, b_mod).reshape(depth, bsz, 1, 3 * d)
    rows = lambda v: v.reshape(depth, 1, -1)
    cols = lambda v: v.reshape(depth, -1, 1)
    qperm = jnp.asarray(_lane_permutation(), BF16)
    qperm_t = jnp.asarray(_lane_permutation().T, BF16)

    gain_all = rows(norm_gain)
    w_in_b = w_in.astype(BF16)
    w_ssm_t = w_in_b[:, :, 0:2 * d].transpose(0, 2, 1)
    toept, ws, wot, atab, segtab = _s5_prep(ssm_lambda_re, ssm_lambda_im, ssm_log_dt,
                                            ssm_b_re, ssm_b_im, ssm_c_re, ssm_c_im)
    dcol = jnp.broadcast_to(ssm_d.reshape(depth * SSM_GROUPS, 1, SSM_GROUP),
                            (depth * SSM_GROUPS, SUB, SSM_GROUP)).reshape(-1, S5_ROW, 1)
    wq, wk, wv = m_wq.astype(BF16), m_wk.astype(BF16), m_wv.astype(BF16)
    gate_pad = ((0, 0), (0, 0), (0, 128 - 2 * HEADS))
    wg = jnp.pad(m_w_gates, gate_pad).astype(BF16)
    bg = jnp.pad(jnp.concatenate([m_b_igate, m_b_fgate], axis=-1).reshape(depth, 1, 2 * HEADS),
                 gate_pad)
    wglut = ssm_w_glu.astype(BF16).transpose(0, 2, 1)
    wout = w_out.astype(BF16)
    fg = final_gain.reshape(1, d)

    for l in range(depth):
        natural = l == 0
        final = l == depth - 1
        zt, gt = _s5_in(h, mod_all, gain_all, w_ssm_t, qperm, l, natural)
        yt = _s5_core(zt, toept, ws, wot, atab, segtab, dcol, bsz, l)
        my = _mlstm2(h, mod_all, gain_all, w_in_b, m_conv_w, rows(m_conv_b), wq, wk, wv, wg, bg,
                     rows(m_norm_gain), rows(m_skip), l, natural)
        h = _out_stage(yt, gt, my, h, mod_all, cols(ssm_b_glu), cols(ssm_out_gain), fg,
                       wglut, wout, qperm_t, l, final, natural)

    return h.reshape(bsz, seq, d)
```

```python
import functools
import math

import numpy as np
import jax
import jax.numpy as jnp
from jax import lax
from jax.experimental import pallas as pl
from jax.experimental.pallas import tpu as pltpu

F32 = jnp.float32
BF16 = jnp.bfloat16

D_MODEL = 1024
SSM_GROUP = 16
SSM_GROUPS = D_MODEL // SSM_GROUP
SSM_STATE = 64
HEADS = 4
HEAD_DIM = D_MODEL // HEADS
CONV_WIDTH = 4
EPS = 1e-6

SUB = 16
SUBS_PER_SEG = 16
SEG_LEN = SUB * SUBS_PER_SEG
N_SEG = 8
SEQ = N_SEG * SEG_LEN
BLK_LANES = N_SEG * SUBS_PER_SEG
S5_ROW = SUB * SSM_GROUP
STATE_HALF = 128
STATE_LANES = 2 * STATE_HALF
S5_GROUPS_PER_STEP = 4
L_PAIR = 2
OUT_STREAMS = 4
M_AUG = HEAD_DIM + 128
VMEM_LIMIT = 56 * 1024 * 1024

_NT = (((1,), (1,)), ((), ()))
_TN = (((0,), (0,)), ((), ()))


def _dot(a, b):
    return jnp.dot(a, b, preferred_element_type=F32)


def _dot_nt(a, b):
    return lax.dot_general(a, b, _NT, preferred_element_type=F32)


def _dot_tn(a, b):
    return lax.dot_general(a, b, _TN, preferred_element_type=F32)


def _sigmoid(x):
    return jax.nn.sigmoid(x)


def _rms(x):
    return x * lax.rsqrt(jnp.mean(x * x, axis=-1, keepdims=True) + EPS)


def _modulated_norm(x, gain, mod):
    shift = mod[:, 0:D_MODEL]
    scale = mod[:, D_MODEL:2 * D_MODEL]
    return _rms(x) * (gain * (1.0 + scale)) + shift


def _split_hi_lo(v):
    hi = v.astype(BF16)
    lo = (v - hi.astype(F32)).astype(BF16)
    return hi, lo


def _dot_nt_f32(a, b):
    a_hi, a_lo = _split_hi_lo(a)
    b_hi, b_lo = _split_hi_lo(b)
    return _dot_nt(a_hi, b_hi) + _dot_nt(a_hi, b_lo) + _dot_nt(a_lo, b_hi)


def _cmul(a_re, a_im, b_re, b_im):
    return a_re * b_re - a_im * b_im, a_re * b_im + a_im * b_re


def _cmul_add(p_re, p_im, s_re, s_im, add_re, add_im):
    return (p_re * s_re - p_im * s_im + add_re,
            p_re * s_im + p_im * s_re + add_im)


def _mod_kernel(c_ref, w_ref, b_ref, o_ref):
    cv = c_ref[...]
    act = cv * _sigmoid(cv)
    o_ref[...] = _dot(act.astype(BF16), w_ref[...].astype(BF16)) + b_ref[...]


def _modulation(c, w_mod, b_mod):
    depth, d, n = w_mod.shape
    bsz = c.shape[0]
    return pl.pallas_call(
        _mod_kernel,
        out_shape=jax.ShapeDtypeStruct((depth, bsz, n), F32),
        grid=(depth, n // d),
        in_specs=[
            pl.BlockSpec((bsz, d), lambda l, j: (0, 0)),
            pl.BlockSpec((None, d, d), lambda l, j: (l, 0, j)),
            pl.BlockSpec((None, 1, d), lambda l, j: (l, 0, j)),
        ],
        out_specs=pl.BlockSpec((None, bsz, d), lambda l, j: (l, 0, j)),
        compiler_params=pltpu.CompilerParams(
            dimension_semantics=("parallel", "parallel"),
            vmem_limit_bytes=VMEM_LIMIT),
        name="adaln_mod",
    )(c, w_mod, b_mod.reshape(depth, 1, n))


def _s5_prep_kernel(lre_ref, lim_ref, ldt_ref, btre_ref, btim_ref, cre_ref, cim_ref,
                    toept_ref, ws_ref, wot_ref, atab_ref, seg_ref):
    gp = S5_GROUPS_PER_STEP
    lane_blk = lax.broadcasted_iota(jnp.int32, (S5_ROW, S5_ROW), 1) // SSM_GROUP
    row8 = lax.broadcasted_iota(jnp.int32, (8, STATE_HALF), 0)
    for j in range(gp):
        lr = lre_ref[j]
        li = lim_ref[j]
        dt = jnp.exp(ldt_ref[j])
        mag = jnp.exp(lr * dt)
        a_re = mag * jnp.cos(li * dt)
        a_im = mag * jnp.sin(li * dt)
        inv = 1.0 / (lr * lr + li * li)
        k_re = ((a_re - 1.0) * lr + a_im * li) * inv
        k_im = (a_im * lr - (a_re - 1.0) * li) * inv
        bb_re, bb_im = _cmul(k_re, k_im, btre_ref[j], btim_ref[j])
        c_re = cre_ref[j]
        c_im = cim_ref[j]

        pw = [(jnp.ones_like(a_re), jnp.zeros_like(a_re))]
        for _ in range(SUB):
            pw.append(_cmul(pw[-1][0], pw[-1][1], a_re, a_im))

        ws_rows, wot_rows, ca_re, ca_im = [], [], [], []
        for l in range(SUB):
            w_re, w_im = _cmul(pw[SUB - 1 - l][0], pw[SUB - 1 - l][1], bb_re, bb_im)
            ws_rows.append(jnp.concatenate([w_re, w_im], axis=-1))
            o_re, o_im = _cmul(c_re, c_im, pw[l + 1][0], pw[l + 1][1])
            wot_rows.append(jnp.concatenate([o_re, -o_im], axis=-1))
            g_re, g_im = _cmul(c_re, c_im, pw[l][0], pw[l][1])
            ca_re.append(g_re)
            ca_im.append(g_im)
        ws_ref[j] = jnp.concatenate(ws_rows, axis=0).astype(BF16)
        wot_ref[j] = jnp.concatenate(wot_rows, axis=0).astype(BF16)

        bbt_re = jnp.concatenate([bb_re] * SUB, axis=0)
        bbt_im = jnp.concatenate([bb_im] * SUB, axis=0)
        kw = (_dot_nt_f32(jnp.concatenate(ca_re, axis=0), bbt_re)
              - _dot_nt_f32(jnp.concatenate(ca_im, axis=0), bbt_im))
        toep = jnp.where(lane_blk == 0, kw, 0.0)
        for lp in range(1, SUB):
            shifted = jnp.concatenate(
                [jnp.zeros((SSM_GROUP * lp, S5_ROW), F32), kw[0:S5_ROW - SSM_GROUP * lp]], axis=0)
            toep = jnp.where(lane_blk == lp, shifted, toep)
        toept_ref[j] = toep.astype(BF16)

        s_re, s_im = pw[SUB]
        atab_ref[j] = jnp.concatenate([jnp.broadcast_to(s_re, (8, STATE_HALF)),
                                       jnp.broadcast_to(s_im, (8, STATE_HALF))], axis=0)
        for _ in range(4):
            s_re, s_im = _cmul(s_re, s_im, s_re, s_im)
        seg = jnp.zeros((8, STATE_HALF), F32)
        for i in range(3):
            seg = jnp.where(row8 == i, s_re, jnp.where(row8 == 3 + i, s_im, seg))
            s_re, s_im = _cmul(s_re, s_im, s_re, s_im)
        seg_ref[j] = seg


def _s5_prep(lam_re, lam_im, log_dt, b_re, b_im, c_re, c_im):
    depth, g, p = lam_re.shape
    cg = SSM_GROUP
    n = depth * g
    gp = S5_GROUPS_PER_STEP
    lane_pad = ((0, 0), (0, 0), (0, STATE_HALF - p))

    def state_rows(v, fill):
        v = v.reshape(n, -1, p)
        return jnp.pad(v, lane_pad, constant_values=fill)

    args = (state_rows(lam_re, -1.0), state_rows(lam_im, 0.0), log_dt.reshape(n, 1, 1),
            state_rows(b_re.transpose(0, 1, 3, 2), 0.0), state_rows(b_im.transpose(0, 1, 3, 2), 0.0),
            state_rows(c_re, 0.0), state_rows(c_im, 0.0))
    vec = pl.BlockSpec((gp, 1, STATE_HALF), lambda i: (i, 0, 0))
    mat = pl.BlockSpec((gp, cg, STATE_HALF), lambda i: (i, 0, 0))
    big = pl.BlockSpec((gp, S5_ROW, STATE_LANES), lambda i: (i, 0, 0))
    return pl.pallas_call(
        _s5_prep_kernel,
        out_shape=(jax.ShapeDtypeStruct((n, S5_ROW, S5_ROW), BF16),
                   jax.ShapeDtypeStruct((n, S5_ROW, STATE_LANES), BF16),
                   jax.ShapeDtypeStruct((n, S5_ROW, STATE_LANES), BF16),
                   jax.ShapeDtypeStruct((n, 16, STATE_HALF), F32),
                   jax.ShapeDtypeStruct((n, 8, STATE_HALF), F32)),
        grid=(n // gp,),
        in_specs=[vec, vec, pl.BlockSpec((gp, 1, 1), lambda i: (i, 0, 0)), mat, mat, mat, mat],
        out_specs=(big, big, big,
                   pl.BlockSpec((gp, 16, STATE_HALF), lambda i: (i, 0, 0)),
                   pl.BlockSpec((gp, 8, STATE_HALF), lambda i: (i, 0, 0))),
        compiler_params=pltpu.CompilerParams(
            dimension_semantics=("parallel",),
            vmem_limit_bytes=VMEM_LIMIT),
        name="s5_prep",
    )(*args)


def _stream_rows(x_ref, natural):
    n_l = OUT_STREAMS * L_PAIR
    rows = L_PAIR * BLK_LANES
    if natural:
        by_l = jnp.swapaxes(x_ref[...].reshape(BLK_LANES, n_l, D_MODEL), 0, 1)
        return lambda s: by_l[s * L_PAIR:(s + 1) * L_PAIR].reshape(rows, D_MODEL)
    return lambda s: x_ref[s * L_PAIR:(s + 1) * L_PAIR].reshape(rows, D_MODEL)


def _s5_in_kernel(x_ref, mod_ref, gain_ref, wut_ref, wgt_ref, q_ref, zt_ref, gt_ref, *,
                  natural):
    d = D_MODEL
    x_rows = _stream_rows(x_ref, natural)

    def stage_norm(s):
        return _modulated_norm(x_rows(s), gain_ref[...], mod_ref[...]).astype(BF16)

    def stage_proj(s, h):
        ut = _dot_nt(wut_ref[...], h).astype(BF16)
        sgt = _dot_nt(wgt_ref[...], h)
        p = _dot(ut, q_ref[...]).astype(BF16)
        for l in range(L_PAIR):
            lanes = slice(l * BLK_LANES, (l + 1) * BLK_LANES)
            row0 = (s * L_PAIR + l) * SSM_GROUP
            zt_ref[:, row0:row0 + SSM_GROUP, :] = (
                p[:, lanes].reshape(SSM_GROUPS, SSM_GROUP, BLK_LANES))
            g = sgt[:, lanes]
            gt_ref[s * L_PAIR + l] = (g * _sigmoid(g)).astype(BF16)

    hs = [stage_norm(s) for s in range(OUT_STREAMS)]
    for s in range(OUT_STREAMS):
        stage_proj(s, hs[s])


def _token_spec(natural):
    n_l = OUT_STREAMS * L_PAIR
    if natural:
        assert n_l == 8
        return pl.BlockSpec((None, N_SEG, SUBS_PER_SEG, n_l, D_MODEL), lambda b, p: (b, 0, 0, p, 0))
    return pl.BlockSpec((None, n_l, N_SEG, SUBS_PER_SEG, D_MODEL), lambda b, p: (b, p, 0, 0, 0))


def _s5_in(x, mod_all, gain_all, w_ssm_t, qperm, layer, natural):
    bsz = x.shape[0]
    d = D_MODEL
    n_l = OUT_STREAMS * L_PAIR
    return pl.pallas_call(
        functools.partial(_s5_in_kernel, natural=natural),
        out_shape=(jax.ShapeDtypeStruct((SSM_GROUPS, S5_ROW, bsz * BLK_LANES), BF16),
                   jax.ShapeDtypeStruct((bsz, SUB, d, BLK_LANES), BF16)),
        grid=(bsz, SUB // n_l),
        in_specs=[
            _token_spec(natural),
            pl.BlockSpec((None, None, 1, 3 * d), lambda b, p: (layer, b, 0, 0)),
            pl.BlockSpec((None, 1, d), lambda b, p: (layer, 0, 0)),
            pl.BlockSpec((None, d, d), lambda b, p: (layer, 0, 0)),
            pl.BlockSpec((None, d, d), lambda b, p: (layer, 1, 0)),
            pl.BlockSpec((L_PAIR * BLK_LANES, L_PAIR * BLK_LANES), lambda b, p: (0, 0)),
        ],
        out_specs=(
            pl.BlockSpec((SSM_GROUPS, n_l * SSM_GROUP, BLK_LANES), lambda b, p: (0, p, b)),
            pl.BlockSpec((None, n_l, d, BLK_LANES), lambda b, p: (b, p, 0, 0)),
        ),
        compiler_params=pltpu.CompilerParams(
            dimension_semantics=("parallel", "parallel"),
            vmem_limit_bytes=VMEM_LIMIT),
        name="s5_in",
    )(x, mod_all, gain_all, w_ssm_t, w_ssm_t, qperm)


def _s5_core_kernel(zt_ref, toept_ref, ws_ref, wot_ref, atab_ref, seg_ref, dcol_ref,
                    yt_ref, loc_ref, sprev_ref, *, bsz):
    gs = S5_GROUPS_PER_STEP
    half = STATE_HALF
    row8 = lax.broadcasted_iota(jnp.int32, (N_SEG, half), 0)

    def shift_down(v, n):
        return jnp.where(row8 >= n, pltpu.roll(v, n, axis=0), 0.0)

    for j in range(gs):
        loc_ref[j] = _dot_tn(zt_ref[j], ws_ref[j])
        a_re = atab_ref[j, 0:8, :]
        a_im = atab_ref[j, 8:16, :]
        for b in range(bsz):
            base = b * BLK_LANES

            def rows(kl, base=base):
                return pl.ds(base + kl * N_SEG, N_SEG)

            e_re = jnp.zeros((N_SEG, half), F32)
            e_im = jnp.zeros((N_SEG, half), F32)
            for kl in range(SUBS_PER_SEG):
                e_re, e_im = _cmul_add(a_re, a_im, e_re, e_im,
                                       loc_ref[j, rows(kl), 0:half],
                                       loc_ref[j, rows(kl), half:STATE_LANES])
            for i, n in enumerate((1, 2, 4)):
                p_re = seg_ref[j, i:i + 1, :]
                p_im = seg_ref[j, 3 + i:4 + i, :]
                e_re, e_im = _cmul_add(p_re, p_im, shift_down(e_re, n), shift_down(e_im, n),
                                       e_re, e_im)
            s_re = shift_down(e_re, 1)
            s_im = shift_down(e_im, 1)
            for kl in range(SUBS_PER_SEG):
                sprev_ref[j, rows(kl), 0:half] = s_re
                sprev_ref[j, rows(kl), half:STATE_LANES] = s_im
                s_re, s_im = _cmul_add(a_re, a_im, s_re, s_im,
                                       loc_ref[j, rows(kl), 0:half],
                                       loc_ref[j, rows(kl), half:STATE_LANES])

        zt = zt_ref[j]
        out = (_dot(toept_ref[j], zt)
               + _dot_nt(wot_ref[j], sprev_ref[j].astype(BF16))
               + dcol_ref[j] * zt.astype(F32))
        yt_ref[j] = out.astype(BF16)


def _s5_core(zt, toept, ws, wot, atab, segtab, dcol, bsz, layer):
    g, rows, lanes = zt.shape
    gs = S5_GROUPS_PER_STEP
    off = layer * (g // gs)
    wspec = pl.BlockSpec((gs, S5_ROW, STATE_LANES), lambda i: (off + i, 0, 0))
    return pl.pallas_call(
        functools.partial(_s5_core_kernel, bsz=bsz),
        out_shape=jax.ShapeDtypeStruct((g, rows, lanes), BF16),
        grid=(g // gs,),
        in_specs=[
            pl.BlockSpec((gs, rows, lanes), lambda i: (i, 0, 0)),
            wspec, wspec, wspec,
            pl.BlockSpec((gs, 16, STATE_HALF), lambda i: (off + i, 0, 0)),
            pl.BlockSpec((gs, 8, STATE_HALF), lambda i: (off + i, 0, 0)),
            pl.BlockSpec((gs, S5_ROW, 1), lambda i: (off + i, 0, 0)),
        ],
        out_specs=pl.BlockSpec((gs, rows, lanes), lambda i: (i, 0, 0)),
        scratch_shapes=[pltpu.VMEM((gs, lanes, STATE_LANES), F32),
                        pltpu.VMEM((gs, lanes, STATE_LANES), F32)],
        compiler_params=pltpu.CompilerParams(
            dimension_semantics=("parallel",),
            vmem_limit_bytes=VMEM_LIMIT),
        name="s5_core",
    )(zt, toept, ws, wot, atab, segtab, dcol)


def _log_sigmoid(x):
    return -(jnp.maximum(-x, 0.0) + jnp.log1p(jnp.exp(-jnp.abs(x))))


def _chunk_time(idx):
    return ((idx & (SUBS_PER_SEG - 1)) * SUB) | (idx >> 4)


M_CHAINS = 2


def _mlstm_kernel(x_ref, mod_ref, gain_ref, w_min_ref, w_mo_ref, w_mg_ref, cw_ref, cb_ref,
                   wq_ref, wk_ref, wv_ref, wg_ref, bg_ref, ng_ref, skip_ref, o_ref,
                   ct_ref, m_ref, tail_ref, ext_ref, qkv_ref, proj_ref, xc_ref, hn_ref, cvb_ref,
                   *, natural):
    L, dh, d = SEG_LEN, HEAD_DIM, D_MODEL
    halo = (CONV_WIDTH - 1) * SUBS_PER_SEG

    @pl.when(pl.program_id(1) == 0)
    def _():
        ct_ref[...] = jnp.zeros_like(ct_ref)
        m_ref[...] = jnp.zeros_like(m_ref)
        tail_ref[...] = jnp.zeros_like(tail_ref)

    t_row = _chunk_time(lax.broadcasted_iota(jnp.int32, (L, L), 0))
    t_col = _chunk_time(lax.broadcasted_iota(jnp.int32, (L, L), 1))
    causal = t_col <= t_row
    tri_l = jnp.where(causal, 1.0, 0.0).astype(BF16)
    lane = lax.broadcasted_iota(jnp.int32, (L, 128), 1)
    ones_blk = jnp.where(lane == 0, 1.0, 0.0).astype(BF16)
    row16 = lax.broadcasted_iota(jnp.int32, (SUBS_PER_SEG, d), 0)

    def stage_norm_in(c):
        x = x_ref[c]
        if natural:
            x = jnp.swapaxes(x, 0, 1)
        hn_ref[c] = _modulated_norm(x.reshape(L, d), gain_ref[...], mod_ref[c]).astype(BF16)
        proj_ref[c, :, 0:d] = _dot(hn_ref[c], w_min_ref[...])

    def stage_gate_proj(c):
        proj_ref[c, :, d:2 * d] = _dot(hn_ref[c], w_mo_ref[...])
        proj_ref[c, :, 2 * d:3 * d] = _dot(hn_ref[c], w_mg_ref[...])

    def stage_conv(c):
        m_in = proj_ref[c, :, 0:d]
        ext_ref[c, halo:halo + L, :] = m_in
        cvb_ref[c, :, d:2 * d] = m_in.astype(BF16)
        for i in range(CONV_WIDTH - 1):
            lsrc = SUB - (CONV_WIDTH - 1) + i
            r0 = halo + lsrc * SUBS_PER_SEG
            shifted = ext_ref[c, pl.ds(r0 - 1, SUBS_PER_SEG), :]
            ext_ref[c, i * SUBS_PER_SEG:(i + 1) * SUBS_PER_SEG, :] = jnp.where(
                row16 == 0, tail_ref[c, i:i + 1, :], shifted)
            tail_ref[c, i:i + 1, :] = ext_ref[c, r0 + SUBS_PER_SEG - 1:r0 + SUBS_PER_SEG, :]
        acc = m_in * cw_ref[CONV_WIDTH - 1:CONV_WIDTH, :] + cb_ref[...]
        for j in range(CONV_WIDTH - 1):
            back = CONV_WIDTH - 1 - j
            acc = acc + ext_ref[c, pl.ds(halo - back * SUBS_PER_SEG, L), :] * cw_ref[j:j + 1, :]
        xc = acc * _sigmoid(acc)
        xc_ref[c] = xc
        cvb_ref[c, :, 0:d] = xc.astype(BF16)

    def stage_qkv(c):
        for hd in range(HEADS):
            sl = slice(hd * dh, (hd + 1) * dh)
            xcb = cvb_ref[c, :, hd * dh:(hd + 1) * dh]
            qkv_ref[c, :, hd * dh:(hd + 1) * dh] = _dot(xcb, wq_ref[hd]).astype(BF16)
            qkv_ref[c, :, d + hd * dh:d + (hd + 1) * dh] = (
                _dot(xcb, wk_ref[hd]) * (dh ** -0.5)).astype(BF16)
            qkv_ref[c, :, 2 * d + hd * dh:2 * d + (hd + 1) * dh] = (
                _dot(cvb_ref[c, :, d + hd * dh:d + (hd + 1) * dh], wv_ref[hd]).astype(BF16))
        gates = _dot(qkv_ref[c], wg_ref[...]) + bg_ref[...]
        log_f = _log_sigmoid(gates)
        lf_hi, lf_lo = _split_hi_lo(log_f)
        b_cols = _dot(tri_l, lf_hi) + _dot(tri_l, lf_lo)
        return gates.T, b_cols.T, gates, b_cols

    def stage_head(c, hd, gate_forms):
        gates_t, b_rows, gates, b_cols = gate_forms
        sl = slice(hd * dh, (hd + 1) * dh)
        i_r = gates_t[hd:hd + 1, :]
        b_r = b_rows[HEADS + hd:HEADS + hd + 1, :]
        i_c = gates[:, hd:hd + 1]
        b_c = b_cols[:, HEADS + hd:HEADS + hd + 1]
        m_prev = m_ref[c, hd:hd + 1, 0:1]

        log_d = jnp.where(causal, b_c + (i_r - b_r), -jnp.inf)
        m_inter = b_c + m_prev
        m_t = jnp.maximum(m_inter, jnp.max(log_d, axis=-1, keepdims=True))
        dmat = jnp.exp(log_d - m_t)
        qb = qkv_ref[c, :, sl]
        kb = qkv_ref[c, :, d + hd * dh:d + (hd + 1) * dh]
        s = _dot_nt(qb, kb)
        p = (s * dmat).astype(BF16)
        v_aug = jnp.concatenate([qkv_ref[c, :, 2 * d + hd * dh:2 * d + (hd + 1) * dh], ones_blk],
                                axis=-1)
        w_inter = jnp.exp(m_inter - m_t)
        inter = _dot(qb, ct_ref[c, hd].astype(BF16))
        intra = _dot(p, v_aug)
        nd = w_inter * inter + intra
        num = nd[:, 0:dh]
        den = nd[:, dh:dh + 1]
        hh = num / jnp.maximum(jnp.abs(den), jnp.exp(-m_t))

        b_tot = b_c[L - 1:L, :]
        log_w = b_tot - b_c + i_c
        m_next = jnp.maximum(b_tot + m_prev, jnp.max(log_w, axis=0, keepdims=True))
        decay = jnp.exp(b_tot + m_prev - m_next)
        w_c = jnp.exp(log_w - m_next)
        upd = _dot(kb.T, (w_c * v_aug.astype(F32)).astype(BF16))
        ct_ref[c, hd] = decay * ct_ref[c, hd] + upd
        m_ref[c, hd:hd + 1, :] = jnp.broadcast_to(m_next, (1, 128))

        hg = hh * _sigmoid(proj_ref[c, :, d + hd * dh:d + (hd + 1) * dh])
        mu = jnp.mean(hg, axis=-1, keepdims=True)
        dev = hg - mu
        var = jnp.mean(dev * dev, axis=-1, keepdims=True)
        hn = dev * lax.rsqrt(var + EPS) * ng_ref[:, sl] + skip_ref[:, sl] * xc_ref[c, :, sl]
        gate = proj_ref[c, :, 2 * d + hd * dh:2 * d + (hd + 1) * dh]
        y = hn * (gate * _sigmoid(gate))
        o_ref[c, :, :, sl] = y.reshape(SUB, SUBS_PER_SEG, dh).astype(o_ref.dtype)

    a, b = 0, 1
    stage_norm_in(a)
    stage_conv(a)
    stage_norm_in(b)
    stage_gate_proj(a)
    gf_a = stage_qkv(a)
    stage_conv(b)
    stage_gate_proj(b)
    stage_head(a, 0, gf_a)
    gf_b = stage_qkv(b)
    stage_head(a, 1, gf_a)
    stage_head(b, 0, gf_b)
    stage_head(a, 2, gf_a)
    stage_head(b, 1, gf_b)
    stage_head(a, 3, gf_a)
    stage_head(b, 2, gf_b)
    stage_head(b, 3, gf_b)


def _mlstm(x, mod_all, gain_all, w_in_b, cw, cb, wq, wk, wv, wg, bg, ng, skip, layer, natural):
    bsz = x.shape[0]
    d = D_MODEL
    L = SEG_LEN
    nc = M_CHAINS
    lay2 = lambda shape: pl.BlockSpec((None,) + shape, lambda b, c: (layer, 0, 0))
    lay3 = lambda shape: pl.BlockSpec((None,) + shape, lambda b, c: (layer, 0, 0, 0))
    seg_spec = pl.BlockSpec((nc, SUB, None, SUBS_PER_SEG, d), lambda b, c: (b, 0, c, 0, 0))
    x_spec = (pl.BlockSpec((nc, None, SUBS_PER_SEG, SUB, d), lambda b, c: (b, c, 0, 0, 0))
              if natural else seg_spec)
    w_cols = lambda j: pl.BlockSpec((None, d, d), lambda b, c: (layer, 0, j))
    halo = (CONV_WIDTH - 1) * SUBS_PER_SEG
    return pl.pallas_call(
        functools.partial(_mlstm_kernel, natural=natural),
        out_shape=jax.ShapeDtypeStruct((bsz, SUB, N_SEG, SUBS_PER_SEG, d), BF16),
        grid=(bsz // nc, N_SEG),
        in_specs=[
            x_spec,
            pl.BlockSpec((None, nc, 1, 3 * d), lambda b, c: (layer, b, 0, 0)),
            lay2((1, d)),
            w_cols(2), w_cols(3), w_cols(4),
            lay2((CONV_WIDTH, d)),
            lay2((1, d)),
            lay3((HEADS, HEAD_DIM, HEAD_DIM)),
            lay3((HEADS, HEAD_DIM, HEAD_DIM)),
            lay3((HEADS, HEAD_DIM, HEAD_DIM)),
            lay2((3 * d, 128)),
            lay2((1, 128)),
            lay2((1, d)),
            lay2((1, d)),
        ],
        out_specs=seg_spec,
        scratch_shapes=[
            pltpu.VMEM((nc, HEADS, HEAD_DIM, M_AUG), F32),
            pltpu.VMEM((nc, 8, 128), F32),
            pltpu.VMEM((nc, 8, d), F32),
            pltpu.VMEM((nc, halo + L, d), F32),
            pltpu.VMEM((nc, L, 3 * d), BF16),
            pltpu.VMEM((nc, L, 3 * d), F32),
            pltpu.VMEM((nc, L, d), F32),
            pltpu.VMEM((nc, L, d), BF16),
            pltpu.VMEM((nc, L, 2 * d), BF16),
        ],
        compiler_params=pltpu.CompilerParams(
            dimension_semantics=("parallel", "arbitrary"),
            vmem_limit_bytes=VMEM_LIMIT),
        name="mlstm",
    )(x, mod_all, gain_all, w_in_b, w_in_b, w_in_b, cw, cb, wq, wk, wv, wg, bg, ng, skip)


def _gelu_tanh(x):
    inner = math.sqrt(2.0 / math.pi) * (x + 0.044715 * (x * x * x))
    return x * (0.5 * (1.0 + jnp.tanh(inner)))


def _out_kernel(yt_ref, gt_ref, my_ref, x_ref, mod_ref, bglu_ref, og_ref, fg_ref,
                wglut_ref, wout_ref, qt_ref, o_ref, *, final, natural):
    d = D_MODEL
    rows = L_PAIR * BLK_LANES
    x_rows = _stream_rows(x_ref, natural)

    def stage_in(s):
        ls = range(s * L_PAIR, (s + 1) * L_PAIR)
        out_m = _dot(my_ref[s * L_PAIR:(s + 1) * L_PAIR].reshape(rows, d),
                     wout_ref[d:2 * d, :])
        yt = jnp.concatenate(
            [yt_ref[:, l * SSM_GROUP:(l + 1) * SSM_GROUP, :].reshape(d, BLK_LANES) for l in ls],
            axis=-1)
        y = _gelu_tanh(_dot(yt, qt_ref[...]))
        return out_m, y

    def stage_glu(s, y):
        ls = range(s * L_PAIR, (s + 1) * L_PAIR)
        glu = y * _sigmoid(_dot(wglut_ref[...], y.astype(BF16)) + bglu_ref[...])
        gate_s = jnp.concatenate([gt_ref[l] for l in ls], axis=-1).astype(F32)
        ms = jnp.mean(glu * glu, axis=0, keepdims=True)
        return (glu * lax.rsqrt(ms + EPS) * og_ref[...] * gate_s).astype(BF16)

    def stage_out(s, out_m, ssm_y):
        out = _dot_tn(ssm_y, wout_ref[0:d, :]) + out_m
        xn = x_rows(s) + mod_ref[:, 2 * d:3 * d] * out
        if final:
            return _rms(xn) * fg_ref[...]
        o_ref[s * L_PAIR:(s + 1) * L_PAIR] = xn.reshape(L_PAIR, N_SEG, SUBS_PER_SEG, d)

    streams = range(OUT_STREAMS)
    ins = [stage_in(s) for s in streams]
    ssm = [stage_glu(s, ins[s][1]) for s in streams]
    outs = [stage_out(s, ins[s][0], ssm[s]) for s in streams]
    if final:
        n_l = OUT_STREAMS * L_PAIR
        by_l = jnp.concatenate(outs, axis=0).reshape(n_l, BLK_LANES, d)
        o_ref[...] = jnp.swapaxes(by_l, 0, 1).reshape(N_SEG, SUBS_PER_SEG, n_l, d)


def _out_stage(yt, gt, my, x, mod_all, bglu_col, og_col, fg, wglut, wout, qperm_t, layer, final,
               natural):
    bsz = x.shape[0]
    d = D_MODEL
    n_l = OUT_STREAMS * L_PAIR
    col = pl.BlockSpec((None, d, 1), lambda b, p: (layer, 0, 0))
    out_dims = (bsz, N_SEG, SUBS_PER_SEG, SUB, d) if final else (bsz, SUB, N_SEG, SUBS_PER_SEG, d)
    return pl.pallas_call(
        functools.partial(_out_kernel, final=final, natural=natural),
        out_shape=jax.ShapeDtypeStruct(out_dims, F32),
        grid=(bsz, SUB // n_l),
        in_specs=[
            pl.BlockSpec((SSM_GROUPS, n_l * SSM_GROUP, BLK_LANES), lambda b, p: (0, p, b)),
            pl.BlockSpec((None, n_l, d, BLK_LANES), lambda b, p: (b, p, 0, 0)),
            _token_spec(False), _token_spec(natural),
            pl.BlockSpec((None, None, 1, 3 * d), lambda b, p: (layer, b, 0, 0)),
            col, col,
            pl.BlockSpec((1, d), lambda b, p: (0, 0)),
            pl.BlockSpec((None, d, d), lambda b, p: (layer, 0, 0)),
            pl.BlockSpec((None, 2 * d, d), lambda b, p: (layer, 0, 0)),
            pl.BlockSpec((L_PAIR * BLK_LANES, L_PAIR * BLK_LANES), lambda b, p: (0, 0)),
        ],
        out_specs=_token_spec(final),
        compiler_params=pltpu.CompilerParams(
            dimension_semantics=("parallel", "parallel"),
            vmem_limit_bytes=VMEM_LIMIT),
        name="out_stage",
    )(yt, gt, my, x, mod_all, bglu_col, og_col, fg, wglut, wout, qperm_t)


def _lane_permutation():
    q = np.zeros((BLK_LANES, BLK_LANES), np.float32)
    for seg in range(N_SEG):
        for kl in range(SUBS_PER_SEG):
            q[seg * SUBS_PER_SEG + kl, kl * N_SEG + seg] = 1.0
    return np.kron(np.eye(L_PAIR, dtype=np.float32), q)


def kernel(x, c, norm_gain, w_mod, b_mod, w_in, ssm_lambda_re, ssm_lambda_im, ssm_log_dt,
           ssm_b_re, ssm_b_im, ssm_c_re, ssm_c_im, ssm_d, ssm_w_glu, ssm_b_glu,
           ssm_out_gain, m_conv_w, m_conv_b, m_wq, m_wk, m_wv, m_w_gates, m_b_igate,
           m_b_fgate, m_norm_gain, m_skip, w_out, final_gain):
    bsz, seq, d = x.shape
    depth = w_in.shape[0]
    assert d == D_MODEL and seq == SEQ

    h = x.reshape(bsz, N_SEG, SUBS_PER_SEG, SUB, d)
    mod_all = _modulation(c, w_mod, b_mod).reshape(depth, bsz, 1, 3 * d)
    rows = lambda v: v.reshape(depth, 1, -1)
    cols = lambda v: v.reshape(depth, -1, 1)
    qperm = jnp.asarray(_lane_permutation(), BF16)
    qperm_t = jnp.asarray(_lane_permutation().T, BF16)

    gain_all = rows(norm_gain)
    w_in_b = w_in.astype(BF16)
    w_ssm_t = w_in_b[:, :, 0:2 * d].transpose(0, 2, 1)
    toept, ws, wot, atab, segtab = _s5_prep(ssm_lambda_re, ssm_lambda_im, ssm_log_dt,
                                            ssm_b_re, ssm_b_im, ssm_c_re, ssm_c_im)
    dcol = jnp.broadcast_to(ssm_d.reshape(depth * SSM_GROUPS, 1, SSM_GROUP),
                            (depth * SSM_GROUPS, SUB, SSM_GROUP)).reshape(-1, S5_ROW, 1)
    wq, wk, wv = m_wq.astype(BF16), m_wk.astype(BF16), m_wv.astype(BF16)
    gate_pad = ((0, 0), (0, 0), (0, 128 - 2 * HEADS))
    wg = jnp.pad(m_w_gates, gate_pad).astype(BF16)
    bg = jnp.pad(jnp.concatenate([m_b_igate, m_b_fgate], axis=-1).reshape(depth, 1, 2 * HEADS),
                 gate_pad)
    wglut = ssm_w_glu.astype(BF16).transpose(0, 2, 1)
    wout = w_out.astype(BF16)
    fg = final_gain.reshape(1, d)

    for l in range(depth):
        natural = l == 0
        final = l == depth - 1
        zt, gt = _s5_in(h, mod_all, gain_all, w_ssm_t, qperm, l, natural)
        yt = _s5_core(zt, toept, ws, wot, atab, segtab, dcol, bsz, l)
        my = _mlstm(h, mod_all, gain_all, w_in_b, m_conv_w, rows(m_conv_b), wq, wk, wv, wg, bg,
                    rows(m_norm_gain), rows(m_skip), l, natural)
        h = _out_stage(yt, gt, my, h, mod_all, cols(ssm_b_glu), cols(ssm_out_gain), fg,
                       wglut, wout, qperm_t, l, final, natural)

    return h.reshape(bsz, seq, d)
```

```python
import functools
import math

import numpy as np
import jax
import jax.numpy as jnp
from jax import lax
from jax.experimental import pallas as pl
from jax.experimental.pallas import tpu as pltpu

F32 = jnp.float32
BF16 = jnp.bfloat16

D_MODEL = 1024
SSM_GROUP = 16
SSM_GROUPS = D_MODEL // SSM_GROUP
SSM_STATE = 64
HEADS = 4
HEAD_DIM = D_MODEL // HEADS
CONV_WIDTH = 4
EPS = 1e-6

SUB = 16
SUBS_PER_SEG = 16
SEG_LEN = SUB * SUBS_PER_SEG
N_SEG = 8
SEQ = N_SEG * SEG_LEN
BLK_LANES = N_SEG * SUBS_PER_SEG
S5_ROW = SUB * SSM_GROUP
STATE_HALF = 128
STATE_LANES = 2 * STATE_HALF
S5_GROUPS_PER_STEP = 4
L_PAIR = 2
OUT_STREAMS = 4
M_AUG = HEAD_DIM + 128
VMEM_LIMIT = 56 * 1024 * 1024

_NT = (((1,), (1,)), ((), ()))
_TN = (((0,), (0,)), ((), ()))


def _dot(a, b):
    return jnp.dot(a, b, preferred_element_type=F32)


def _dot_nt(a, b):
    return lax.dot_general(a, b, _NT, preferred_element_type=F32)


def _dot_tn(a, b):
    return lax.dot_general(a, b, _TN, preferred_element_type=F32)


def _sigmoid(x):
    return jax.nn.sigmoid(x)


def _rms(x):
    return x * lax.rsqrt(jnp.mean(x * x, axis=-1, keepdims=True) + EPS)


def _modulated_norm(x, gain, mod):
    shift = mod[:, 0:D_MODEL]
    scale = mod[:, D_MODEL:2 * D_MODEL]
    return _rms(x) * (gain * (1.0 + scale)) + shift


def _split_hi_lo(v):
    hi = v.astype(BF16)
    lo = (v - hi.astype(F32)).astype(BF16)
    return hi, lo


def _dot_nt_f32(a, b):
    a_hi, a_lo = _split_hi_lo(a)
    b_hi, b_lo = _split_hi_lo(b)
    return _dot_nt(a_hi, b_hi) + _dot_nt(a_hi, b_lo) + _dot_nt(a_lo, b_hi)


def _cmul(a_re, a_im, b_re, b_im):
    return a_re * b_re - a_im * b_im, a_re * b_im + a_im * b_re


def _cmul_add(p_re, p_im, s_re, s_im, add_re, add_im):
    return (p_re * s_re - p_im * s_im + add_re,
            p_re * s_im + p_im * s_re + add_im)


def _mod_kernel(c_ref, w_ref, b_ref, o_ref):
    cv = c_ref[...]
    act = cv * _sigmoid(cv)
    o_ref[...] = _dot(act.astype(BF16), w_ref[...].astype(BF16)) + b_ref[...]


def _modulation(c, w_mod, b_mod):
    depth, d, n = w_mod.shape
    bsz = c.shape[0]
    return pl.pallas_call(
        _mod_kernel,
        out_shape=jax.ShapeDtypeStruct((depth, bsz, n), F32),
        grid=(depth, n // d),
        in_specs=[
            pl.BlockSpec((bsz, d), lambda l, j: (0, 0)),
            pl.BlockSpec((None, d, d), lambda l, j: (l, 0, j)),
            pl.BlockSpec((None, 1, d), lambda l, j: (l, 0, j)),
        ],
        out_specs=pl.BlockSpec((None, bsz, d), lambda l, j: (l, 0, j)),
        compiler_params=pltpu.CompilerParams(
            dimension_semantics=("parallel", "parallel"),
            vmem_limit_bytes=VMEM_LIMIT),
        name="adaln_mod",
    )(c, w_mod, b_mod.reshape(depth, 1, n))


def _s5_prep_kernel(lre_ref, lim_ref, ldt_ref, btre_ref, btim_ref, cre_ref, cim_ref,
                    toept_ref, ws_ref, wot_ref, atab_ref, seg_ref):
    gp = S5_GROUPS_PER_STEP
    lane_blk = lax.broadcasted_iota(jnp.int32, (S5_ROW, S5_ROW), 1) // SSM_GROUP
    row8 = lax.broadcasted_iota(jnp.int32, (8, STATE_HALF), 0)
    for j in range(gp):
        lr = lre_ref[j]
        li = lim_ref[j]
        dt = jnp.exp(ldt_ref[j])
        mag = jnp.exp(lr * dt)
        a_re = mag * jnp.cos(li * dt)
        a_im = mag * jnp.sin(li * dt)
        inv = 1.0 / (lr * lr + li * li)
        k_re = ((a_re - 1.0) * lr + a_im * li) * inv
        k_im = (a_im * lr - (a_re - 1.0) * li) * inv
        bb_re, bb_im = _cmul(k_re, k_im, btre_ref[j], btim_ref[j])
        c_re = cre_ref[j]
        c_im = cim_ref[j]

        pw = [(jnp.ones_like(a_re), jnp.zeros_like(a_re))]
        for _ in range(SUB):
            pw.append(_cmul(pw[-1][0], pw[-1][1], a_re, a_im))

        ws_rows, wot_rows, ca_re, ca_im = [], [], [], []
        for l in range(SUB):
            w_re, w_im = _cmul(pw[SUB - 1 - l][0], pw[SUB - 1 - l][1], bb_re, bb_im)
            ws_rows.append(jnp.concatenate([w_re, w_im], axis=-1))
            o_re, o_im = _cmul(c_re, c_im, pw[l + 1][0], pw[l + 1][1])
            wot_rows.append(jnp.concatenate([o_re, -o_im], axis=-1))
            g_re, g_im = _cmul(c_re, c_im, pw[l][0], pw[l][1])
            ca_re.append(g_re)
            ca_im.append(g_im)
        ws_ref[j] = jnp.concatenate(ws_rows, axis=0).astype(BF16)
        wot_ref[j] = jnp.concatenate(wot_rows, axis=0).astype(BF16)

        bbt_re = jnp.concatenate([bb_re] * SUB, axis=0)
        bbt_im = jnp.concatenate([bb_im] * SUB, axis=0)
        kw = (_dot_nt_f32(jnp.concatenate(ca_re, axis=0), bbt_re)
              - _dot_nt_f32(jnp.concatenate(ca_im, axis=0), bbt_im))
        toep = jnp.where(lane_blk == 0, kw, 0.0)
        for lp in range(1, SUB):
            shifted = jnp.concatenate(
                [jnp.zeros((SSM_GROUP * lp, S5_ROW), F32), kw[0:S5_ROW - SSM_GROUP * lp]], axis=0)
            toep = jnp.where(lane_blk == lp, shifted, toep)
        toept_ref[j] = toep.astype(BF16)

        s_re, s_im = pw[SUB]
        atab_ref[j] = jnp.concatenate([jnp.broadcast_to(s_re, (8, STATE_HALF)),
                                       jnp.broadcast_to(s_im, (8, STATE_HALF))], axis=0)
        for _ in range(4):
            s_re, s_im = _cmul(s_re, s_im, s_re, s_im)
        seg = jnp.zeros((8, STATE_HALF), F32)
        for i in range(3):
            seg = jnp.where(row8 == i, s_re, jnp.where(row8 == 3 + i, s_im, seg))
            s_re, s_im = _cmul(s_re, s_im, s_re, s_im)
        seg_ref[j] = seg


def _s5_prep(lam_re, lam_im, log_dt, b_re, b_im, c_re, c_im):
    depth, g, p = lam_re.shape
    cg = SSM_GROUP
    n = depth * g
    gp = S5_GROUPS_PER_STEP
    lane_pad = ((0, 0), (0, 0), (0, STATE_HALF - p))

    def state_rows(v, fill):
        v = v.reshape(n, -1, p)
        return jnp.pad(v, lane_pad, constant_values=fill)

    args = (state_rows(lam_re, -1.0), state_rows(lam_im, 0.0), log_dt.reshape(n, 1, 1),
            state_rows(b_re.transpose(0, 1, 3, 2), 0.0), state_rows(b_im.transpose(0, 1, 3, 2), 0.0),
            state_rows(c_re, 0.0), state_rows(c_im, 0.0))
    vec = pl.BlockSpec((gp, 1, STATE_HALF), lambda i: (i, 0, 0))
    mat = pl.BlockSpec((gp, cg, STATE_HALF), lambda i: (i, 0, 0))
    big = pl.BlockSpec((gp, S5_ROW, STATE_LANES), lambda i: (i, 0, 0))
    return pl.pallas_call(
        _s5_prep_kernel,
        out_shape=(jax.ShapeDtypeStruct((n, S5_ROW, S5_ROW), BF16),
                   jax.ShapeDtypeStruct((n, S5_ROW, STATE_LANES), BF16),
                   jax.ShapeDtypeStruct((n, S5_ROW, STATE_LANES), BF16),
                   jax.ShapeDtypeStruct((n, 16, STATE_HALF), F32),
                   jax.ShapeDtypeStruct((n, 8, STATE_HALF), F32)),
        grid=(n // gp,),
        in_specs=[vec, vec, pl.BlockSpec((gp, 1, 1), lambda i: (i, 0, 0)), mat, mat, mat, mat],
        out_specs=(big, big, big,
                   pl.BlockSpec((gp, 16, STATE_HALF), lambda i: (i, 0, 0)),
                   pl.BlockSpec((gp, 8, STATE_HALF), lambda i: (i, 0, 0))),
        compiler_params=pltpu.CompilerParams(
            dimension_semantics=("parallel",),
            vmem_limit_bytes=VMEM_LIMIT),
        name="s5_prep",
    )(*args)


def _stream_rows(x_ref, natural):
    n_l = OUT_STREAMS * L_PAIR
    rows = L_PAIR * BLK_LANES
    if natural:
        by_l = jnp.swapaxes(x_ref[...].reshape(BLK_LANES, n_l, D_MODEL), 0, 1)
        return lambda s: by_l[s * L_PAIR:(s + 1) * L_PAIR].reshape(rows, D_MODEL)
    return lambda s: x_ref[s * L_PAIR:(s + 1) * L_PAIR].reshape(rows, D_MODEL)


def _s5_in_kernel(x_ref, mod_ref, gain_ref, wut_ref, wgt_ref, q_ref, zt_ref, gt_ref, *,
                  natural):
    d = D_MODEL
    x_rows = _stream_rows(x_ref, natural)

    def stage_norm(s):
        return _modulated_norm(x_rows(s), gain_ref[...], mod_ref[...]).astype(BF16)

    def stage_proj(s, h):
        ut = _dot_nt(wut_ref[...], h).astype(BF16)
        sgt = _dot_nt(wgt_ref[...], h)
        p = _dot(ut, q_ref[...]).astype(BF16)
        for l in range(L_PAIR):
            lanes = slice(l * BLK_LANES, (l + 1) * BLK_LANES)
            row0 = (s * L_PAIR + l) * SSM_GROUP
            zt_ref[:, row0:row0 + SSM_GROUP, :] = (
                p[:, lanes].reshape(SSM_GROUPS, SSM_GROUP, BLK_LANES))
            g = sgt[:, lanes]
            gt_ref[s * L_PAIR + l] = (g * _sigmoid(g)).astype(BF16)

    hs = [stage_norm(s) for s in range(OUT_STREAMS)]
    for s in range(OUT_STREAMS):
        stage_proj(s, hs[s])


def _token_spec(natural):
    n_l = OUT_STREAMS * L_PAIR
    if natural:
        assert n_l == 8
        return pl.BlockSpec((None, N_SEG, SUBS_PER_SEG, n_l, D_MODEL), lambda b, p: (b, 0, 0, p, 0))
    return pl.BlockSpec((None, n_l, N_SEG, SUBS_PER_SEG, D_MODEL), lambda b, p: (b, p, 0, 0, 0))


def _s5_in(x, mod_all, gain_all, w_ssm_t, qperm, layer, natural):
    bsz = x.shape[0]
    d = D_MODEL
    n_l = OUT_STREAMS * L_PAIR
    return pl.pallas_call(
        functools.partial(_s5_in_kernel, natural=natural),
        out_shape=(jax.ShapeDtypeStruct((SSM_GROUPS, S5_ROW, bsz * BLK_LANES), BF16),
                   jax.ShapeDtypeStruct((bsz, SUB, d, BLK_LANES), BF16)),
        grid=(bsz, SUB // n_l),
        in_specs=[
            _token_spec(natural),
            pl.BlockSpec((None, None, 1, 3 * d), lambda b, p: (layer, b, 0, 0)),
            pl.BlockSpec((None, 1, d), lambda b, p: (layer, 0, 0)),
            pl.BlockSpec((None, d, d), lambda b, p: (layer, 0, 0)),
            pl.BlockSpec((None, d, d), lambda b, p: (layer, 1, 0)),
            pl.BlockSpec((L_PAIR * BLK_LANES, L_PAIR * BLK_LANES), lambda b, p: (0, 0)),
        ],
        out_specs=(
            pl.BlockSpec((SSM_GROUPS, n_l * SSM_GROUP, BLK_LANES), lambda b, p: (0, p, b)),
            pl.BlockSpec((None, n_l, d, BLK_LANES), lambda b, p: (b, p, 0, 0)),
        ),
        compiler_params=pltpu.CompilerParams(
            dimension_semantics=("parallel", "parallel"),
            vmem_limit_bytes=VMEM_LIMIT),
        name="s5_in",
    )(x, mod_all, gain_all, w_ssm_t, w_ssm_t, qperm)


def _s5_core_kernel(zt_ref, toept_ref, ws_ref, wot_ref, atab_ref, seg_ref, dcol_ref,
                    yt_ref, loc_ref, sprev_ref, *, bsz):
    gs = S5_GROUPS_PER_STEP
    half = STATE_HALF
    row8 = lax.broadcasted_iota(jnp.int32, (N_SEG, half), 0)

    def shift_down(v, n):
        return jnp.where(row8 >= n, pltpu.roll(v, n, axis=0), 0.0)

    for j in range(gs):
        loc_ref[j] = _dot_tn(zt_ref[j], ws_ref[j])
        a_re = atab_ref[j, 0:8, :]
        a_im = atab_ref[j, 8:16, :]
        for b in range(bsz):
            base = b * BLK_LANES

            def rows(kl, base=base):
                return pl.ds(base + kl * N_SEG, N_SEG)

            e_re = jnp.zeros((N_SEG, half), F32)
            e_im = jnp.zeros((N_SEG, half), F32)
            for kl in range(SUBS_PER_SEG):
                e_re, e_im = _cmul_add(a_re, a_im, e_re, e_im,
                                       loc_ref[j, rows(kl), 0:half],
                                       loc_ref[j, rows(kl), half:STATE_LANES])
            for i, n in enumerate((1, 2, 4)):
                p_re = seg_ref[j, i:i + 1, :]
                p_im = seg_ref[j, 3 + i:4 + i, :]
                e_re, e_im = _cmul_add(p_re, p_im, shift_down(e_re, n), shift_down(e_im, n),
                                       e_re, e_im)
            s_re = shift_down(e_re, 1)
            s_im = shift_down(e_im, 1)
            for kl in range(SUBS_PER_SEG):
                sprev_ref[j, rows(kl), 0:half] = s_re
                sprev_ref[j, rows(kl), half:STATE_LANES] = s_im
                s_re, s_im = _cmul_add(a_re, a_im, s_re, s_im,
                                       loc_ref[j, rows(kl), 0:half],
                                       loc_ref[j, rows(kl), half:STATE_LANES])

        zt = zt_ref[j]
        out = (_dot(toept_ref[j], zt)
               + _dot_nt(wot_ref[j], sprev_ref[j].astype(BF16))
               + dcol_ref[j] * zt.astype(F32))
        yt_ref[j] = out.astype(BF16)


def _s5_core(zt, toept, ws, wot, atab, segtab, dcol, bsz, layer):
    g, rows, lanes = zt.shape
    gs = S5_GROUPS_PER_STEP
    off = layer * (g // gs)
    wspec = pl.BlockSpec((gs, S5_ROW, STATE_LANES), lambda i: (off + i, 0, 0))
    return pl.pallas_call(
        functools.partial(_s5_core_kernel, bsz=bsz),
        out_shape=jax.ShapeDtypeStruct((g, rows, lanes), BF16),
        grid=(g // gs,),
        in_specs=[
            pl.BlockSpec((gs, rows, lanes), lambda i: (i, 0, 0)),
            wspec, wspec, wspec,
            pl.BlockSpec((gs, 16, STATE_HALF), lambda i: (off + i, 0, 0)),
            pl.BlockSpec((gs, 8, STATE_HALF), lambda i: (off + i, 0, 0)),
            pl.BlockSpec((gs, S5_ROW, 1), lambda i: (off + i, 0, 0)),
        ],
        out_specs=pl.BlockSpec((gs, rows, lanes), lambda i: (i, 0, 0)),
        scratch_shapes=[pltpu.VMEM((gs, lanes, STATE_LANES), F32),
                        pltpu.VMEM((gs, lanes, STATE_LANES), F32)],
        compiler_params=pltpu.CompilerParams(
            dimension_semantics=("parallel",),
            vmem_limit_bytes=VMEM_LIMIT),
        name="s5_core",
    )(zt, toept, ws, wot, atab, segtab, dcol)


def _log_sigmoid(x):
    return -(jnp.maximum(-x, 0.0) + jnp.log1p(jnp.exp(-jnp.abs(x))))


def _chunk_time(idx):
    return ((idx & (SUBS_PER_SEG - 1)) * SUB) | (idx >> 4)


M_CHAINS = 4


def _mlstm_kernel(x_ref, mod_ref, gain_ref, w_min_ref, w_mo_ref, w_mg_ref, cw_ref, cb_ref,
                   wq_ref, wk_ref, wv_ref, wg_ref, bg_ref, ng_ref, skip_ref, o_ref,
                   ct_ref, m_ref, tail_ref, ext_ref, qkv_ref, proj_ref, xc_ref, hn_ref, cvb_ref,
                   *, natural):
    L, dh, d = SEG_LEN, HEAD_DIM, D_MODEL
    halo = (CONV_WIDTH - 1) * SUBS_PER_SEG

    @pl.when(pl.program_id(1) == 0)
    def _():
        ct_ref[...] = jnp.zeros_like(ct_ref)
        m_ref[...] = jnp.zeros_like(m_ref)
        tail_ref[...] = jnp.zeros_like(tail_ref)

    t_row = _chunk_time(lax.broadcasted_iota(jnp.int32, (L, L), 0))
    t_col = _chunk_time(lax.broadcasted_iota(jnp.int32, (L, L), 1))
    causal = t_col <= t_row
    tri_l = jnp.where(causal, 1.0, 0.0).astype(BF16)
    lane = lax.broadcasted_iota(jnp.int32, (L, 128), 1)
    ones_blk = jnp.where(lane == 0, 1.0, 0.0).astype(BF16)
    row16 = lax.broadcasted_iota(jnp.int32, (SUBS_PER_SEG, d), 0)

    def stage_norm_in(c):
        x = x_ref[c]
        if natural:
            x = jnp.swapaxes(x, 0, 1)
        hn_ref[c] = _modulated_norm(x.reshape(L, d), gain_ref[...], mod_ref[c]).astype(BF16)
        ext_ref[c, halo:halo + L, :] = _dot(hn_ref[c], w_min_ref[...])

    def stage_gate_proj(c):
        proj_ref[c, :, 0:d] = _dot(hn_ref[c], w_mo_ref[...]).astype(BF16)
        proj_ref[c, :, d:2 * d] = _dot(hn_ref[c], w_mg_ref[...]).astype(BF16)

    def stage_conv(c):
        m_in = ext_ref[c, halo:halo + L, :]
        cvb_ref[c, :, d:2 * d] = m_in.astype(BF16)
        for i in range(CONV_WIDTH - 1):
            lsrc = SUB - (CONV_WIDTH - 1) + i
            r0 = halo + lsrc * SUBS_PER_SEG
            shifted = ext_ref[c, pl.ds(r0 - 1, SUBS_PER_SEG), :]
            ext_ref[c, i * SUBS_PER_SEG:(i + 1) * SUBS_PER_SEG, :] = jnp.where(
                row16 == 0, tail_ref[c, i:i + 1, :], shifted)
            tail_ref[c, i:i + 1, :] = ext_ref[c, r0 + SUBS_PER_SEG - 1:r0 + SUBS_PER_SEG, :]
        acc = m_in * cw_ref[CONV_WIDTH - 1:CONV_WIDTH, :] + cb_ref[...]
        for j in range(CONV_WIDTH - 1):
            back = CONV_WIDTH - 1 - j
            acc = acc + ext_ref[c, pl.ds(halo - back * SUBS_PER_SEG, L), :] * cw_ref[j:j + 1, :]
        xc = acc * _sigmoid(acc)
        xc_ref[c] = xc
        cvb_ref[c, :, 0:d] = xc.astype(BF16)

    def stage_qkv(c):
        for hd in range(HEADS):
            sl = slice(hd * dh, (hd + 1) * dh)
            xcb = cvb_ref[c, :, hd * dh:(hd + 1) * dh]
            qkv_ref[c, :, hd * dh:(hd + 1) * dh] = _dot(xcb, wq_ref[hd]).astype(BF16)
            qkv_ref[c, :, d + hd * dh:d + (hd + 1) * dh] = (
                _dot(xcb, wk_ref[hd]) * (dh ** -0.5)).astype(BF16)
            qkv_ref[c, :, 2 * d + hd * dh:2 * d + (hd + 1) * dh] = (
                _dot(cvb_ref[c, :, d + hd * dh:d + (hd + 1) * dh], wv_ref[hd]).astype(BF16))
        gates = _dot(qkv_ref[c], wg_ref[...]) + bg_ref[...]
        log_f = _log_sigmoid(gates)
        lf_hi, lf_lo = _split_hi_lo(log_f)
        b_cols = _dot(tri_l, lf_hi) + _dot(tri_l, lf_lo)
        return gates.T, b_cols.T, gates, b_cols

    def stage_head(c, hd, gate_forms):
        gates_t, b_rows, gates, b_cols = gate_forms
        sl = slice(hd * dh, (hd + 1) * dh)
        i_r = gates_t[hd:hd + 1, :]
        b_r = b_rows[HEADS + hd:HEADS + hd + 1, :]
        i_c = gates[:, hd:hd + 1]
        b_c = b_cols[:, HEADS + hd:HEADS + hd + 1]
        m_prev = m_ref[c, hd:hd + 1, 0:1]

        log_d = jnp.where(causal, b_c + (i_r - b_r), -jnp.inf)
        m_inter = b_c + m_prev
        m_t = jnp.maximum(m_inter, jnp.max(log_d, axis=-1, keepdims=True))
        dmat = jnp.exp(log_d - m_t)
        qb = qkv_ref[c, :, sl]
        kb = qkv_ref[c, :, d + hd * dh:d + (hd + 1) * dh]
        s = _dot_nt(qb, kb)
        p = (s * dmat).astype(BF16)
        v_aug = jnp.concatenate([qkv_ref[c, :, 2 * d + hd * dh:2 * d + (hd + 1) * dh], ones_blk],
                                axis=-1)
        w_inter = jnp.exp(m_inter - m_t)
        inter = _dot(qb, ct_ref[c, hd].astype(BF16))
        intra = _dot(p, v_aug)
        nd = w_inter * inter + intra
        num = nd[:, 0:dh]
        den = nd[:, dh:dh + 1]
        hh = num / jnp.maximum(jnp.abs(den), jnp.exp(-m_t))

        b_tot = b_c[L - 1:L, :]
        log_w = b_tot - b_c + i_c
        m_next = jnp.maximum(b_tot + m_prev, jnp.max(log_w, axis=0, keepdims=True))
        decay = jnp.exp(b_tot + m_prev - m_next)
        w_c = jnp.exp(log_w - m_next)
        upd = _dot(kb.T, (w_c * v_aug.astype(F32)).astype(BF16))
        ct_ref[c, hd] = decay * ct_ref[c, hd] + upd
        m_ref[c, hd:hd + 1, :] = jnp.broadcast_to(m_next, (1, 128))

        hg = hh * _sigmoid(proj_ref[c, :, sl].astype(F32))
        mu = jnp.mean(hg, axis=-1, keepdims=True)
        dev = hg - mu
        var = jnp.mean(dev * dev, axis=-1, keepdims=True)
        hn = dev * lax.rsqrt(var + EPS) * ng_ref[:, sl] + skip_ref[:, sl] * xc_ref[c, :, sl]
        gate = proj_ref[c, :, d + hd * dh:d + (hd + 1) * dh].astype(F32)
        y = hn * (gate * _sigmoid(gate))
        o_ref[c, :, :, sl] = y.reshape(SUB, SUBS_PER_SEG, dh).astype(o_ref.dtype)

    chains = range(M_CHAINS)
    for c in chains:
        stage_norm_in(c)
    for c in chains:
        stage_conv(c)
    for c in chains:
        stage_gate_proj(c)
    gfs = [stage_qkv(c) for c in chains]
    for hd in range(HEADS):
        for c in chains:
            stage_head(c, hd, gfs[c])


def _mlstm(x, mod_all, gain_all, w_in_b, cw, cb, wq, wk, wv, wg, bg, ng, skip, layer, natural):
    bsz = x.shape[0]
    d = D_MODEL
    L = SEG_LEN
    nc = M_CHAINS
    once = pl.Buffered(1)
    lay2 = lambda shape: pl.BlockSpec((None,) + shape, lambda b, c: (layer, 0, 0),
                                      pipeline_mode=once)
    lay3 = lambda shape: pl.BlockSpec((None,) + shape, lambda b, c: (layer, 0, 0, 0),
                                      pipeline_mode=once)
    seg_spec = pl.BlockSpec((nc, SUB, None, SUBS_PER_SEG, d), lambda b, c: (b, 0, c, 0, 0))
    x_spec = (pl.BlockSpec((nc, None, SUBS_PER_SEG, SUB, d), lambda b, c: (b, c, 0, 0, 0))
              if natural else seg_spec)
    w_cols = lambda j: pl.BlockSpec((None, d, d), lambda b, c: (layer, 0, j), pipeline_mode=once)
    halo = (CONV_WIDTH - 1) * SUBS_PER_SEG
    return pl.pallas_call(
        functools.partial(_mlstm_kernel, natural=natural),
        out_shape=jax.ShapeDtypeStruct((bsz, SUB, N_SEG, SUBS_PER_SEG, d), BF16),
        grid=(bsz // nc, N_SEG),
        in_specs=[
            x_spec,
            pl.BlockSpec((None, nc, 1, 3 * d), lambda b, c: (layer, b, 0, 0)),
            lay2((1, d)),
            w_cols(2), w_cols(3), w_cols(4),
            lay2((CONV_WIDTH, d)),
            lay2((1, d)),
            lay3((HEADS, HEAD_DIM, HEAD_DIM)),
            lay3((HEADS, HEAD_DIM, HEAD_DIM)),
            lay3((HEADS, HEAD_DIM, HEAD_DIM)),
            lay2((3 * d, 128)),
            lay2((1, 128)),
            lay2((1, d)),
            lay2((1, d)),
        ],
        out_specs=seg_spec,
        scratch_shapes=[
            pltpu.VMEM((nc, HEADS, HEAD_DIM, M_AUG), F32),
            pltpu.VMEM((nc, 8, 128), F32),
            pltpu.VMEM((nc, 8, d), F32),
            pltpu.VMEM((nc, halo + L, d), F32),
            pltpu.VMEM((nc, L, 3 * d), BF16),
            pltpu.VMEM((nc, L, 2 * d), BF16),
            pltpu.VMEM((nc, L, d), F32),
            pltpu.VMEM((nc, L, d), BF16),
            pltpu.VMEM((nc, L, 2 * d), BF16),
        ],
        compiler_params=pltpu.CompilerParams(
            dimension_semantics=("parallel", "arbitrary"),
            vmem_limit_bytes=60 * 1024 * 1024),
        name="mlstm",
    )(x, mod_all, gain_all, w_in_b, w_in_b, w_in_b, cw, cb, wq, wk, wv, wg, bg, ng, skip)


def _gelu_tanh(x):
    inner = math.sqrt(2.0 / math.pi) * (x + 0.044715 * (x * x * x))
    return x * (0.5 * (1.0 + jnp.tanh(inner)))


def _out_kernel(yt_ref, gt_ref, my_ref, x_ref, mod_ref, bglu_ref, og_ref, fg_ref,
                wglut_ref, wout_ref, qt_ref, o_ref, *, final, natural):
    d = D_MODEL
    rows = L_PAIR * BLK_LANES
    x_rows = _stream_rows(x_ref, natural)

    def stage_in(s):
        ls = range(s * L_PAIR, (s + 1) * L_PAIR)
        out_m = _dot(my_ref[s * L_PAIR:(s + 1) * L_PAIR].reshape(rows, d),
                     wout_ref[d:2 * d, :])
        yt = jnp.concatenate(
            [yt_ref[:, l * SSM_GROUP:(l + 1) * SSM_GROUP, :].reshape(d, BLK_LANES) for l in ls],
            axis=-1)
        y = _gelu_tanh(_dot(yt, qt_ref[...]))
        return out_m, y

    def stage_glu(s, y):
        ls = range(s * L_PAIR, (s + 1) * L_PAIR)
        glu = y * _sigmoid(_dot(wglut_ref[...], y.astype(BF16)) + bglu_ref[...])
        gate_s = jnp.concatenate([gt_ref[l] for l in ls], axis=-1).astype(F32)
        ms = jnp.mean(glu * glu, axis=0, keepdims=True)
        return (glu * lax.rsqrt(ms + EPS) * og_ref[...] * gate_s).astype(BF16)

    def stage_out(s, out_m, ssm_y):
        out = _dot_tn(ssm_y, wout_ref[0:d, :]) + out_m
        xn = x_rows(s) + mod_ref[:, 2 * d:3 * d] * out
        if final:
            return _rms(xn) * fg_ref[...]
        o_ref[s * L_PAIR:(s + 1) * L_PAIR] = xn.reshape(L_PAIR, N_SEG, SUBS_PER_SEG, d)

    streams = range(OUT_STREAMS)
    ins = [stage_in(s) for s in streams]
    ssm = [stage_glu(s, ins[s][1]) for s in streams]
    outs = [stage_out(s, ins[s][0], ssm[s]) for s in streams]
    if final:
        n_l = OUT_STREAMS * L_PAIR
        by_l = jnp.concatenate(outs, axis=0).reshape(n_l, BLK_LANES, d)
        o_ref[...] = jnp.swapaxes(by_l, 0, 1).reshape(N_SEG, SUBS_PER_SEG, n_l, d)


def _out_stage(yt, gt, my, x, mod_all, bglu_col, og_col, fg, wglut, wout, qperm_t, layer, final,
               natural):
    bsz = x.shape[0]
    d = D_MODEL
    n_l = OUT_STREAMS * L_PAIR
    col = pl.BlockSpec((None, d, 1), lambda b, p: (layer, 0, 0))
    out_dims = (bsz, N_SEG, SUBS_PER_SEG, SUB, d) if final else (bsz, SUB, N_SEG, SUBS_PER_SEG, d)
    return pl.pallas_call(
        functools.partial(_out_kernel, final=final, natural=natural),
        out_shape=jax.ShapeDtypeStruct(out_dims, F32),
        grid=(bsz, SUB // n_l),
        in_specs=[
            pl.BlockSpec((SSM_GROUPS, n_l * SSM_GROUP, BLK_LANES), lambda b, p: (0, p, b)),
            pl.BlockSpec((None, n_l, d, BLK_LANES), lambda b, p: (b, p, 0, 0)),
            _token_spec(False), _token_spec(natural),
            pl.BlockSpec((None, None, 1, 3 * d), lambda b, p: (layer, b, 0, 0)),
            col, col,
            pl.BlockSpec((1, d), lambda b, p: (0, 0)),
            pl.BlockSpec((None, d, d), lambda b, p: (layer, 0, 0)),
            pl.BlockSpec((None, 2 * d, d), lambda b, p: (layer, 0, 0)),
            pl.BlockSpec((L_PAIR * BLK_LANES, L_PAIR * BLK_LANES), lambda b, p: (0, 0)),
        ],
        out_specs=_token_spec(final),
        compiler_params=pltpu.CompilerParams(
            dimension_semantics=("parallel", "parallel"),
            vmem_limit_bytes=VMEM_LIMIT),
        name="out_stage",
    )(yt, gt, my, x, mod_all, bglu_col, og_col, fg, wglut, wout, qperm_t)


def _lane_permutation():
    q = np.zeros((BLK_LANES, BLK_LANES), np.float32)
    for seg in range(N_SEG):
        for kl in range(SUBS_PER_SEG):
            q[seg * SUBS_PER_SEG + kl, kl * N_SEG + seg] = 1.0
    return np.kron(np.eye(L_PAIR, dtype=np.float32), q)


def kernel(x, c, norm_gain, w_mod, b_mod, w_in, ssm_lambda_re, ssm_lambda_im, ssm_log_dt,
           ssm_b_re, ssm_b_im, ssm_c_re, ssm_c_im, ssm_d, ssm_w_glu, ssm_b_glu,
           ssm_out_gain, m_conv_w, m_conv_b, m_wq, m_wk, m_wv, m_w_gates, m_b_igate,
           m_b_fgate, m_norm_gain, m_skip, w_out, final_gain):
    bsz, seq, d = x.shape
    depth = w_in.shape[0]
    assert d == D_MODEL and seq == SEQ

    h = x.reshape(bsz, N_SEG, SUBS_PER_SEG, SUB, d)
    mod_all = _modulation(c, w_mod, b_mod).reshape(depth, bsz, 1, 3 * d)
    rows = lambda v: v.reshape(depth, 1, -1)
    cols = lambda v: v.reshape(depth, -1, 1)
    qperm = jnp.asarray(_lane_permutation(), BF16)
    qperm_t = jnp.asarray(_lane_permutation().T, BF16)

    gain_all = rows(norm_gain)
    w_in_b = w_in.astype(BF16)
    w_ssm_t = w_in_b[:, :, 0:2 * d].transpose(0, 2, 1)
    toept, ws, wot, atab, segtab = _s5_prep(ssm_lambda_re, ssm_lambda_im, ssm_log_dt,
                                            ssm_b_re, ssm_b_im, ssm_c_re, ssm_c_im)
    dcol = jnp.broadcast_to(ssm_d.reshape(depth * SSM_GROUPS, 1, SSM_GROUP),
                            (depth * SSM_GROUPS, SUB, SSM_GROUP)).reshape(-1, S5_ROW, 1)
    wq, wk, wv = m_wq.astype(BF16), m_wk.astype(BF16), m_wv.astype(BF16)
    gate_pad = ((0, 0), (0, 0), (0, 128 - 2 * HEADS))
    wg = jnp.pad(m_w_gates, gate_pad).astype(BF16)
    bg = jnp.pad(jnp.concatenate([m_b_igate, m_b_fgate], axis=-1).reshape(depth, 1, 2 * HEADS),
                 gate_pad)
    wglut = ssm_w_glu.astype(BF16).transpose(0, 2, 1)
    wout = w_out.astype(BF16)
    fg = final_gain.reshape(1, d)

    for l in range(depth):
        natural = l == 0
        final = l == depth - 1
        zt, gt = _s5_in(h, mod_all, gain_all, w_ssm_t, qperm, l, natural)
        yt = _s5_core(zt, toept, ws, wot, atab, segtab, dcol, bsz, l)
        my = _mlstm(h, mod_all, gain_all, w_in_b, m_conv_w, rows(m_conv_b), wq, wk, wv, wg, bg,
                    rows(m_norm_gain), rows(m_skip), l, natural)
        h = _out_stage(yt, gt, my, h, mod_all, cols(ssm_b_glu), cols(ssm_out_gain), fg,
                       wglut, wout, qperm_t, l, final, natural)

    return h.reshape(bsz, seq, d)
```

```python
import functools
import math

import numpy as np
import jax
import jax.numpy as jnp
from jax import lax
from jax.experimental import pallas as pl
from jax.experimental.pallas import tpu as pltpu

F32 = jnp.float32
BF16 = jnp.bfloat16

D_MODEL = 1024
SSM_GROUP = 16
SSM_GROUPS = D_MODEL // SSM_GROUP
SSM_STATE = 64
HEADS = 4
HEAD_DIM = D_MODEL // HEADS
CONV_WIDTH = 4
EPS = 1e-6

SUB = 16
SUBS_PER_SEG = 16
SEG_LEN = SUB * SUBS_PER_SEG
N_SEG = 8
SEQ = N_SEG * SEG_LEN
BLK_LANES = N_SEG * SUBS_PER_SEG
S5_ROW = SUB * SSM_GROUP
STATE_HALF = 128
STATE_LANES = 2 * STATE_HALF
S5_GROUPS_PER_STEP = 8
L_PAIR = 2
OUT_STREAMS = 4
M_AUG = HEAD_DIM + 128
VMEM_LIMIT = 56 * 1024 * 1024
MLSTM_VMEM_LIMIT = 60 * 1024 * 1024

_NT = (((1,), (1,)), ((), ()))
_TN = (((0,), (0,)), ((), ()))


def _dot(a, b):
    return jnp.dot(a, b, preferred_element_type=F32)


def _dot_nt(a, b):
    return lax.dot_general(a, b, _NT, preferred_element_type=F32)


def _dot_tn(a, b):
    return lax.dot_general(a, b, _TN, preferred_element_type=F32)


def _sigmoid(x):
    return jax.nn.sigmoid(x)


def _rms(x):
    return x * lax.rsqrt(jnp.mean(x * x, axis=-1, keepdims=True) + EPS)


def _modulated_norm(x, gain, mod):
    shift = mod[:, 0:D_MODEL]
    scale = mod[:, D_MODEL:2 * D_MODEL]
    return _rms(x) * (gain * (1.0 + scale)) + shift


def _split_hi_lo(v):
    hi = v.astype(BF16)
    lo = (v - hi.astype(F32)).astype(BF16)
    return hi, lo


def _dot_nt_f32(a, b):
    a_hi, a_lo = _split_hi_lo(a)
    b_hi, b_lo = _split_hi_lo(b)
    return _dot_nt(a_hi, b_hi) + _dot_nt(a_hi, b_lo) + _dot_nt(a_lo, b_hi)


def _cmul(a_re, a_im, b_re, b_im):
    return a_re * b_re - a_im * b_im, a_re * b_im + a_im * b_re


def _cmul_add(p_re, p_im, s_re, s_im, add_re, add_im):
    return (p_re * s_re - p_im * s_im + add_re,
            p_re * s_im + p_im * s_re + add_im)


def _mod_kernel(c_ref, w_ref, b_ref, o_ref):
    cv = c_ref[...]
    act = cv * _sigmoid(cv)
    o_ref[...] = _dot(act.astype(BF16), w_ref[...].astype(BF16)) + b_ref[...]


def _modulation(c, w_mod, b_mod):
    depth, d, n = w_mod.shape
    bsz = c.shape[0]
    return pl.pallas_call(
        _mod_kernel,
        out_shape=jax.ShapeDtypeStruct((depth, bsz, n), F32),
        grid=(depth, n // d),
        in_specs=[
            pl.BlockSpec((bsz, d), lambda l, j: (0, 0)),
            pl.BlockSpec((None, d, d), lambda l, j: (l, 0, j)),
            pl.BlockSpec((None, 1, d), lambda l, j: (l, 0, j)),
        ],
        out_specs=pl.BlockSpec((None, bsz, d), lambda l, j: (l, 0, j)),
        compiler_params=pltpu.CompilerParams(
            dimension_semantics=("parallel", "parallel"),
            vmem_limit_bytes=VMEM_LIMIT),
        name="adaln_mod",
    )(c, w_mod, b_mod.reshape(depth, 1, n))


def _s5_prep_kernel(lre_ref, lim_ref, ldt_ref, btre_ref, btim_ref, cre_ref, cim_ref,
                    toept_ref, ws_ref, wot_ref, atab_ref, seg_ref):
    gp = S5_GROUPS_PER_STEP
    lane_blk = lax.broadcasted_iota(jnp.int32, (S5_ROW, S5_ROW), 1) // SSM_GROUP
    row8 = lax.broadcasted_iota(jnp.int32, (8, STATE_HALF), 0)
    for j in range(gp):
        lr = lre_ref[j]
        li = lim_ref[j]
        dt = jnp.exp(ldt_ref[j])
        mag = jnp.exp(lr * dt)
        a_re = mag * jnp.cos(li * dt)
        a_im = mag * jnp.sin(li * dt)
        inv = 1.0 / (lr * lr + li * li)
        k_re = ((a_re - 1.0) * lr + a_im * li) * inv
        k_im = (a_im * lr - (a_re - 1.0) * li) * inv
        bb_re, bb_im = _cmul(k_re, k_im, btre_ref[j], btim_ref[j])
        c_re = cre_ref[j]
        c_im = cim_ref[j]

        pw = [(jnp.ones_like(a_re), jnp.zeros_like(a_re))]
        for _ in range(SUB):
            pw.append(_cmul(pw[-1][0], pw[-1][1], a_re, a_im))

        ws_rows, wot_rows, ca_re, ca_im = [], [], [], []
        for l in range(SUB):
            w_re, w_im = _cmul(pw[SUB - 1 - l][0], pw[SUB - 1 - l][1], bb_re, bb_im)
            ws_rows.append(jnp.concatenate([w_re, w_im], axis=-1))
            o_re, o_im = _cmul(c_re, c_im, pw[l + 1][0], pw[l + 1][1])
            wot_rows.append(jnp.concatenate([o_re, -o_im], axis=-1))
            g_re, g_im = _cmul(c_re, c_im, pw[l][0], pw[l][1])
            ca_re.append(g_re)
            ca_im.append(g_im)
        ws_ref[j] = jnp.concatenate(ws_rows, axis=0).astype(BF16)
        wot_ref[j] = jnp.concatenate(wot_rows, axis=0).astype(BF16)

        bbt_re = jnp.concatenate([bb_re] * SUB, axis=0)
        bbt_im = jnp.concatenate([bb_im] * SUB, axis=0)
        kw = (_dot_nt_f32(jnp.concatenate(ca_re, axis=0), bbt_re)
              - _dot_nt_f32(jnp.concatenate(ca_im, axis=0), bbt_im))
        toep = jnp.where(lane_blk == 0, kw, 0.0)
        for lp in range(1, SUB):
            shifted = jnp.concatenate(
                [jnp.zeros((SSM_GROUP * lp, S5_ROW), F32), kw[0:S5_ROW - SSM_GROUP * lp]], axis=0)
            toep = jnp.where(lane_blk == lp, shifted, toep)
        toept_ref[j] = toep.astype(BF16)

        s_re, s_im = pw[SUB]
        atab_ref[j] = jnp.concatenate([jnp.broadcast_to(s_re, (8, STATE_HALF)),
                                       jnp.broadcast_to(s_im, (8, STATE_HALF))], axis=0)
        for _ in range(4):
            s_re, s_im = _cmul(s_re, s_im, s_re, s_im)
        seg = jnp.zeros((8, STATE_HALF), F32)
        for i in range(3):
            seg = jnp.where(row8 == i, s_re, jnp.where(row8 == 3 + i, s_im, seg))
            s_re, s_im = _cmul(s_re, s_im, s_re, s_im)
        seg_ref[j] = seg


def _s5_prep(lam_re, lam_im, log_dt, b_re, b_im, c_re, c_im):
    depth, g, p = lam_re.shape
    cg = SSM_GROUP
    n = depth * g
    gp = S5_GROUPS_PER_STEP
    lane_pad = ((0, 0), (0, 0), (0, STATE_HALF - p))

    def state_rows(v, fill):
        v = v.reshape(n, -1, p)
        return jnp.pad(v, lane_pad, constant_values=fill)

    args = (state_rows(lam_re, -1.0), state_rows(lam_im, 0.0), log_dt.reshape(n, 1, 1),
            state_rows(b_re.transpose(0, 1, 3, 2), 0.0), state_rows(b_im.transpose(0, 1, 3, 2), 0.0),
            state_rows(c_re, 0.0), state_rows(c_im, 0.0))
    vec = pl.BlockSpec((gp, 1, STATE_HALF), lambda i: (i, 0, 0))
    mat = pl.BlockSpec((gp, cg, STATE_HALF), lambda i: (i, 0, 0))
    big = pl.BlockSpec((gp, S5_ROW, STATE_LANES), lambda i: (i, 0, 0))
    return pl.pallas_call(
        _s5_prep_kernel,
        out_shape=(jax.ShapeDtypeStruct((n, S5_ROW, S5_ROW), BF16),
                   jax.ShapeDtypeStruct((n, S5_ROW, STATE_LANES), BF16),
                   jax.ShapeDtypeStruct((n, S5_ROW, STATE_LANES), BF16),
                   jax.ShapeDtypeStruct((n, 16, STATE_HALF), F32),
                   jax.ShapeDtypeStruct((n, 8, STATE_HALF), F32)),
        grid=(n // gp,),
        in_specs=[vec, vec, pl.BlockSpec((gp, 1, 1), lambda i: (i, 0, 0)), mat, mat, mat, mat],
        out_specs=(big, big, big,
                   pl.BlockSpec((gp, 16, STATE_HALF), lambda i: (i, 0, 0)),
                   pl.BlockSpec((gp, 8, STATE_HALF), lambda i: (i, 0, 0))),
        compiler_params=pltpu.CompilerParams(
            dimension_semantics=("parallel",),
            vmem_limit_bytes=VMEM_LIMIT),
        name="s5_prep",
    )(*args)


def _stream_rows(x_ref, natural):
    n_l = OUT_STREAMS * L_PAIR
    rows = L_PAIR * BLK_LANES
    if natural:
        by_l = jnp.swapaxes(x_ref[...].reshape(BLK_LANES, n_l, D_MODEL), 0, 1)
        return lambda s: by_l[s * L_PAIR:(s + 1) * L_PAIR].reshape(rows, D_MODEL)
    return lambda s: x_ref[s * L_PAIR:(s + 1) * L_PAIR].reshape(rows, D_MODEL)


def _s5_in_kernel(x_ref, mod_ref, gain_ref, wut_ref, wgt_ref, q_ref, zt_ref, gt_ref, *,
                  natural):
    d = D_MODEL
    x_rows = _stream_rows(x_ref, natural)

    def stage_norm(s):
        return _modulated_norm(x_rows(s), gain_ref[...], mod_ref[...]).astype(BF16)

    def stage_proj(s, h):
        ut = _dot_nt(wut_ref[...], h).astype(BF16)
        sgt = _dot_nt(wgt_ref[...], h)
        p = _dot(ut, q_ref[...]).astype(BF16)
        for l in range(L_PAIR):
            lanes = slice(l * BLK_LANES, (l + 1) * BLK_LANES)
            row0 = (s * L_PAIR + l) * SSM_GROUP
            zt_ref[:, row0:row0 + SSM_GROUP, :] = (
                p[:, lanes].reshape(SSM_GROUPS, SSM_GROUP, BLK_LANES))
            g = sgt[:, lanes]
            gt_ref[s * L_PAIR + l] = (g * _sigmoid(g)).astype(BF16)

    hs = [stage_norm(s) for s in range(OUT_STREAMS)]
    for s in range(OUT_STREAMS):
        stage_proj(s, hs[s])


def _token_spec(natural):
    n_l = OUT_STREAMS * L_PAIR
    if natural:
        assert n_l == 8
        return pl.BlockSpec((None, N_SEG, SUBS_PER_SEG, n_l, D_MODEL), lambda b, p: (b, 0, 0, p, 0))
    return pl.BlockSpec((None, n_l, N_SEG, SUBS_PER_SEG, D_MODEL), lambda b, p: (b, p, 0, 0, 0))


def _s5_in(x, mod_all, gain_all, w_ssm_t, qperm, layer, natural):
    bsz = x.shape[0]
    d = D_MODEL
    n_l = OUT_STREAMS * L_PAIR
    return pl.pallas_call(
        functools.partial(_s5_in_kernel, natural=natural),
        out_shape=(jax.ShapeDtypeStruct((SSM_GROUPS, S5_ROW, bsz * BLK_LANES), BF16),
                   jax.ShapeDtypeStruct((bsz, SUB, d, BLK_LANES), BF16)),
        grid=(bsz, SUB // n_l),
        in_specs=[
            _token_spec(natural),
            pl.BlockSpec((None, None, 1, 3 * d), lambda b, p: (layer, b, 0, 0)),
            pl.BlockSpec((None, 1, d), lambda b, p: (layer, 0, 0)),
            pl.BlockSpec((None, d, d), lambda b, p: (layer, 0, 0)),
            pl.BlockSpec((None, d, d), lambda b, p: (layer, 1, 0)),
            pl.BlockSpec((L_PAIR * BLK_LANES, L_PAIR * BLK_LANES), lambda b, p: (0, 0)),
        ],
        out_specs=(
            pl.BlockSpec((SSM_GROUPS, n_l * SSM_GROUP, BLK_LANES), lambda b, p: (0, p, b)),
            pl.BlockSpec((None, n_l, d, BLK_LANES), lambda b, p: (b, p, 0, 0)),
        ),
        compiler_params=pltpu.CompilerParams(
            dimension_semantics=("parallel", "parallel"),
            vmem_limit_bytes=VMEM_LIMIT),
        name="s5_in",
    )(x, mod_all, gain_all, w_ssm_t, w_ssm_t, qperm)


def _s5_core_kernel(zt_ref, toept_ref, ws_ref, wot_ref, atab_ref, seg_ref, dcol_ref,
                    yt_ref, loc_ref, sprev_ref, *, bsz):
    gs = S5_GROUPS_PER_STEP
    half = STATE_HALF
    row8 = lax.broadcasted_iota(jnp.int32, (N_SEG, half), 0)

    def shift_down(v, n):
        return jnp.where(row8 >= n, pltpu.roll(v, n, axis=0), 0.0)

    for j in range(gs):
        loc_ref[j] = _dot_tn(zt_ref[j], ws_ref[j])

    for j in range(gs):
        a_re = atab_ref[j, 0:8, :]
        a_im = atab_ref[j, 8:16, :]
        for b in range(bsz):
            base = b * BLK_LANES

            def rows(kl, base=base):
                return pl.ds(base + kl * N_SEG, N_SEG)

            e_re = jnp.zeros((N_SEG, half), F32)
            e_im = jnp.zeros((N_SEG, half), F32)
            for kl in range(SUBS_PER_SEG):
                e_re, e_im = _cmul_add(a_re, a_im, e_re, e_im,
                                       loc_ref[j, rows(kl), 0:half],
                                       loc_ref[j, rows(kl), half:STATE_LANES])
            for i, n in enumerate((1, 2, 4)):
                p_re = seg_ref[j, i:i + 1, :]
                p_im = seg_ref[j, 3 + i:4 + i, :]
                e_re, e_im = _cmul_add(p_re, p_im, shift_down(e_re, n), shift_down(e_im, n),
                                       e_re, e_im)
            s_re = shift_down(e_re, 1)
            s_im = shift_down(e_im, 1)
            for kl in range(SUBS_PER_SEG):
                sprev_ref[j, rows(kl), 0:half] = s_re
                sprev_ref[j, rows(kl), half:STATE_LANES] = s_im
                s_re, s_im = _cmul_add(a_re, a_im, s_re, s_im,
                                       loc_ref[j, rows(kl), 0:half],
                                       loc_ref[j, rows(kl), half:STATE_LANES])

    for j in range(gs):
        zt = zt_ref[j]
        out = (_dot(toept_ref[j], zt)
               + _dot_nt(wot_ref[j], sprev_ref[j].astype(BF16))
               + dcol_ref[j] * zt.astype(F32))
        yt_ref[j] = out.astype(BF16)


def _s5_core(zt, toept, ws, wot, atab, segtab, dcol, bsz, layer):
    g, rows, lanes = zt.shape
    gs = S5_GROUPS_PER_STEP
    off = layer * (g // gs)
    wspec = pl.BlockSpec((gs, S5_ROW, STATE_LANES), lambda i: (off + i, 0, 0))
    return pl.pallas_call(
        functools.partial(_s5_core_kernel, bsz=bsz),
        out_shape=jax.ShapeDtypeStruct((g, rows, lanes), BF16),
        grid=(g // gs,),
        in_specs=[
            pl.BlockSpec((gs, rows, lanes), lambda i: (i, 0, 0)),
            wspec, wspec, wspec,
            pl.BlockSpec((gs, 16, STATE_HALF), lambda i: (off + i, 0, 0)),
            pl.BlockSpec((gs, 8, STATE_HALF), lambda i: (off + i, 0, 0)),
            pl.BlockSpec((gs, S5_ROW, 1), lambda i: (off + i, 0, 0)),
        ],
        out_specs=pl.BlockSpec((gs, rows, lanes), lambda i: (i, 0, 0)),
        scratch_shapes=[pltpu.VMEM((gs, lanes, STATE_LANES), F32),
                        pltpu.VMEM((gs, lanes, STATE_LANES), F32)],
        compiler_params=pltpu.CompilerParams(
            dimension_semantics=("parallel",),
            vmem_limit_bytes=VMEM_LIMIT),
        name="s5_core",
    )(zt, toept, ws, wot, atab, segtab, dcol)


def _log_sigmoid(x):
    return -(jnp.maximum(-x, 0.0) + jnp.log1p(jnp.exp(-jnp.abs(x))))


def _chunk_time(idx):
    return ((idx & (SUBS_PER_SEG - 1)) * SUB) | (idx >> 4)


M_CHAINS = 4


def _mlstm_kernel(x_ref, mod_ref, gain_ref, w_min_ref, w_mo_ref, w_mg_ref, cw_ref, cb_ref,
                   wq_ref, wk_ref, wv_ref, wg_ref, bg_ref, ng_ref, skip_ref, o_ref,
                   ct_ref, m_ref, tail_ref, ext_ref, qkv_ref, proj_ref, xc_ref, hn_ref, cvb_ref,
                   *, natural):
    L, dh, d = SEG_LEN, HEAD_DIM, D_MODEL
    halo = (CONV_WIDTH - 1) * SUBS_PER_SEG

    @pl.when(pl.program_id(1) == 0)
    def _():
        ct_ref[...] = jnp.zeros_like(ct_ref)
        m_ref[...] = jnp.zeros_like(m_ref)
        tail_ref[...] = jnp.zeros_like(tail_ref)

    t_row = _chunk_time(lax.broadcasted_iota(jnp.int32, (L, L), 0))
    t_col = _chunk_time(lax.broadcasted_iota(jnp.int32, (L, L), 1))
    causal = t_col <= t_row
    tri_l = jnp.where(causal, 1.0, 0.0).astype(BF16)
    lane = lax.broadcasted_iota(jnp.int32, (L, 128), 1)
    ones_blk = jnp.where(lane == 0, 1.0, 0.0).astype(BF16)
    row16 = lax.broadcasted_iota(jnp.int32, (SUBS_PER_SEG, d), 0)

    def stage_norm_in(c):
        x = x_ref[c]
        if natural:
            x = jnp.swapaxes(x, 0, 1)
        hn_ref[c] = _modulated_norm(x.reshape(L, d), gain_ref[...], mod_ref[c]).astype(BF16)
        ext_ref[c, halo:halo + L, :] = _dot(hn_ref[c], w_min_ref[...])

    def stage_gate_proj(c):
        proj_ref[c, :, 0:d] = _dot(hn_ref[c], w_mo_ref[...]).astype(BF16)
        proj_ref[c, :, d:2 * d] = _dot(hn_ref[c], w_mg_ref[...]).astype(BF16)

    def stage_conv(c):
        m_in = ext_ref[c, halo:halo + L, :]
        cvb_ref[c, :, d:2 * d] = m_in.astype(BF16)
        for i in range(CONV_WIDTH - 1):
            lsrc = SUB - (CONV_WIDTH - 1) + i
            r0 = halo + lsrc * SUBS_PER_SEG
            shifted = ext_ref[c, pl.ds(r0 - 1, SUBS_PER_SEG), :]
            ext_ref[c, i * SUBS_PER_SEG:(i + 1) * SUBS_PER_SEG, :] = jnp.where(
                row16 == 0, tail_ref[c, i:i + 1, :], shifted)
            tail_ref[c, i:i + 1, :] = ext_ref[c, r0 + SUBS_PER_SEG - 1:r0 + SUBS_PER_SEG, :]
        acc = m_in * cw_ref[CONV_WIDTH - 1:CONV_WIDTH, :] + cb_ref[...]
        for j in range(CONV_WIDTH - 1):
            back = CONV_WIDTH - 1 - j
            acc = acc + ext_ref[c, pl.ds(halo - back * SUBS_PER_SEG, L), :] * cw_ref[j:j + 1, :]
        xc = acc * _sigmoid(acc)
        xc_ref[c] = xc
        cvb_ref[c, :, 0:d] = xc.astype(BF16)

    def stage_qkv(c):
        for hd in range(HEADS):
            sl = slice(hd * dh, (hd + 1) * dh)
            xcb = cvb_ref[c, :, hd * dh:(hd + 1) * dh]
            qkv_ref[c, :, hd * dh:(hd + 1) * dh] = _dot(xcb, wq_ref[hd]).astype(BF16)
            qkv_ref[c, :, d + hd * dh:d + (hd + 1) * dh] = (
                _dot(xcb, wk_ref[hd]) * (dh ** -0.5)).astype(BF16)
            qkv_ref[c, :, 2 * d + hd * dh:2 * d + (hd + 1) * dh] = (
                _dot(cvb_ref[c, :, d + hd * dh:d + (hd + 1) * dh], wv_ref[hd]).astype(BF16))
        gates = _dot(qkv_ref[c], wg_ref[...]) + bg_ref[...]
        log_f = _log_sigmoid(gates)
        lf_hi, lf_lo = _split_hi_lo(log_f)
        b_cols = _dot(tri_l, lf_hi) + _dot(tri_l, lf_lo)
        return gates.T, b_cols.T, gates, b_cols

    def stage_head(c, hd, gate_forms):
        gates_t, b_rows, gates, b_cols = gate_forms
        sl = slice(hd * dh, (hd + 1) * dh)
        i_r = gates_t[hd:hd + 1, :]
        b_r = b_rows[HEADS + hd:HEADS + hd + 1, :]
        i_c = gates[:, hd:hd + 1]
        b_c = b_cols[:, HEADS + hd:HEADS + hd + 1]
        m_prev = m_ref[c, hd:hd + 1, 0:1]

        log_d = jnp.where(causal, b_c + (i_r - b_r), -jnp.inf)
        m_inter = b_c + m_prev
        m_t = jnp.maximum(m_inter, jnp.max(log_d, axis=-1, keepdims=True))
        dmat = jnp.exp(log_d - m_t)
        qb = qkv_ref[c, :, sl]
        kb = qkv_ref[c, :, d + hd * dh:d + (hd + 1) * dh]
        s = _dot_nt(qb, kb)
        p = (s * dmat).astype(BF16)
        v_aug = jnp.concatenate([qkv_ref[c, :, 2 * d + hd * dh:2 * d + (hd + 1) * dh], ones_blk],
                                axis=-1)
        w_inter = jnp.exp(m_inter - m_t)
        inter = _dot(qb, ct_ref[c, hd].astype(BF16))
        intra = _dot(p, v_aug)
        nd = w_inter * inter + intra
        num = nd[:, 0:dh]
        den = nd[:, dh:dh + 1]
        hh = num / jnp.maximum(jnp.abs(den), jnp.exp(-m_t))

        b_tot = b_c[L - 1:L, :]
        log_w = b_tot - b_c + i_c
        m_next = jnp.maximum(b_tot + m_prev, jnp.max(log_w, axis=0, keepdims=True))
        decay = jnp.exp(b_tot + m_prev - m_next)
        w_c = jnp.exp(log_w - m_next)
        upd = _dot(kb.T, (w_c * v_aug.astype(F32)).astype(BF16))
        ct_ref[c, hd] = decay * ct_ref[c, hd] + upd
        m_ref[c, hd:hd + 1, :] = jnp.broadcast_to(m_next, (1, 128))

        hg = hh * _sigmoid(proj_ref[c, :, sl].astype(F32))
        mu = jnp.mean(hg, axis=-1, keepdims=True)
        dev = hg - mu
        var = jnp.mean(dev * dev, axis=-1, keepdims=True)
        hn = dev * lax.rsqrt(var + EPS) * ng_ref[:, sl] + skip_ref[:, sl] * xc_ref[c, :, sl]
        gate = proj_ref[c, :, d + hd * dh:d + (hd + 1) * dh].astype(F32)
        y = hn * (gate * _sigmoid(gate))
        o_ref[c, :, :, sl] = y.reshape(SUB, SUBS_PER_SEG, dh).astype(o_ref.dtype)

    chains = range(M_CHAINS)
    for c in chains:
        stage_norm_in(c)
    for c in chains:
        stage_conv(c)
    early, late = chains[:M_CHAINS // 2], chains[M_CHAINS // 2:]
    assert len(late) * 2 == HEADS
    gfs = {}
    for c in early:
        stage_gate_proj(c)
    for c in early:
        gfs[c] = stage_qkv(c)
    for hd in range(HEADS):
        for c in early:
            stage_head(c, hd, gfs[c])
        c_late = late[hd % len(late)]
        if hd < len(late):
            stage_gate_proj(c_late)
        else:
            gfs[c_late] = stage_qkv(c_late)
    for hd in range(HEADS):
        for c in late:
            stage_head(c, hd, gfs[c])


def _mlstm(x, mod_all, gain_all, w_in_b, cw, cb, wq, wk, wv, wg, bg, ng, skip, layer, natural):
    bsz = x.shape[0]
    d = D_MODEL
    L = SEG_LEN
    nc = M_CHAINS
    once = pl.Buffered(1)
    lay2 = lambda shape: pl.BlockSpec((None,) + shape, lambda b, c: (layer, 0, 0),
                                      pipeline_mode=once)
    lay3 = lambda shape: pl.BlockSpec((None,) + shape, lambda b, c: (layer, 0, 0, 0),
                                      pipeline_mode=once)
    seg_spec = pl.BlockSpec((nc, SUB, None, SUBS_PER_SEG, d), lambda b, c: (b, 0, c, 0, 0))
    x_spec = (pl.BlockSpec((nc, None, SUBS_PER_SEG, SUB, d), lambda b, c: (b, c, 0, 0, 0))
              if natural else seg_spec)
    w_cols = lambda j: pl.BlockSpec((None, d, d), lambda b, c: (layer, 0, j), pipeline_mode=once)
    halo = (CONV_WIDTH - 1) * SUBS_PER_SEG
    return pl.pallas_call(
        functools.partial(_mlstm_kernel, natural=natural),
        out_shape=jax.ShapeDtypeStruct((bsz, SUB, N_SEG, SUBS_PER_SEG, d), BF16),
        grid=(bsz // nc, N_SEG),
        in_specs=[
            x_spec,
            pl.BlockSpec((None, nc, 1, 3 * d), lambda b, c: (layer, b, 0, 0)),
            lay2((1, d)),
            w_cols(2), w_cols(3), w_cols(4),
            lay2((CONV_WIDTH, d)),
            lay2((1, d)),
            lay3((HEADS, HEAD_DIM, HEAD_DIM)),
            lay3((HEADS, HEAD_DIM, HEAD_DIM)),
            lay3((HEADS, HEAD_DIM, HEAD_DIM)),
            lay2((3 * d, 128)),
            lay2((1, 128)),
            lay2((1, d)),
            lay2((1, d)),
        ],
        out_specs=seg_spec,
        scratch_shapes=[
            pltpu.VMEM((nc, HEADS, HEAD_DIM, M_AUG), F32),
            pltpu.VMEM((nc, 8, 128), F32),
            pltpu.VMEM((nc, 8, d), F32),
            pltpu.VMEM((nc, halo + L, d), F32),
            pltpu.VMEM((nc, L, 3 * d), BF16),
            pltpu.VMEM((nc, L, 2 * d), BF16),
            pltpu.VMEM((nc, L, d), F32),
            pltpu.VMEM((nc, L, d), BF16),
            pltpu.VMEM((nc, L, 2 * d), BF16),
        ],
        compiler_params=pltpu.CompilerParams(
            dimension_semantics=("parallel", "arbitrary"),
            vmem_limit_bytes=MLSTM_VMEM_LIMIT),
        name="mlstm",
    )(x, mod_all, gain_all, w_in_b, w_in_b, w_in_b, cw, cb, wq, wk, wv, wg, bg, ng, skip)


def _gelu_tanh(x):
    inner = math.sqrt(2.0 / math.pi) * (x + 0.044715 * (x * x * x))
    return x * (0.5 * (1.0 + jnp.tanh(inner)))


def _out_kernel(yt_ref, gt_ref, my_ref, x_ref, mod_ref, bglu_ref, og_ref, fg_ref,
                wglut_ref, wout_ref, qt_ref, o_ref, *, final, natural):
    d = D_MODEL
    rows = L_PAIR * BLK_LANES
    x_rows = _stream_rows(x_ref, natural)

    def stage_in(s):
        ls = range(s * L_PAIR, (s + 1) * L_PAIR)
        out_m = _dot(my_ref[s * L_PAIR:(s + 1) * L_PAIR].reshape(rows, d),
                     wout_ref[d:2 * d, :])
        yt = jnp.concatenate(
            [yt_ref[:, l * SSM_GROUP:(l + 1) * SSM_GROUP, :].reshape(d, BLK_LANES) for l in ls],
            axis=-1)
        y = _gelu_tanh(_dot(yt, qt_ref[...]))
        return out_m, y

    def stage_glu(s, y):
        ls = range(s * L_PAIR, (s + 1) * L_PAIR)
        glu = y * _sigmoid(_dot(wglut_ref[...], y.astype(BF16)) + bglu_ref[...])
        gate_s = jnp.concatenate([gt_ref[l] for l in ls], axis=-1).astype(F32)
        ms = jnp.mean(glu * glu, axis=0, keepdims=True)
        return (glu * lax.rsqrt(ms + EPS) * og_ref[...] * gate_s).astype(BF16)

    def stage_out(s, out_m, ssm_y):
        out = _dot_tn(ssm_y, wout_ref[0:d, :]) + out_m
        xn = x_rows(s) + mod_ref[:, 2 * d:3 * d] * out
        if final:
            return _rms(xn) * fg_ref[...]
        o_ref[s * L_PAIR:(s + 1) * L_PAIR] = xn.reshape(L_PAIR, N_SEG, SUBS_PER_SEG, d)

    streams = range(OUT_STREAMS)
    ins = [stage_in(s) for s in streams]
    ssm = [stage_glu(s, ins[s][1]) for s in streams]
    outs = [stage_out(s, ins[s][0], ssm[s]) for s in streams]
    if final:
        n_l = OUT_STREAMS * L_PAIR
        by_l = jnp.concatenate(outs, axis=0).reshape(n_l, BLK_LANES, d)
        o_ref[...] = jnp.swapaxes(by_l, 0, 1).reshape(N_SEG, SUBS_PER_SEG, n_l, d)


def _out_stage(yt, gt, my, x, mod_all, bglu_col, og_col, fg, wglut, wout, qperm_t, layer, final,
               natural):
    bsz = x.shape[0]
    d = D_MODEL
    n_l = OUT_STREAMS * L_PAIR
    col = pl.BlockSpec((None, d, 1), lambda b, p: (layer, 0, 0))
    out_dims = (bsz, N_SEG, SUBS_PER_SEG, SUB, d) if final else (bsz, SUB, N_SEG, SUBS_PER_SEG, d)
    return pl.pallas_call(
        functools.partial(_out_kernel, final=final, natural=natural),
        out_shape=jax.ShapeDtypeStruct(out_dims, F32),
        grid=(bsz, SUB // n_l),
        in_specs=[
            pl.BlockSpec((SSM_GROUPS, n_l * SSM_GROUP, BLK_LANES), lambda b, p: (0, p, b)),
            pl.BlockSpec((None, n_l, d, BLK_LANES), lambda b, p: (b, p, 0, 0)),
            _token_spec(False), _token_spec(natural),
            pl.BlockSpec((None, None, 1, 3 * d), lambda b, p: (layer, b, 0, 0)),
            col, col,
            pl.BlockSpec((1, d), lambda b, p: (0, 0)),
            pl.BlockSpec((None, d, d), lambda b, p: (layer, 0, 0)),
            pl.BlockSpec((None, 2 * d, d), lambda b, p: (layer, 0, 0)),
            pl.BlockSpec((L_PAIR * BLK_LANES, L_PAIR * BLK_LANES), lambda b, p: (0, 0)),
        ],
        out_specs=_token_spec(final),
        compiler_params=pltpu.CompilerParams(
            dimension_semantics=("parallel", "parallel"),
            vmem_limit_bytes=VMEM_LIMIT),
        name="out_stage",
    )(yt, gt, my, x, mod_all, bglu_col, og_col, fg, wglut, wout, qperm_t)


def _lane_permutation():
    q = np.zeros((BLK_LANES, BLK_LANES), np.float32)
    for seg in range(N_SEG):
        for kl in range(SUBS_PER_SEG):
            q[seg * SUBS_PER_SEG + kl, kl * N_SEG + seg] = 1.0
    return np.kron(np.eye(L_PAIR, dtype=np.float32), q)


def kernel(x, c, norm_gain, w_mod, b_mod, w_in, ssm_lambda_re, ssm_lambda_im, ssm_log_dt,
           ssm_b_re, ssm_b_im, ssm_c_re, ssm_c_im, ssm_d, ssm_w_glu, ssm_b_glu,
           ssm_out_gain, m_conv_w, m_conv_b, m_wq, m_wk, m_wv, m_w_gates, m_b_igate,
           m_b_fgate, m_norm_gain, m_skip, w_out, final_gain):
    bsz, seq, d = x.shape
    depth = w_in.shape[0]
    assert d == D_MODEL and seq == SEQ

    h = x.reshape(bsz, N_SEG, SUBS_PER_SEG, SUB, d)
    mod_all = _modulation(c, w_mod, b_mod).reshape(depth, bsz, 1, 3 * d)
    rows = lambda v: v.reshape(depth, 1, -1)
    cols = lambda v: v.reshape(depth, -1, 1)
    qperm = jnp.asarray(_lane_permutation(), BF16)
    qperm_t = jnp.asarray(_lane_permutation().T, BF16)

    gain_all = rows(norm_gain)
    w_in_b = w_in.astype(BF16)
    w_ssm_t = w_in_b[:, :, 0:2 * d].transpose(0, 2, 1)
    toept, ws, wot, atab, segtab = _s5_prep(ssm_lambda_re, ssm_lambda_im, ssm_log_dt,
                                            ssm_b_re, ssm_b_im, ssm_c_re, ssm_c_im)
    dcol = jnp.broadcast_to(ssm_d.reshape(depth * SSM_GROUPS, 1, SSM_GROUP),
                            (depth * SSM_GROUPS, SUB, SSM_GROUP)).reshape(-1, S5_ROW, 1)
    wq, wk, wv = m_wq.astype(BF16), m_wk.astype(BF16), m_wv.astype(BF16)
    gate_pad = ((0, 0), (0, 0), (0, 128 - 2 * HEADS))
    wg = jnp.pad(m_w_gates, gate_pad).astype(BF16)
    bg = jnp.pad(jnp.concatenate([m_b_igate, m_b_fgate], axis=-1).reshape(depth, 1, 2 * HEADS),
                 gate_pad)
    wglut = ssm_w_glu.astype(BF16).transpose(0, 2, 1)
    wout = w_out.astype(BF16)
    fg = final_gain.reshape(1, d)

    for l in range(depth):
        natural = l == 0
        final = l == depth - 1
        zt, gt = _s5_in(h, mod_all, gain_all, w_ssm_t, qperm, l, natural)
        yt = _s5_core(zt, toept, ws, wot, atab, segtab, dcol, bsz, l)
        my = _mlstm(h, mod_all, gain_all, w_in_b, m_conv_w, rows(m_conv_b), wq, wk, wv, wg, bg,
                    rows(m_norm_gain), rows(m_skip), l, natural)
        h = _out_stage(yt, gt, my, h, mod_all, cols(ssm_b_glu), cols(ssm_out_gain), fg,
                       wglut, wout, qperm_t, l, final, natural)

    return h.reshape(bsz, seq, d)
```

```python
import functools
import math

import numpy as np
import jax
import jax.numpy as jnp
from jax import lax
from jax.experimental import pallas as pl
from jax.experimental.pallas import tpu as pltpu

F32 = jnp.float32
BF16 = jnp.bfloat16

D_MODEL = 1024
SSM_GROUP = 16
SSM_GROUPS = D_MODEL // SSM_GROUP
SSM_STATE = 64
HEADS = 4
HEAD_DIM = D_MODEL // HEADS
CONV_WIDTH = 4
EPS = 1e-6

SUB = 16
SUBS_PER_SEG = 16
SEG_LEN = SUB * SUBS_PER_SEG
N_SEG = 8
SEQ = N_SEG * SEG_LEN
BLK_LANES = N_SEG * SUBS_PER_SEG
S5_ROW = SUB * SSM_GROUP
STATE_HALF = 128
STATE_LANES = 2 * STATE_HALF
S5_GROUPS_PER_STEP = 8
L_PAIR = 2
OUT_STREAMS = 4
M_AUG = HEAD_DIM + 128
VMEM_LIMIT = 56 * 1024 * 1024
MLSTM_VMEM_LIMIT = 60 * 1024 * 1024

_NT = (((1,), (1,)), ((), ()))
_TN = (((0,), (0,)), ((), ()))


def _dot(a, b):
    return jnp.dot(a, b, preferred_element_type=F32)


def _dot_nt(a, b):
    return lax.dot_general(a, b, _NT, preferred_element_type=F32)


def _dot_tn(a, b):
    return lax.dot_general(a, b, _TN, preferred_element_type=F32)


def _sigmoid(x):
    return jax.nn.sigmoid(x)


def _rms(x):
    return x * lax.rsqrt(jnp.mean(x * x, axis=-1, keepdims=True) + EPS)


def _modulated_norm(x, gain, mod):
    shift = mod[:, 0:D_MODEL]
    scale = mod[:, D_MODEL:2 * D_MODEL]
    return _rms(x) * (gain * (1.0 + scale)) + shift


def _split_hi_lo(v):
    hi = v.astype(BF16)
    lo = (v - hi.astype(F32)).astype(BF16)
    return hi, lo


def _dot_nt_f32(a, b):
    a_hi, a_lo = _split_hi_lo(a)
    b_hi, b_lo = _split_hi_lo(b)
    return _dot_nt(a_hi, b_hi) + _dot_nt(a_hi, b_lo) + _dot_nt(a_lo, b_hi)


def _cmul(a_re, a_im, b_re, b_im):
    return a_re * b_re - a_im * b_im, a_re * b_im + a_im * b_re


def _cmul_add(p_re, p_im, s_re, s_im, add_re, add_im):
    return (p_re * s_re - p_im * s_im + add_re,
            p_re * s_im + p_im * s_re + add_im)


def _mod_kernel(c_ref, w_ref, b_ref, o_ref):
    cv = c_ref[...]
    act = cv * _sigmoid(cv)
    o_ref[...] = _dot(act.astype(BF16), w_ref[...].astype(BF16)) + b_ref[...]


def _modulation(c, w_mod, b_mod):
    depth, d, n = w_mod.shape
    bsz = c.shape[0]
    return pl.pallas_call(
        _mod_kernel,
        out_shape=jax.ShapeDtypeStruct((depth, bsz, n), F32),
        grid=(depth, n // d),
        in_specs=[
            pl.BlockSpec((bsz, d), lambda l, j: (0, 0)),
            pl.BlockSpec((None, d, d), lambda l, j: (l, 0, j)),
            pl.BlockSpec((None, 1, d), lambda l, j: (l, 0, j)),
        ],
        out_specs=pl.BlockSpec((None, bsz, d), lambda l, j: (l, 0, j)),
        compiler_params=pltpu.CompilerParams(
            dimension_semantics=("parallel", "parallel"),
            vmem_limit_bytes=VMEM_LIMIT),
        name="adaln_mod",
    )(c, w_mod, b_mod.reshape(depth, 1, n))


def _s5_prep_kernel(lre_ref, lim_ref, ldt_ref, btre_ref, btim_ref, cre_ref, cim_ref,
                    toept_ref, ws_ref, wot_ref, atab_ref, seg_ref):
    gp = S5_GROUPS_PER_STEP
    lane_blk = lax.broadcasted_iota(jnp.int32, (S5_ROW, S5_ROW), 1) // SSM_GROUP
    row8 = lax.broadcasted_iota(jnp.int32, (8, STATE_HALF), 0)
    for j in range(gp):
        lr = lre_ref[j]
        li = lim_ref[j]
        dt = jnp.exp(ldt_ref[j])
        mag = jnp.exp(lr * dt)
        a_re = mag * jnp.cos(li * dt)
        a_im = mag * jnp.sin(li * dt)
        inv = 1.0 / (lr * lr + li * li)
        k_re = ((a_re - 1.0) * lr + a_im * li) * inv
        k_im = (a_im * lr - (a_re - 1.0) * li) * inv
        bb_re, bb_im = _cmul(k_re, k_im, btre_ref[j], btim_ref[j])
        c_re = cre_ref[j]
        c_im = cim_ref[j]

        pw = [(jnp.ones_like(a_re), jnp.zeros_like(a_re))]
        for _ in range(SUB):
            pw.append(_cmul(pw[-1][0], pw[-1][1], a_re, a_im))

        ws_rows, wot_rows, ca_re, ca_im = [], [], [], []
        for l in range(SUB):
            w_re, w_im = _cmul(pw[SUB - 1 - l][0], pw[SUB - 1 - l][1], bb_re, bb_im)
            ws_rows.append(jnp.concatenate([w_re, w_im], axis=-1))
            o_re, o_im = _cmul(c_re, c_im, pw[l + 1][0], pw[l + 1][1])
            wot_rows.append(jnp.concatenate([o_re, -o_im], axis=-1))
            g_re, g_im = _cmul(c_re, c_im, pw[l][0], pw[l][1])
            ca_re.append(g_re)
            ca_im.append(g_im)
        ws_ref[j] = jnp.concatenate(ws_rows, axis=0).astype(BF16)
        wot_ref[j] = jnp.concatenate(wot_rows, axis=0).astype(BF16)

        bbt_re = jnp.concatenate([bb_re] * SUB, axis=0)
        bbt_im = jnp.concatenate([bb_im] * SUB, axis=0)
        kw = (_dot_nt_f32(jnp.concatenate(ca_re, axis=0), bbt_re)
              - _dot_nt_f32(jnp.concatenate(ca_im, axis=0), bbt_im))
        toep = jnp.where(lane_blk == 0, kw, 0.0)
        for lp in range(1, SUB):
            shifted = jnp.concatenate(
                [jnp.zeros((SSM_GROUP * lp, S5_ROW), F32), kw[0:S5_ROW - SSM_GROUP * lp]], axis=0)
            toep = jnp.where(lane_blk == lp, shifted, toep)
        toept_ref[j] = toep.astype(BF16)

        s_re, s_im = pw[SUB]
        atab_ref[j] = jnp.concatenate([jnp.broadcast_to(s_re, (8, STATE_HALF)),
                                       jnp.broadcast_to(s_im, (8, STATE_HALF))], axis=0)
        for _ in range(4):
            s_re, s_im = _cmul(s_re, s_im, s_re, s_im)
        seg = jnp.zeros((8, STATE_HALF), F32)
        for i in range(3):
            seg = jnp.where(row8 == i, s_re, jnp.where(row8 == 3 + i, s_im, seg))
            s_re, s_im = _cmul(s_re, s_im, s_re, s_im)
        seg_ref[j] = seg


def _s5_prep(lam_re, lam_im, log_dt, b_re, b_im, c_re, c_im):
    depth, g, p = lam_re.shape
    cg = SSM_GROUP
    n = depth * g
    gp = S5_GROUPS_PER_STEP
    lane_pad = ((0, 0), (0, 0), (0, STATE_HALF - p))

    def state_rows(v, fill):
        v = v.reshape(n, -1, p)
        return jnp.pad(v, lane_pad, constant_values=fill)

    args = (state_rows(lam_re, -1.0), state_rows(lam_im, 0.0), log_dt.reshape(n, 1, 1),
            state_rows(b_re.transpose(0, 1, 3, 2), 0.0), state_rows(b_im.transpose(0, 1, 3, 2), 0.0),
            state_rows(c_re, 0.0), state_rows(c_im, 0.0))
    vec = pl.BlockSpec((gp, 1, STATE_HALF), lambda i: (i, 0, 0))
    mat = pl.BlockSpec((gp, cg, STATE_HALF), lambda i: (i, 0, 0))
    big = pl.BlockSpec((gp, S5_ROW, STATE_LANES), lambda i: (i, 0, 0))
    return pl.pallas_call(
        _s5_prep_kernel,
        out_shape=(jax.ShapeDtypeStruct((n, S5_ROW, S5_ROW), BF16),
                   jax.ShapeDtypeStruct((n, S5_ROW, STATE_LANES), BF16),
                   jax.ShapeDtypeStruct((n, S5_ROW, STATE_LANES), BF16),
                   jax.ShapeDtypeStruct((n, 16, STATE_HALF), F32),
                   jax.ShapeDtypeStruct((n, 8, STATE_HALF), F32)),
        grid=(n // gp,),
        in_specs=[vec, vec, pl.BlockSpec((gp, 1, 1), lambda i: (i, 0, 0)), mat, mat, mat, mat],
        out_specs=(big, big, big,
                   pl.BlockSpec((gp, 16, STATE_HALF), lambda i: (i, 0, 0)),
                   pl.BlockSpec((gp, 8, STATE_HALF), lambda i: (i, 0, 0))),
        compiler_params=pltpu.CompilerParams(
            dimension_semantics=("parallel",),
            vmem_limit_bytes=VMEM_LIMIT),
        name="s5_prep",
    )(*args)


def _stream_rows(x_ref, natural):
    n_l = OUT_STREAMS * L_PAIR
    rows = L_PAIR * BLK_LANES
    if natural:
        by_l = jnp.swapaxes(x_ref[...].reshape(BLK_LANES, n_l, D_MODEL), 0, 1)
        return lambda s: by_l[s * L_PAIR:(s + 1) * L_PAIR].reshape(rows, D_MODEL)
    return lambda s: x_ref[s * L_PAIR:(s + 1) * L_PAIR].reshape(rows, D_MODEL)


def _s5_in_kernel(x_ref, mod_ref, gain_ref, wut_ref, wgt_ref, q_ref, zt_ref, gt_ref,
                  xp_ref=None, *, natural):
    d = D_MODEL
    x_rows = _stream_rows(x_ref, natural)

    def stage_norm(s):
        x = x_rows(s)
        if natural:
            xp_ref[s * L_PAIR:(s + 1) * L_PAIR] = x.reshape(L_PAIR, N_SEG, SUBS_PER_SEG, d)
        return _modulated_norm(x, gain_ref[...], mod_ref[...]).astype(BF16)

    def stage_proj(s, h):
        ut = _dot_nt(wut_ref[...], h).astype(BF16)
        sgt = _dot_nt(wgt_ref[...], h)
        p = _dot(ut, q_ref[...]).astype(BF16)
        for l in range(L_PAIR):
            lanes = slice(l * BLK_LANES, (l + 1) * BLK_LANES)
            row0 = (s * L_PAIR + l) * SSM_GROUP
            zt_ref[:, row0:row0 + SSM_GROUP, :] = (
                p[:, lanes].reshape(SSM_GROUPS, SSM_GROUP, BLK_LANES))
            g = sgt[:, lanes]
            gt_ref[s * L_PAIR + l] = (g * _sigmoid(g)).astype(BF16)

    hs = [stage_norm(s) for s in range(OUT_STREAMS)]
    for s in range(OUT_STREAMS):
        stage_proj(s, hs[s])


def _token_spec(natural):
    n_l = OUT_STREAMS * L_PAIR
    if natural:
        assert n_l == 8
        return pl.BlockSpec((None, N_SEG, SUBS_PER_SEG, n_l, D_MODEL), lambda b, p: (b, 0, 0, p, 0))
    return pl.BlockSpec((None, n_l, N_SEG, SUBS_PER_SEG, D_MODEL), lambda b, p: (b, p, 0, 0, 0))


def _s5_in(x, mod_all, gain_all, w_ssm_t, qperm, layer, natural):
    bsz = x.shape[0]
    d = D_MODEL
    n_l = OUT_STREAMS * L_PAIR
    out_shape = [jax.ShapeDtypeStruct((SSM_GROUPS, S5_ROW, bsz * BLK_LANES), BF16),
                 jax.ShapeDtypeStruct((bsz, SUB, d, BLK_LANES), BF16)]
    out_specs = [pl.BlockSpec((SSM_GROUPS, n_l * SSM_GROUP, BLK_LANES), lambda b, p: (0, p, b)),
                 pl.BlockSpec((None, n_l, d, BLK_LANES), lambda b, p: (b, p, 0, 0))]
    if natural:
        out_shape.append(jax.ShapeDtypeStruct((bsz, SUB, N_SEG, SUBS_PER_SEG, d), F32))
        out_specs.append(_token_spec(False))
    return pl.pallas_call(
        functools.partial(_s5_in_kernel, natural=natural),
        out_shape=tuple(out_shape),
        grid=(bsz, SUB // n_l),
        in_specs=[
            _token_spec(natural),
            pl.BlockSpec((None, None, 1, 3 * d), lambda b, p: (layer, b, 0, 0)),
            pl.BlockSpec((None, 1, d), lambda b, p: (layer, 0, 0)),
            pl.BlockSpec((None, d, d), lambda b, p: (layer, 0, 0)),
            pl.BlockSpec((None, d, d), lambda b, p: (layer, 1, 0)),
            pl.BlockSpec((L_PAIR * BLK_LANES, L_PAIR * BLK_LANES), lambda b, p: (0, 0)),
        ],
        out_specs=tuple(out_specs),
        compiler_params=pltpu.CompilerParams(
            dimension_semantics=("parallel", "parallel"),
            vmem_limit_bytes=VMEM_LIMIT),
        name="s5_in",
    )(x, mod_all, gain_all, w_ssm_t, w_ssm_t, qperm)


def _s5_core_kernel(zt_ref, toept_ref, ws_ref, wot_ref, atab_ref, seg_ref, dcol_ref,
                    yt_ref, loc_ref, sprev_ref, *, bsz):
    gs = S5_GROUPS_PER_STEP
    half = STATE_HALF
    row8 = lax.broadcasted_iota(jnp.int32, (N_SEG, half), 0)

    def shift_down(v, n):
        return jnp.where(row8 >= n, pltpu.roll(v, n, axis=0), 0.0)

    for j in range(gs):
        loc_ref[j] = _dot_tn(zt_ref[j], ws_ref[j])

    for j in range(gs):
        a_re = atab_ref[j, 0:8, :]
        a_im = atab_ref[j, 8:16, :]
        for b in range(bsz):
            base = b * BLK_LANES

            def rows(kl, base=base):
                return pl.ds(base + kl * N_SEG, N_SEG)

            e_re = jnp.zeros((N_SEG, half), F32)
            e_im = jnp.zeros((N_SEG, half), F32)
            for kl in range(SUBS_PER_SEG):
                e_re, e_im = _cmul_add(a_re, a_im, e_re, e_im,
                                       loc_ref[j, rows(kl), 0:half],
                                       loc_ref[j, rows(kl), half:STATE_LANES])
            for i, n in enumerate((1, 2, 4)):
                p_re = seg_ref[j, i:i + 1, :]
                p_im = seg_ref[j, 3 + i:4 + i, :]
                e_re, e_im = _cmul_add(p_re, p_im, shift_down(e_re, n), shift_down(e_im, n),
                                       e_re, e_im)
            s_re = shift_down(e_re, 1)
            s_im = shift_down(e_im, 1)
            for kl in range(SUBS_PER_SEG):
                sprev_ref[j, rows(kl), 0:half] = s_re
                sprev_ref[j, rows(kl), half:STATE_LANES] = s_im
                s_re, s_im = _cmul_add(a_re, a_im, s_re, s_im,
                                       loc_ref[j, rows(kl), 0:half],
                                       loc_ref[j, rows(kl), half:STATE_LANES])

    for j in range(gs):
        zt = zt_ref[j]
        out = (_dot(toept_ref[j], zt)
               + _dot_nt(wot_ref[j], sprev_ref[j].astype(BF16))
               + dcol_ref[j] * zt.astype(F32))
        yt_ref[j] = out.astype(BF16)


def _s5_core(zt, toept, ws, wot, atab, segtab, dcol, bsz, layer):
    g, rows, lanes = zt.shape
    gs = S5_GROUPS_PER_STEP
    off = layer * (g // gs)
    wspec = pl.BlockSpec((gs, S5_ROW, STATE_LANES), lambda i: (off + i, 0, 0))
    return pl.pallas_call(
        functools.partial(_s5_core_kernel, bsz=bsz),
        out_shape=jax.ShapeDtypeStruct((g, rows, lanes), BF16),
        grid=(g // gs,),
        in_specs=[
            pl.BlockSpec((gs, rows, lanes), lambda i: (i, 0, 0)),
            wspec, wspec, wspec,
            pl.BlockSpec((gs, 16, STATE_HALF), lambda i: (off + i, 0, 0)),
            pl.BlockSpec((gs, 8, STATE_HALF), lambda i: (off + i, 0, 0)),
            pl.BlockSpec((gs, S5_ROW, 1), lambda i: (off + i, 0, 0)),
        ],
        out_specs=pl.BlockSpec((gs, rows, lanes), lambda i: (i, 0, 0)),
        scratch_shapes=[pltpu.VMEM((gs, lanes, STATE_LANES), F32),
                        pltpu.VMEM((gs, lanes, STATE_LANES), F32)],
        compiler_params=pltpu.CompilerParams(
            dimension_semantics=("parallel",),
            vmem_limit_bytes=VMEM_LIMIT),
        name="s5_core",
    )(zt, toept, ws, wot, atab, segtab, dcol)


def _log_sigmoid(x):
    return -(jnp.maximum(-x, 0.0) + jnp.log1p(jnp.exp(-jnp.abs(x))))


def _chunk_time(idx):
    return ((idx & (SUBS_PER_SEG - 1)) * SUB) | (idx >> 4)


M_CHAINS = 4


def _mlstm_kernel(x_ref, mod_ref, gain_ref, w_min_ref, w_mo_ref, w_mg_ref, cw_ref, cb_ref,
                   wq_ref, wk_ref, wv_ref, wg_ref, bg_ref, ng_ref, skip_ref, o_ref,
                   ct_ref, m_ref, tail_ref, ext_ref, qkv_ref, proj_ref, xc_ref, hn_ref, cvb_ref,
                   *, natural):
    L, dh, d = SEG_LEN, HEAD_DIM, D_MODEL
    halo = (CONV_WIDTH - 1) * SUBS_PER_SEG

    @pl.when(pl.program_id(1) == 0)
    def _():
        ct_ref[...] = jnp.zeros_like(ct_ref)
        m_ref[...] = jnp.zeros_like(m_ref)
        tail_ref[...] = jnp.zeros_like(tail_ref)

    t_row = _chunk_time(lax.broadcasted_iota(jnp.int32, (L, L), 0))
    t_col = _chunk_time(lax.broadcasted_iota(jnp.int32, (L, L), 1))
    causal = t_col <= t_row
    tri_l = jnp.where(causal, 1.0, 0.0).astype(BF16)
    lane = lax.broadcasted_iota(jnp.int32, (L, 128), 1)
    ones_blk = jnp.where(lane == 0, 1.0, 0.0).astype(BF16)
    row16 = lax.broadcasted_iota(jnp.int32, (SUBS_PER_SEG, d), 0)

    def stage_norm_in(c):
        x = x_ref[c]
        if natural:
            x = jnp.swapaxes(x, 0, 1)
        hn_ref[c] = _modulated_norm(x.reshape(L, d), gain_ref[...], mod_ref[c]).astype(BF16)
        ext_ref[c, halo:halo + L, :] = _dot(hn_ref[c], w_min_ref[...])

    def stage_gate_proj(c):
        proj_ref[c, :, 0:d] = _dot(hn_ref[c], w_mo_ref[...]).astype(BF16)
        proj_ref[c, :, d:2 * d] = _dot(hn_ref[c], w_mg_ref[...]).astype(BF16)

    def stage_conv(c):
        m_in = ext_ref[c, halo:halo + L, :]
        cvb_ref[c, :, d:2 * d] = m_in.astype(BF16)
        for i in range(CONV_WIDTH - 1):
            lsrc = SUB - (CONV_WIDTH - 1) + i
            r0 = halo + lsrc * SUBS_PER_SEG
            shifted = ext_ref[c, pl.ds(r0 - 1, SUBS_PER_SEG), :]
            ext_ref[c, i * SUBS_PER_SEG:(i + 1) * SUBS_PER_SEG, :] = jnp.where(
                row16 == 0, tail_ref[c, i:i + 1, :], shifted)
            tail_ref[c, i:i + 1, :] = ext_ref[c, r0 + SUBS_PER_SEG - 1:r0 + SUBS_PER_SEG, :]
        acc = m_in * cw_ref[CONV_WIDTH - 1:CONV_WIDTH, :] + cb_ref[...]
        for j in range(CONV_WIDTH - 1):
            back = CONV_WIDTH - 1 - j
            acc = acc + ext_ref[c, pl.ds(halo - back * SUBS_PER_SEG, L), :] * cw_ref[j:j + 1, :]
        xc = acc * _sigmoid(acc)
        xc_ref[c] = xc
        cvb_ref[c, :, 0:d] = xc.astype(BF16)

    def stage_qkv(c):
        for hd in range(HEADS):
            sl = slice(hd * dh, (hd + 1) * dh)
            xcb = cvb_ref[c, :, hd * dh:(hd + 1) * dh]
            qkv_ref[c, :, hd * dh:(hd + 1) * dh] = _dot(xcb, wq_ref[hd]).astype(BF16)
            qkv_ref[c, :, d + hd * dh:d + (hd + 1) * dh] = (
                _dot(xcb, wk_ref[hd]) * (dh ** -0.5)).astype(BF16)
            qkv_ref[c, :, 2 * d + hd * dh:2 * d + (hd + 1) * dh] = (
                _dot(cvb_ref[c, :, d + hd * dh:d + (hd + 1) * dh], wv_ref[hd]).astype(BF16))
        gates = _dot(qkv_ref[c], wg_ref[...]) + bg_ref[...]
        log_f = _log_sigmoid(gates)
        lf_hi, lf_lo = _split_hi_lo(log_f)
        b_cols = _dot(tri_l, lf_hi) + _dot(tri_l, lf_lo)
        return gates.T, b_cols.T, gates, b_cols

    def stage_head(c, hd, gate_forms):
        gates_t, b_rows, gates, b_cols = gate_forms
        sl = slice(hd * dh, (hd + 1) * dh)
        i_r = gates_t[hd:hd + 1, :]
        b_r = b_rows[HEADS + hd:HEADS + hd + 1, :]
        i_c = gates[:, hd:hd + 1]
        b_c = b_cols[:, HEADS + hd:HEADS + hd + 1]
        m_prev = m_ref[c, hd:hd + 1, 0:1]

        log_d = jnp.where(causal, b_c + (i_r - b_r), -jnp.inf)
        m_inter = b_c + m_prev
        m_t = jnp.maximum(m_inter, jnp.max(log_d, axis=-1, keepdims=True))
        dmat = jnp.exp(log_d - m_t)
        qb = qkv_ref[c, :, sl]
        kb = qkv_ref[c, :, d + hd * dh:d + (hd + 1) * dh]
        s = _dot_nt(qb, kb)
        p = (s * dmat).astype(BF16)
        v_aug = jnp.concatenate([qkv_ref[c, :, 2 * d + hd * dh:2 * d + (hd + 1) * dh], ones_blk],
                                axis=-1)
        w_inter = jnp.exp(m_inter - m_t)
        inter = _dot(qb, ct_ref[c, hd].astype(BF16))
        intra = _dot(p, v_aug)
        nd = w_inter * inter + intra
        num = nd[:, 0:dh]
        den = nd[:, dh:dh + 1]
        hh = num / jnp.maximum(jnp.abs(den), jnp.exp(-m_t))

        b_tot = b_c[L - 1:L, :]
        log_w = b_tot - b_c + i_c
        m_next = jnp.maximum(b_tot + m_prev, jnp.max(log_w, axis=0, keepdims=True))
        decay = jnp.exp(b_tot + m_prev - m_next)
        w_c = jnp.exp(log_w - m_next)
        upd = _dot(kb.T, (w_c * v_aug.astype(F32)).astype(BF16))
        ct_ref[c, hd] = decay * ct_ref[c, hd] + upd
        m_ref[c, hd:hd + 1, :] = jnp.broadcast_to(m_next, (1, 128))

        hg = hh * _sigmoid(proj_ref[c, :, sl].astype(F32))
        mu = jnp.mean(hg, axis=-1, keepdims=True)
        dev = hg - mu
        var = jnp.mean(dev * dev, axis=-1, keepdims=True)
        hn = dev * lax.rsqrt(var + EPS) * ng_ref[:, sl] + skip_ref[:, sl] * xc_ref[c, :, sl]
        gate = proj_ref[c, :, d + hd * dh:d + (hd + 1) * dh].astype(F32)
        y = hn * (gate * _sigmoid(gate))
        o_ref[c, :, :, sl] = y.reshape(SUB, SUBS_PER_SEG, dh).astype(o_ref.dtype)

    chains = range(M_CHAINS)
    for c in chains:
        stage_norm_in(c)
    for c in chains:
        stage_conv(c)
    early, late = chains[:M_CHAINS // 2], chains[M_CHAINS // 2:]
    assert len(late) * 2 == HEADS
    gfs = {}
    for c in early:
        stage_gate_proj(c)
    for c in early:
        gfs[c] = stage_qkv(c)
    for hd in range(HEADS):
        for c in early:
            stage_head(c, hd, gfs[c])
        c_late = late[hd % len(late)]
        if hd < len(late):
            stage_gate_proj(c_late)
        else:
            gfs[c_late] = stage_qkv(c_late)
    for hd in range(HEADS):
        for c in late:
            stage_head(c, hd, gfs[c])


def _mlstm(x, mod_all, gain_all, w_in_b, cw, cb, wq, wk, wv, wg, bg, ng, skip, layer, natural):
    bsz = x.shape[0]
    d = D_MODEL
    L = SEG_LEN
    nc = M_CHAINS
    once = pl.Buffered(1)
    lay2 = lambda shape: pl.BlockSpec((None,) + shape, lambda b, c: (layer, 0, 0),
                                      pipeline_mode=once)
    lay3 = lambda shape: pl.BlockSpec((None,) + shape, lambda b, c: (layer, 0, 0, 0),
                                      pipeline_mode=once)
    seg_spec = pl.BlockSpec((nc, SUB, None, SUBS_PER_SEG, d), lambda b, c: (b, 0, c, 0, 0))
    x_spec = (pl.BlockSpec((nc, None, SUBS_PER_SEG, SUB, d), lambda b, c: (b, c, 0, 0, 0))
              if natural else seg_spec)
    w_cols = lambda j: pl.BlockSpec((None, d, d), lambda b, c: (layer, 0, j), pipeline_mode=once)
    halo = (CONV_WIDTH - 1) * SUBS_PER_SEG
    return pl.pallas_call(
        functools.partial(_mlstm_kernel, natural=natural),
        out_shape=jax.ShapeDtypeStruct((bsz, SUB, N_SEG, SUBS_PER_SEG, d), BF16),
        grid=(bsz // nc, N_SEG),
        in_specs=[
            x_spec,
            pl.BlockSpec((None, nc, 1, 3 * d), lambda b, c: (layer, b, 0, 0)),
            lay2((1, d)),
            w_cols(2), w_cols(3), w_cols(4),
            lay2((CONV_WIDTH, d)),
            lay2((1, d)),
            lay3((HEADS, HEAD_DIM, HEAD_DIM)),
            lay3((HEADS, HEAD_DIM, HEAD_DIM)),
            lay3((HEADS, HEAD_DIM, HEAD_DIM)),
            lay2((3 * d, 128)),
            lay2((1, 128)),
            lay2((1, d)),
            lay2((1, d)),
        ],
        out_specs=seg_spec,
        scratch_shapes=[
            pltpu.VMEM((nc, HEADS, HEAD_DIM, M_AUG), F32),
            pltpu.VMEM((nc, 8, 128), F32),
            pltpu.VMEM((nc, 8, d), F32),
            pltpu.VMEM((nc, halo + L, d), F32),
            pltpu.VMEM((nc, L, 3 * d), BF16),
            pltpu.VMEM((nc, L, 2 * d), BF16),
            pltpu.VMEM((nc, L, d), F32),
            pltpu.VMEM((nc, L, d), BF16),
            pltpu.VMEM((nc, L, 2 * d), BF16),
        ],
        compiler_params=pltpu.CompilerParams(
            dimension_semantics=("parallel", "arbitrary"),
            vmem_limit_bytes=MLSTM_VMEM_LIMIT),
        name="mlstm",
    )(x, mod_all, gain_all, w_in_b, w_in_b, w_in_b, cw, cb, wq, wk, wv, wg, bg, ng, skip)


def _gelu_tanh(x):
    inner = math.sqrt(2.0 / math.pi) * (x + 0.044715 * (x * x * x))
    return x * (0.5 * (1.0 + jnp.tanh(inner)))


def _out_kernel(yt_ref, gt_ref, my_ref, x_ref, mod_ref, bglu_ref, og_ref, fg_ref,
                wglut_ref, wout_ref, qt_ref, o_ref, *, final, natural):
    d = D_MODEL
    rows = L_PAIR * BLK_LANES
    x_rows = _stream_rows(x_ref, natural)

    def stage_in(s):
        ls = range(s * L_PAIR, (s + 1) * L_PAIR)
        out_m = _dot(my_ref[s * L_PAIR:(s + 1) * L_PAIR].reshape(rows, d),
                     wout_ref[d:2 * d, :])
        yt = jnp.concatenate(
            [yt_ref[:, l * SSM_GROUP:(l + 1) * SSM_GROUP, :].reshape(d, BLK_LANES) for l in ls],
            axis=-1)
        y = _gelu_tanh(_dot(yt, qt_ref[...]))
        return out_m, y

    def stage_glu(s, y):
        ls = range(s * L_PAIR, (s + 1) * L_PAIR)
        glu = y * _sigmoid(_dot(wglut_ref[...], y.astype(BF16)) + bglu_ref[...])
        gate_s = jnp.concatenate([gt_ref[l] for l in ls], axis=-1).astype(F32)
        ms = jnp.mean(glu * glu, axis=0, keepdims=True)
        return (glu * lax.rsqrt(ms + EPS) * og_ref[...] * gate_s).astype(BF16)

    def stage_out(s, out_m, ssm_y):
        out = _dot_tn(ssm_y, wout_ref[0:d, :]) + out_m
        xn = x_rows(s) + mod_ref[:, 2 * d:3 * d] * out
        if final:
            return _rms(xn) * fg_ref[...]
        o_ref[s * L_PAIR:(s + 1) * L_PAIR] = xn.reshape(L_PAIR, N_SEG, SUBS_PER_SEG, d)

    streams = range(OUT_STREAMS)
    ins = [stage_in(s) for s in streams]
    ssm = [stage_glu(s, ins[s][1]) for s in streams]
    outs = [stage_out(s, ins[s][0], ssm[s]) for s in streams]
    if final:
        n_l = OUT_STREAMS * L_PAIR
        by_l = jnp.concatenate(outs, axis=0).reshape(n_l, BLK_LANES, d)
        o_ref[...] = jnp.swapaxes(by_l, 0, 1).reshape(N_SEG, SUBS_PER_SEG, n_l, d)


def _out_stage(yt, gt, my, x, mod_all, bglu_col, og_col, fg, wglut, wout, qperm_t, layer, final,
               natural):
    bsz = x.shape[0]
    d = D_MODEL
    n_l = OUT_STREAMS * L_PAIR
    col = pl.BlockSpec((None, d, 1), lambda b, p: (layer, 0, 0))
    out_dims = (bsz, N_SEG, SUBS_PER_SEG, SUB, d) if final else (bsz, SUB, N_SEG, SUBS_PER_SEG, d)
    return pl.pallas_call(
        functools.partial(_out_kernel, final=final, natural=natural),
        out_shape=jax.ShapeDtypeStruct(out_dims, F32),
        grid=(bsz, SUB // n_l),
        in_specs=[
            pl.BlockSpec((SSM_GROUPS, n_l * SSM_GROUP, BLK_LANES), lambda b, p: (0, p, b)),
            pl.BlockSpec((None, n_l, d, BLK_LANES), lambda b, p: (b, p, 0, 0)),
            _token_spec(False), _token_spec(natural),
            pl.BlockSpec((None, None, 1, 3 * d), lambda b, p: (layer, b, 0, 0)),
            col, col,
            pl.BlockSpec((1, d), lambda b, p: (0, 0)),
            pl.BlockSpec((None, d, d), lambda b, p: (layer, 0, 0)),
            pl.BlockSpec((None, 2 * d, d), lambda b, p: (layer, 0, 0)),
            pl.BlockSpec((L_PAIR * BLK_LANES, L_PAIR * BLK_LANES), lambda b, p: (0, 0)),
        ],
        out_specs=_token_spec(final),
        compiler_params=pltpu.CompilerParams(
            dimension_semantics=("parallel", "parallel"),
            vmem_limit_bytes=VMEM_LIMIT),
        name="out_stage",
    )(yt, gt, my, x, mod_all, bglu_col, og_col, fg, wglut, wout, qperm_t)


def _lane_permutation():
    q = np.zeros((BLK_LANES, BLK_LANES), np.float32)
    for seg in range(N_SEG):
        for kl in range(SUBS_PER_SEG):
            q[seg * SUBS_PER_SEG + kl, kl * N_SEG + seg] = 1.0
    return np.kron(np.eye(L_PAIR, dtype=np.float32), q)


def kernel(x, c, norm_gain, w_mod, b_mod, w_in, ssm_lambda_re, ssm_lambda_im, ssm_log_dt,
           ssm_b_re, ssm_b_im, ssm_c_re, ssm_c_im, ssm_d, ssm_w_glu, ssm_b_glu,
           ssm_out_gain, m_conv_w, m_conv_b, m_wq, m_wk, m_wv, m_w_gates, m_b_igate,
           m_b_fgate, m_norm_gain, m_skip, w_out, final_gain):
    bsz, seq, d = x.shape
    depth = w_in.shape[0]
    assert d == D_MODEL and seq == SEQ

    h = x.reshape(bsz, N_SEG, SUBS_PER_SEG, SUB, d)
    mod_all = _modulation(c, w_mod, b_mod).reshape(depth, bsz, 1, 3 * d)
    rows = lambda v: v.reshape(depth, 1, -1)
    cols = lambda v: v.reshape(depth, -1, 1)
    qperm = jnp.asarray(_lane_permutation(), BF16)
    qperm_t = jnp.asarray(_lane_permutation().T, BF16)

    gain_all = rows(norm_gain)
    w_in_b = w_in.astype(BF16)
    w_ssm_t = w_in_b[:, :, 0:2 * d].transpose(0, 2, 1)
    toept, ws, wot, atab, segtab = _s5_prep(ssm_lambda_re, ssm_lambda_im, ssm_log_dt,
                                            ssm_b_re, ssm_b_im, ssm_c_re, ssm_c_im)
    dcol = jnp.broadcast_to(ssm_d.reshape(depth * SSM_GROUPS, 1, SSM_GROUP),
                            (depth * SSM_GROUPS, SUB, SSM_GROUP)).reshape(-1, S5_ROW, 1)
    wq, wk, wv = m_wq.astype(BF16), m_wk.astype(BF16), m_wv.astype(BF16)
    gate_pad = ((0, 0), (0, 0), (0, 128 - 2 * HEADS))
    wg = jnp.pad(m_w_gates, gate_pad).astype(BF16)
    bg = jnp.pad(jnp.concatenate([m_b_igate, m_b_fgate], axis=-1).reshape(depth, 1, 2 * HEADS),
                 gate_pad)
    wglut = ssm_w_glu.astype(BF16).transpose(0, 2, 1)
    wout = w_out.astype(BF16)
    fg = final_gain.reshape(1, d)

    for l in range(depth):
        final = l == depth - 1
        if l == 0:
            zt, gt, h = _s5_in(h, mod_all, gain_all, w_ssm_t, qperm, l, natural=True)
        else:
            zt, gt = _s5_in(h, mod_all, gain_all, w_ssm_t, qperm, l, natural=False)
        yt = _s5_core(zt, toept, ws, wot, atab, segtab, dcol, bsz, l)
        my = _mlstm(h, mod_all, gain_all, w_in_b, m_conv_w, rows(m_conv_b), wq, wk, wv, wg, bg,
                    rows(m_norm_gain), rows(m_skip), l, natural=False)
        h = _out_stage(yt, gt, my, h, mod_all, cols(ssm_b_glu), cols(ssm_out_gain), fg,
                       wglut, wout, qperm_t, l, final, natural=False)

    return h.reshape(bsz, seq, d)
```

```python
import functools
import math

import numpy as np
import jax
import jax.numpy as jnp
from jax import lax
from jax.experimental import pallas as pl
from jax.experimental.pallas import tpu as pltpu

F32 = jnp.float32
BF16 = jnp.bfloat16

D_MODEL = 1024
SSM_GROUP = 16
SSM_GROUPS = D_MODEL // SSM_GROUP
SSM_STATE = 64
HEADS = 4
HEAD_DIM = D_MODEL // HEADS
CONV_WIDTH = 4
EPS = 1e-6

SUB = 16
SUBS_PER_SEG = 16
SEG_LEN = SUB * SUBS_PER_SEG
N_SEG = 8
SEQ = N_SEG * SEG_LEN
BLK_LANES = N_SEG * SUBS_PER_SEG
S5_ROW = SUB * SSM_GROUP
STATE_HALF = 128
STATE_LANES = 2 * STATE_HALF
S5_GROUPS_PER_STEP = 8
L_PAIR = 2
OUT_STREAMS = 4
M_AUG = HEAD_DIM + 128
VMEM_LIMIT = 56 * 1024 * 1024
MLSTM_VMEM_LIMIT = 60 * 1024 * 1024

_NT = (((1,), (1,)), ((), ()))
_TN = (((0,), (0,)), ((), ()))


def _dot(a, b):
    return jnp.dot(a, b, preferred_element_type=F32)


def _dot_nt(a, b):
    return lax.dot_general(a, b, _NT, preferred_element_type=F32)


def _dot_tn(a, b):
    return lax.dot_general(a, b, _TN, preferred_element_type=F32)


def _sigmoid(x):
    return jax.nn.sigmoid(x)


def _rms(x):
    return x * lax.rsqrt(jnp.mean(x * x, axis=-1, keepdims=True) + EPS)


def _modulated_norm(x, gain, mod):
    shift = mod[:, 0:D_MODEL]
    scale = mod[:, D_MODEL:2 * D_MODEL]
    return _rms(x) * (gain * (1.0 + scale)) + shift


def _split_hi_lo(v):
    hi = v.astype(BF16)
    lo = (v - hi.astype(F32)).astype(BF16)
    return hi, lo


def _dot_nt_f32(a, b):
    a_hi, a_lo = _split_hi_lo(a)
    b_hi, b_lo = _split_hi_lo(b)
    return _dot_nt(a_hi, b_hi) + _dot_nt(a_hi, b_lo) + _dot_nt(a_lo, b_hi)


def _cmul(a_re, a_im, b_re, b_im):
    return a_re * b_re - a_im * b_im, a_re * b_im + a_im * b_re


def _cmul_add(p_re, p_im, s_re, s_im, add_re, add_im):
    return (p_re * s_re - p_im * s_im + add_re,
            p_re * s_im + p_im * s_re + add_im)


def _mod_kernel(c_ref, w_ref, b_ref, o_ref):
    cv = c_ref[...]
    act = cv * _sigmoid(cv)
    o_ref[...] = _dot(act.astype(BF16), w_ref[...].astype(BF16)) + b_ref[...]


def _modulation(c, w_mod, b_mod):
    depth, d, n = w_mod.shape
    bsz = c.shape[0]
    return pl.pallas_call(
        _mod_kernel,
        out_shape=jax.ShapeDtypeStruct((depth, bsz, n), F32),
        grid=(depth, n // d),
        in_specs=[
            pl.BlockSpec((bsz, d), lambda l, j: (0, 0)),
            pl.BlockSpec((None, d, d), lambda l, j: (l, 0, j)),
            pl.BlockSpec((None, 1, d), lambda l, j: (l, 0, j)),
        ],
        out_specs=pl.BlockSpec((None, bsz, d), lambda l, j: (l, 0, j)),
        compiler_params=pltpu.CompilerParams(
            dimension_semantics=("parallel", "parallel"),
            vmem_limit_bytes=VMEM_LIMIT),
        name="adaln_mod",
    )(c, w_mod, b_mod.reshape(depth, 1, n))


def _s5_prep_kernel(lre_ref, lim_ref, ldt_ref, btre_ref, btim_ref, cre_ref, cim_ref,
                    toept_ref, ws_ref, wot_ref, atab_ref, seg_ref):
    gp = S5_GROUPS_PER_STEP
    lane_blk = lax.broadcasted_iota(jnp.int32, (S5_ROW, S5_ROW), 1) // SSM_GROUP
    row8 = lax.broadcasted_iota(jnp.int32, (8, STATE_HALF), 0)
    for j in range(gp):
        lr = lre_ref[j]
        li = lim_ref[j]
        dt = jnp.exp(ldt_ref[j])
        mag = jnp.exp(lr * dt)
        a_re = mag * jnp.cos(li * dt)
        a_im = mag * jnp.sin(li * dt)
        inv = 1.0 / (lr * lr + li * li)
        k_re = ((a_re - 1.0) * lr + a_im * li) * inv
        k_im = (a_im * lr - (a_re - 1.0) * li) * inv
        bb_re, bb_im = _cmul(k_re, k_im, btre_ref[j], btim_ref[j])
        c_re = cre_ref[j]
        c_im = cim_ref[j]

        pw = [(jnp.ones_like(a_re), jnp.zeros_like(a_re))]
        for _ in range(SUB):
            pw.append(_cmul(pw[-1][0], pw[-1][1], a_re, a_im))

        ws_rows, wot_rows, ca_re, ca_im = [], [], [], []
        for l in range(SUB):
            w_re, w_im = _cmul(pw[SUB - 1 - l][0], pw[SUB - 1 - l][1], bb_re, bb_im)
            ws_rows.append(jnp.concatenate([w_re, w_im], axis=-1))
            o_re, o_im = _cmul(c_re, c_im, pw[l + 1][0], pw[l + 1][1])
            wot_rows.append(jnp.concatenate([o_re, -o_im], axis=-1))
            g_re, g_im = _cmul(c_re, c_im, pw[l][0], pw[l][1])
            ca_re.append(g_re)
            ca_im.append(g_im)
        ws_ref[j] = jnp.concatenate(ws_rows, axis=0).astype(BF16)
        wot_ref[j] = jnp.concatenate(wot_rows, axis=0).astype(BF16)

        bbt_re = jnp.concatenate([bb_re] * SUB, axis=0)
        bbt_im = jnp.concatenate([bb_im] * SUB, axis=0)
        kw = (_dot_nt_f32(jnp.concatenate(ca_re, axis=0), bbt_re)
              - _dot_nt_f32(jnp.concatenate(ca_im, axis=0), bbt_im))
        toep = jnp.where(lane_blk == 0, kw, 0.0)
        for lp in range(1, SUB):
            shifted = jnp.concatenate(
                [jnp.zeros((SSM_GROUP * lp, S5_ROW), F32), kw[0:S5_ROW - SSM_GROUP * lp]], axis=0)
            toep = jnp.where(lane_blk == lp, shifted, toep)
        toept_ref[j] = toep.astype(BF16)

        s_re, s_im = pw[SUB]
        atab_ref[j] = jnp.concatenate([jnp.broadcast_to(s_re, (8, STATE_HALF)),
                                       jnp.broadcast_to(s_im, (8, STATE_HALF))], axis=0)
        for _ in range(4):
            s_re, s_im = _cmul(s_re, s_im, s_re, s_im)
        seg = jnp.zeros((8, STATE_HALF), F32)
        for i in range(3):
            seg = jnp.where(row8 == i, s_re, jnp.where(row8 == 3 + i, s_im, seg))
            s_re, s_im = _cmul(s_re, s_im, s_re, s_im)
        seg_ref[j] = seg


def _s5_prep(lam_re, lam_im, log_dt, b_re, b_im, c_re, c_im):
    depth, g, p = lam_re.shape
    cg = SSM_GROUP
    n = depth * g
    gp = S5_GROUPS_PER_STEP
    lane_pad = ((0, 0), (0, 0), (0, STATE_HALF - p))

    def state_rows(v, fill):
        v = v.reshape(n, -1, p)
        return jnp.pad(v, lane_pad, constant_values=fill)

    args = (state_rows(lam_re, -1.0), state_rows(lam_im, 0.0), log_dt.reshape(n, 1, 1),
            state_rows(b_re.transpose(0, 1, 3, 2), 0.0), state_rows(b_im.transpose(0, 1, 3, 2), 0.0),
            state_rows(c_re, 0.0), state_rows(c_im, 0.0))
    vec = pl.BlockSpec((gp, 1, STATE_HALF), lambda i: (i, 0, 0))
    mat = pl.BlockSpec((gp, cg, STATE_HALF), lambda i: (i, 0, 0))
    big = pl.BlockSpec((gp, S5_ROW, STATE_LANES), lambda i: (i, 0, 0))
    return pl.pallas_call(
        _s5_prep_kernel,
        out_shape=(jax.ShapeDtypeStruct((n, S5_ROW, S5_ROW), BF16),
                   jax.ShapeDtypeStruct((n, S5_ROW, STATE_LANES), BF16),
                   jax.ShapeDtypeStruct((n, S5_ROW, STATE_LANES), BF16),
                   jax.ShapeDtypeStruct((n, 16, STATE_HALF), F32),
                   jax.ShapeDtypeStruct((n, 8, STATE_HALF), F32)),
        grid=(n // gp,),
        in_specs=[vec, vec, pl.BlockSpec((gp, 1, 1), lambda i: (i, 0, 0)), mat, mat, mat, mat],
        out_specs=(big, big, big,
                   pl.BlockSpec((gp, 16, STATE_HALF), lambda i: (i, 0, 0)),
                   pl.BlockSpec((gp, 8, STATE_HALF), lambda i: (i, 0, 0))),
        compiler_params=pltpu.CompilerParams(
            dimension_semantics=("parallel",),
            vmem_limit_bytes=VMEM_LIMIT),
        name="s5_prep",
    )(*args)


def _stream_rows(x_ref, natural):
    n_l = OUT_STREAMS * L_PAIR
    rows = L_PAIR * BLK_LANES
    if natural:
        by_l = jnp.swapaxes(x_ref[...].reshape(BLK_LANES, n_l, D_MODEL), 0, 1)
        return lambda s: by_l[s * L_PAIR:(s + 1) * L_PAIR].reshape(rows, D_MODEL)
    return lambda s: x_ref[s * L_PAIR:(s + 1) * L_PAIR].reshape(rows, D_MODEL)


def _s5_in_kernel(x_ref, mod_ref, gain_ref, wut_ref, wgt_ref, q_ref, zt_ref, gt_ref,
                  xp_ref=None, *, natural):
    d = D_MODEL
    x_rows = _stream_rows(x_ref, natural)

    def stage_norm(s):
        x = x_rows(s)
        if natural:
            xp_ref[s * L_PAIR:(s + 1) * L_PAIR] = x.reshape(L_PAIR, N_SEG, SUBS_PER_SEG, d)
        return _modulated_norm(x, gain_ref[...], mod_ref[...]).astype(BF16)

    def stage_proj(s, h):
        ut = _dot_nt(wut_ref[...], h).astype(BF16)
        sgt = _dot_nt(wgt_ref[...], h)
        p = _dot(ut, q_ref[...]).astype(BF16)
        for l in range(L_PAIR):
            lanes = slice(l * BLK_LANES, (l + 1) * BLK_LANES)
            row0 = (s * L_PAIR + l) * SSM_GROUP
            zt_ref[:, row0:row0 + SSM_GROUP, :] = (
                p[:, lanes].reshape(SSM_GROUPS, SSM_GROUP, BLK_LANES))
            g = sgt[:, lanes]
            gt_ref[s * L_PAIR + l] = (g * _sigmoid(g)).astype(BF16)

    hs = [stage_norm(s) for s in range(OUT_STREAMS)]
    for s in range(OUT_STREAMS):
        stage_proj(s, hs[s])


def _token_spec(natural):
    n_l = OUT_STREAMS * L_PAIR
    if natural:
        assert n_l == 8
        return pl.BlockSpec((None, N_SEG, SUBS_PER_SEG, n_l, D_MODEL), lambda b, p: (b, 0, 0, p, 0))
    return pl.BlockSpec((None, n_l, N_SEG, SUBS_PER_SEG, D_MODEL), lambda b, p: (b, p, 0, 0, 0))


def _s5_in(x, mod_all, gain_all, w_ssm_t, qperm, layer, natural):
    bsz = x.shape[0]
    d = D_MODEL
    n_l = OUT_STREAMS * L_PAIR
    out_shape = [jax.ShapeDtypeStruct((SSM_GROUPS, S5_ROW, bsz * BLK_LANES), BF16),
                 jax.ShapeDtypeStruct((bsz, SUB, d, BLK_LANES), BF16)]
    out_specs = [pl.BlockSpec((SSM_GROUPS, n_l * SSM_GROUP, BLK_LANES), lambda b, p: (0, p, b)),
                 pl.BlockSpec((None, n_l, d, BLK_LANES), lambda b, p: (b, p, 0, 0))]
    if natural:
        out_shape.append(jax.ShapeDtypeStruct((bsz, SUB, N_SEG, SUBS_PER_SEG, d), F32))
        out_specs.append(_token_spec(False))
    return pl.pallas_call(
        functools.partial(_s5_in_kernel, natural=natural),
        out_shape=tuple(out_shape),
        grid=(bsz, SUB // n_l),
        in_specs=[
            _token_spec(natural),
            pl.BlockSpec((None, None, 1, 3 * d), lambda b, p: (layer, b, 0, 0)),
            pl.BlockSpec((None, 1, d), lambda b, p: (layer, 0, 0)),
            pl.BlockSpec((None, d, d), lambda b, p: (layer, 0, 0)),
            pl.BlockSpec((None, d, d), lambda b, p: (layer, 1, 0)),
            pl.BlockSpec((L_PAIR * BLK_LANES, L_PAIR * BLK_LANES), lambda b, p: (0, 0)),
        ],
        out_specs=tuple(out_specs),
        compiler_params=pltpu.CompilerParams(
            dimension_semantics=("parallel", "parallel"),
            vmem_limit_bytes=VMEM_LIMIT),
        name="s5_in",
    )(x, mod_all, gain_all, w_ssm_t, w_ssm_t, qperm)


def _s5_core_kernel(zt_ref, toept_ref, ws_ref, wot_ref, atab_ref, seg_ref, dcol_ref,
                    yt_ref, loc_ref, sprev_ref, *, bsz):
    gs = S5_GROUPS_PER_STEP
    half = STATE_HALF
    row8 = lax.broadcasted_iota(jnp.int32, (N_SEG, half), 0)

    def shift_down(v, n):
        return jnp.where(row8 >= n, pltpu.roll(v, n, axis=0), 0.0)

    for j in range(gs):
        loc_ref[j] = _dot_tn(zt_ref[j], ws_ref[j])

    for j in range(gs):
        a_re = atab_ref[j, 0:8, :]
        a_im = atab_ref[j, 8:16, :]
        for b in range(bsz):
            base = b * BLK_LANES

            def rows(kl, base=base):
                return pl.ds(base + kl * N_SEG, N_SEG)

            e_re = jnp.zeros((N_SEG, half), F32)
            e_im = jnp.zeros((N_SEG, half), F32)
            for kl in range(SUBS_PER_SEG):
                e_re, e_im = _cmul_add(a_re, a_im, e_re, e_im,
                                       loc_ref[j, rows(kl), 0:half],
                                       loc_ref[j, rows(kl), half:STATE_LANES])
            for i, n in enumerate((1, 2, 4)):
                p_re = seg_ref[j, i:i + 1, :]
                p_im = seg_ref[j, 3 + i:4 + i, :]
                e_re, e_im = _cmul_add(p_re, p_im, shift_down(e_re, n), shift_down(e_im, n),
                                       e_re, e_im)
            s_re = shift_down(e_re, 1)
            s_im = shift_down(e_im, 1)
            for kl in range(SUBS_PER_SEG):
                sprev_ref[j, rows(kl), 0:half] = s_re
                sprev_ref[j, rows(kl), half:STATE_LANES] = s_im
                s_re, s_im = _cmul_add(a_re, a_im, s_re, s_im,
                                       loc_ref[j, rows(kl), 0:half],
                                       loc_ref[j, rows(kl), half:STATE_LANES])

    for j in range(gs):
        zt = zt_ref[j]
        out = (_dot(toept_ref[j], zt)
               + _dot_nt(wot_ref[j], sprev_ref[j].astype(BF16))
               + dcol_ref[j] * zt.astype(F32))
        yt_ref[j] = out.astype(BF16)


def _s5_core(zt, toept, ws, wot, atab, segtab, dcol, bsz, layer):
    g, rows, lanes = zt.shape
    gs = S5_GROUPS_PER_STEP
    off = layer * (g // gs)
    wspec = pl.BlockSpec((gs, S5_ROW, STATE_LANES), lambda i: (off + i, 0, 0))
    return pl.pallas_call(
        functools.partial(_s5_core_kernel, bsz=bsz),
        out_shape=jax.ShapeDtypeStruct((g, rows, lanes), BF16),
        grid=(g // gs,),
        in_specs=[
            pl.BlockSpec((gs, rows, lanes), lambda i: (i, 0, 0)),
            wspec, wspec, wspec,
            pl.BlockSpec((gs, 16, STATE_HALF), lambda i: (off + i, 0, 0)),
            pl.BlockSpec((gs, 8, STATE_HALF), lambda i: (off + i, 0, 0)),
            pl.BlockSpec((gs, S5_ROW, 1), lambda i: (off + i, 0, 0)),
        ],
        out_specs=pl.BlockSpec((gs, rows, lanes), lambda i: (i, 0, 0)),
        scratch_shapes=[pltpu.VMEM((gs, lanes, STATE_LANES), F32),
                        pltpu.VMEM((gs, lanes, STATE_LANES), F32)],
        compiler_params=pltpu.CompilerParams(
            dimension_semantics=("parallel",),
            vmem_limit_bytes=VMEM_LIMIT),
        name="s5_core",
    )(zt, toept, ws, wot, atab, segtab, dcol)


def _log_sigmoid(x):
    return -(jnp.maximum(-x, 0.0) + jnp.log1p(jnp.exp(-jnp.abs(x))))


def _chunk_time(idx):
    return ((idx & (SUBS_PER_SEG - 1)) * SUB) | (idx >> 4)


M_CHAINS = 4


def _mlstm_kernel(x_ref, mod_ref, gain_ref, w_min_ref, w_mo_ref, w_mg_ref, cw_ref, cb_ref,
                   wq_ref, wk_ref, wv_ref, wg_ref, bg_ref, ng_ref, skip_ref, o_ref,
                   ct_ref, m_ref, tail_ref, ext_ref, qkv_ref, proj_ref, xc_ref, hn_ref, cvb_ref):
    L, dh, d = SEG_LEN, HEAD_DIM, D_MODEL
    halo = (CONV_WIDTH - 1) * SUBS_PER_SEG

    @pl.when(pl.program_id(1) == 0)
    def _():
        ct_ref[...] = jnp.zeros_like(ct_ref)
        m_ref[...] = jnp.zeros_like(m_ref)
        tail_ref[...] = jnp.zeros_like(tail_ref)

    t_row = _chunk_time(lax.broadcasted_iota(jnp.int32, (L, L), 0))
    t_col = _chunk_time(lax.broadcasted_iota(jnp.int32, (L, L), 1))
    causal = t_col <= t_row
    tri_l = jnp.where(causal, 1.0, 0.0).astype(BF16)
    lane = lax.broadcasted_iota(jnp.int32, (L, 128), 1)
    ones_blk = jnp.where(lane == 0, 1.0, 0.0).astype(BF16)
    row16 = lax.broadcasted_iota(jnp.int32, (SUBS_PER_SEG, d), 0)

    def stage_norm_in(c):
        x = x_ref[c].reshape(L, d)
        hn_ref[c] = _modulated_norm(x, gain_ref[...], mod_ref[c]).astype(BF16)
        ext_ref[c, halo:halo + L, :] = _dot(hn_ref[c], w_min_ref[...])

    def stage_gate_proj(c):
        proj_ref[c, :, 0:d] = _dot(hn_ref[c], w_mo_ref[...]).astype(BF16)
        proj_ref[c, :, d:2 * d] = _dot(hn_ref[c], w_mg_ref[...]).astype(BF16)

    def stage_conv(c):
        m_in = ext_ref[c, halo:halo + L, :]
        cvb_ref[c, :, d:2 * d] = m_in.astype(BF16)
        for i in range(CONV_WIDTH - 1):
            lsrc = SUB - (CONV_WIDTH - 1) + i
            r0 = halo + lsrc * SUBS_PER_SEG
            shifted = ext_ref[c, pl.ds(r0 - 1, SUBS_PER_SEG), :]
            ext_ref[c, i * SUBS_PER_SEG:(i + 1) * SUBS_PER_SEG, :] = jnp.where(
                row16 == 0, tail_ref[c, i:i + 1, :], shifted)
            tail_ref[c, i:i + 1, :] = ext_ref[c, r0 + SUBS_PER_SEG - 1:r0 + SUBS_PER_SEG, :]
        acc = m_in * cw_ref[CONV_WIDTH - 1:CONV_WIDTH, :] + cb_ref[...]
        for j in range(CONV_WIDTH - 1):
            back = CONV_WIDTH - 1 - j
            acc = acc + ext_ref[c, pl.ds(halo - back * SUBS_PER_SEG, L), :] * cw_ref[j:j + 1, :]
        xc = acc * _sigmoid(acc)
        xc_ref[c] = xc
        cvb_ref[c, :, 0:d] = xc.astype(BF16)

    def stage_qkv(c):
        for hd in range(HEADS):
            sl = slice(hd * dh, (hd + 1) * dh)
            xcb = cvb_ref[c, :, hd * dh:(hd + 1) * dh]
            qkv_ref[c, :, hd * dh:(hd + 1) * dh] = _dot(xcb, wq_ref[hd]).astype(BF16)
            qkv_ref[c, :, d + hd * dh:d + (hd + 1) * dh] = (
                _dot(xcb, wk_ref[hd]) * (dh ** -0.5)).astype(BF16)
            qkv_ref[c, :, 2 * d + hd * dh:2 * d + (hd + 1) * dh] = (
                _dot(cvb_ref[c, :, d + hd * dh:d + (hd + 1) * dh], wv_ref[hd]).astype(BF16))
        gates = _dot(qkv_ref[c], wg_ref[...]) + bg_ref[...]
        log_f = _log_sigmoid(gates)
        lf_hi, lf_lo = _split_hi_lo(log_f)
        b_cols = _dot(tri_l, lf_hi) + _dot(tri_l, lf_lo)
        return gates.T, b_cols.T, gates, b_cols

    def stage_head(c, hd, gate_forms):
        gates_t, b_rows, gates, b_cols = gate_forms
        sl = slice(hd * dh, (hd + 1) * dh)
        i_r = gates_t[hd:hd + 1, :]
        b_r = b_rows[HEADS + hd:HEADS + hd + 1, :]
        i_c = gates[:, hd:hd + 1]
        b_c = b_cols[:, HEADS + hd:HEADS + hd + 1]
        m_prev = m_ref[c, hd:hd + 1, 0:1]

        log_d = jnp.where(causal, b_c + (i_r - b_r), -jnp.inf)
        m_inter = b_c + m_prev
        m_t = jnp.maximum(m_inter, jnp.max(log_d, axis=-1, keepdims=True))
        dmat = jnp.exp(log_d - m_t)
        qb = qkv_ref[c, :, sl]
        kb = qkv_ref[c, :, d + hd * dh:d + (hd + 1) * dh]
        s = _dot_nt(qb, kb)
        p = (s * dmat).astype(BF16)
        v_aug = jnp.concatenate([qkv_ref[c, :, 2 * d + hd * dh:2 * d + (hd + 1) * dh], ones_blk],
                                axis=-1)
        w_inter = jnp.exp(m_inter - m_t)
        inter = _dot(qb, ct_ref[c, hd].astype(BF16))
        intra = _dot(p, v_aug)
        nd = w_inter * inter + intra
        num = nd[:, 0:dh]
        den = nd[:, dh:dh + 1]
        hh = num / jnp.maximum(jnp.abs(den), jnp.exp(-m_t))

        b_tot = b_c[L - 1:L, :]
        log_w = b_tot - b_c + i_c
        m_next = jnp.maximum(b_tot + m_prev, jnp.max(log_w, axis=0, keepdims=True))
        decay = jnp.exp(b_tot + m_prev - m_next)
        w_c = jnp.exp(log_w - m_next)
        upd = _dot(kb.T, (w_c * v_aug.astype(F32)).astype(BF16))
        ct_ref[c, hd] = decay * ct_ref[c, hd] + upd
        m_ref[c, hd:hd + 1, :] = jnp.broadcast_to(m_next, (1, 128))

        hg = hh * _sigmoid(proj_ref[c, :, sl].astype(F32))
        mu = jnp.mean(hg, axis=-1, keepdims=True)
        dev = hg - mu
        var = jnp.mean(dev * dev, axis=-1, keepdims=True)
        hn = dev * lax.rsqrt(var + EPS) * ng_ref[:, sl] + skip_ref[:, sl] * xc_ref[c, :, sl]
        gate = proj_ref[c, :, d + hd * dh:d + (hd + 1) * dh].astype(F32)
        y = hn * (gate * _sigmoid(gate))
        o_ref[c, :, :, sl] = y.reshape(SUB, SUBS_PER_SEG, dh).astype(o_ref.dtype)

    chains = range(M_CHAINS)
    for c in chains:
        stage_norm_in(c)
    for c in chains:
        stage_conv(c)
    early, late = chains[:M_CHAINS // 2], chains[M_CHAINS // 2:]
    assert len(late) * 2 == HEADS
    gfs = {}
    for c in early:
        stage_gate_proj(c)
    for c in early:
        gfs[c] = stage_qkv(c)
    for hd in range(HEADS):
        for c in early:
            stage_head(c, hd, gfs[c])
        c_late = late[hd % len(late)]
        if hd < len(late):
            stage_gate_proj(c_late)
        else:
            gfs[c_late] = stage_qkv(c_late)
    for hd in range(HEADS):
        for c in late:
            stage_head(c, hd, gfs[c])


def _mlstm(x, mod_all, gain_all, w_in_b, cw, cb, wq, wk, wv, wg, bg, ng, skip, layer):
    bsz = x.shape[0]
    d = D_MODEL
    L = SEG_LEN
    nc = M_CHAINS
    once = pl.Buffered(1)
    lay2 = lambda shape: pl.BlockSpec((None,) + shape, lambda b, c: (layer, 0, 0),
                                      pipeline_mode=once)
    lay3 = lambda shape: pl.BlockSpec((None,) + shape, lambda b, c: (layer, 0, 0, 0),
                                      pipeline_mode=once)
    seg_spec = pl.BlockSpec((nc, SUB, None, SUBS_PER_SEG, d), lambda b, c: (b, 0, c, 0, 0))
    w_cols = lambda j: pl.BlockSpec((None, d, d), lambda b, c: (layer, 0, j), pipeline_mode=once)
    halo = (CONV_WIDTH - 1) * SUBS_PER_SEG
    return pl.pallas_call(
        _mlstm_kernel,
        out_shape=jax.ShapeDtypeStruct((bsz, SUB, N_SEG, SUBS_PER_SEG, d), BF16),
        grid=(bsz // nc, N_SEG),
        in_specs=[
            seg_spec,
            pl.BlockSpec((None, nc, 1, 3 * d), lambda b, c: (layer, b, 0, 0)),
            lay2((1, d)),
            w_cols(2), w_cols(3), w_cols(4),
            lay2((CONV_WIDTH, d)),
            lay2((1, d)),
            lay3((HEADS, HEAD_DIM, HEAD_DIM)),
            lay3((HEADS, HEAD_DIM, HEAD_DIM)),
            lay3((HEADS, HEAD_DIM, HEAD_DIM)),
            lay2((3 * d, 128)),
            lay2((1, 128)),
            lay2((1, d)),
            lay2((1, d)),
        ],
        out_specs=seg_spec,
        scratch_shapes=[
            pltpu.VMEM((nc, HEADS, HEAD_DIM, M_AUG), F32),
            pltpu.VMEM((nc, 8, 128), F32),
            pltpu.VMEM((nc, 8, d), F32),
            pltpu.VMEM((nc, halo + L, d), F32),
            pltpu.VMEM((nc, L, 3 * d), BF16),
            pltpu.VMEM((nc, L, 2 * d), BF16),
            pltpu.VMEM((nc, L, d), F32),
            pltpu.VMEM((nc, L, d), BF16),
            pltpu.VMEM((nc, L, 2 * d), BF16),
        ],
        compiler_params=pltpu.CompilerParams(
            dimension_semantics=("parallel", "arbitrary"),
            vmem_limit_bytes=MLSTM_VMEM_LIMIT),
        name="mlstm",
    )(x, mod_all, gain_all, w_in_b, w_in_b, w_in_b, cw, cb, wq, wk, wv, wg, bg, ng, skip)


def _gelu_tanh(x):
    inner = math.sqrt(2.0 / math.pi) * (x + 0.044715 * (x * x * x))
    return x * (0.5 * (1.0 + jnp.tanh(inner)))


def _out_kernel(yt_ref, gt_ref, my_ref, x_ref, mod_ref, bglu_ref, og_ref, fg_ref,
                wglut_ref, wout_ref, qt_ref, o_ref, *, final):
    d = D_MODEL
    rows = L_PAIR * BLK_LANES
    x_rows = _stream_rows(x_ref, natural=False)

    def stage_in(s):
        ls = range(s * L_PAIR, (s + 1) * L_PAIR)
        out_m = _dot(my_ref[s * L_PAIR:(s + 1) * L_PAIR].reshape(rows, d),
                     wout_ref[d:2 * d, :])
        yt = jnp.concatenate(
            [yt_ref[:, l * SSM_GROUP:(l + 1) * SSM_GROUP, :].reshape(d, BLK_LANES) for l in ls],
            axis=-1)
        y = _gelu_tanh(_dot(yt, qt_ref[...]))
        return out_m, y

    def stage_glu(s, y):
        ls = range(s * L_PAIR, (s + 1) * L_PAIR)
        glu = y * _sigmoid(_dot(wglut_ref[...], y.astype(BF16)) + bglu_ref[...])
        gate_s = jnp.concatenate([gt_ref[l] for l in ls], axis=-1).astype(F32)
        ms = jnp.mean(glu * glu, axis=0, keepdims=True)
        return (glu * lax.rsqrt(ms + EPS) * og_ref[...] * gate_s).astype(BF16)

    def stage_out(s, out_m, ssm_y):
        out = _dot_tn(ssm_y, wout_ref[0:d, :]) + out_m
        xn = x_rows(s) + mod_ref[:, 2 * d:3 * d] * out
        if final:
            return _rms(xn) * fg_ref[...]
        o_ref[s * L_PAIR:(s + 1) * L_PAIR] = xn.reshape(L_PAIR, N_SEG, SUBS_PER_SEG, d)

    streams = range(OUT_STREAMS)
    ins = [stage_in(s) for s in streams]
    ssm = [stage_glu(s, ins[s][1]) for s in streams]
    outs = [stage_out(s, ins[s][0], ssm[s]) for s in streams]
    if final:
        n_l = OUT_STREAMS * L_PAIR
        by_l = jnp.concatenate(outs, axis=0).reshape(n_l, BLK_LANES, d)
        o_ref[...] = jnp.swapaxes(by_l, 0, 1).reshape(N_SEG, SUBS_PER_SEG, n_l, d)


def _out_stage(yt, gt, my, x, mod_all, bglu_col, og_col, fg, wglut, wout, qperm_t, layer, final):
    bsz = x.shape[0]
    d = D_MODEL
    n_l = OUT_STREAMS * L_PAIR
    col = pl.BlockSpec((None, d, 1), lambda b, p: (layer, 0, 0))
    out_dims = (bsz, N_SEG, SUBS_PER_SEG, SUB, d) if final else (bsz, SUB, N_SEG, SUBS_PER_SEG, d)
    return pl.pallas_call(
        functools.partial(_out_kernel, final=final),
        out_shape=jax.ShapeDtypeStruct(out_dims, F32),
        grid=(bsz, SUB // n_l),
        in_specs=[
            pl.BlockSpec((SSM_GROUPS, n_l * SSM_GROUP, BLK_LANES), lambda b, p: (0, p, b)),
            pl.BlockSpec((None, n_l, d, BLK_LANES), lambda b, p: (b, p, 0, 0)),
            _token_spec(False), _token_spec(False),
            pl.BlockSpec((None, None, 1, 3 * d), lambda b, p: (layer, b, 0, 0)),
            col, col,
            pl.BlockSpec((1, d), lambda b, p: (0, 0)),
            pl.BlockSpec((None, d, d), lambda b, p: (layer, 0, 0)),
            pl.BlockSpec((None, 2 * d, d), lambda b, p: (layer, 0, 0)),
            pl.BlockSpec((L_PAIR * BLK_LANES, L_PAIR * BLK_LANES), lambda b, p: (0, 0)),
        ],
        out_specs=_token_spec(final),
        compiler_params=pltpu.CompilerParams(
            dimension_semantics=("parallel", "parallel"),
            vmem_limit_bytes=VMEM_LIMIT),
        name="out_stage",
    )(yt, gt, my, x, mod_all, bglu_col, og_col, fg, wglut, wout, qperm_t)


def _lane_permutation():
    q = np.zeros((BLK_LANES, BLK_LANES), np.float32)
    for seg in range(N_SEG):
        for kl in range(SUBS_PER_SEG):
            q[seg * SUBS_PER_SEG + kl, kl * N_SEG + seg] = 1.0
    return np.kron(np.eye(L_PAIR, dtype=np.float32), q)


def kernel(x, c, norm_gain, w_mod, b_mod, w_in, ssm_lambda_re, ssm_lambda_im, ssm_log_dt,
           ssm_b_re, ssm_b_im, ssm_c_re, ssm_c_im, ssm_d, ssm_w_glu, ssm_b_glu,
           ssm_out_gain, m_conv_w, m_conv_b, m_wq, m_wk, m_wv, m_w_gates, m_b_igate,
           m_b_fgate, m_norm_gain, m_skip, w_out, final_gain):
    bsz, seq, d = x.shape
    depth = w_in.shape[0]
    assert d == D_MODEL and seq == SEQ

    h = x.reshape(bsz, N_SEG, SUBS_PER_SEG, SUB, d)
    mod_all = _modulation(c, w_mod, b_mod).reshape(depth, bsz, 1, 3 * d)
    rows = lambda v: v.reshape(depth, 1, -1)
    cols = lambda v: v.reshape(depth, -1, 1)
    qperm = jnp.asarray(_lane_permutation(), BF16)
    qperm_t = jnp.asarray(_lane_permutation().T, BF16)

    gain_all = rows(norm_gain)
    w_in_b = w_in.astype(BF16)
    w_ssm_t = w_in_b[:, :, 0:2 * d].transpose(0, 2, 1)
    toept, ws, wot, atab, segtab = _s5_prep(ssm_lambda_re, ssm_lambda_im, ssm_log_dt,
                                            ssm_b_re, ssm_b_im, ssm_c_re, ssm_c_im)
    dcol = jnp.broadcast_to(ssm_d.reshape(depth * SSM_GROUPS, 1, SSM_GROUP),
                            (depth * SSM_GROUPS, SUB, SSM_GROUP)).reshape(-1, S5_ROW, 1)
    wq, wk, wv = m_wq.astype(BF16), m_wk.astype(BF16), m_wv.astype(BF16)
    gate_pad = ((0, 0), (0, 0), (0, 128 - 2 * HEADS))
    wg = jnp.pad(m_w_gates, gate_pad).astype(BF16)
    bg = jnp.pad(jnp.concatenate([m_b_igate, m_b_fgate], axis=-1).reshape(depth, 1, 2 * HEADS),
                 gate_pad)
    wglut = ssm_w_glu.astype(BF16).transpose(0, 2, 1)
    wout = w_out.astype(BF16)
    fg = final_gain.reshape(1, d)

    for l in range(depth):
        final = l == depth - 1
        if l == 0:
            zt, gt, h = _s5_in(h, mod_all, gain_all, w_ssm_t, qperm, l, natural=True)
        else:
            zt, gt = _s5_in(h, mod_all, gain_all, w_ssm_t, qperm, l, natural=False)
        yt = _s5_core(zt, toept, ws, wot, atab, segtab, dcol, bsz, l)
        my = _mlstm(h, mod_all, gain_all, w_in_b, m_conv_w, rows(m_conv_b), wq, wk, wv, wg, bg,
                    rows(m_norm_gain), rows(m_skip), l)
        h = _out_stage(yt, gt, my, h, mod_all, cols(ssm_b_glu), cols(ssm_out_gain), fg,
                       wglut, wout, qperm_t, l, final)

    return h.reshape(bsz, seq, d)
```

```python
import functools
import math

import numpy as np
import jax
import jax.numpy as jnp
from jax import lax
from jax.experimental import pallas as pl
from jax.experimental.pallas import tpu as pltpu

F32 = jnp.float32
BF16 = jnp.bfloat16

D_MODEL = 1024
SSM_GROUP = 16
SSM_GROUPS = D_MODEL // SSM_GROUP
SSM_STATE = 64
HEADS = 4
HEAD_DIM = D_MODEL // HEADS
CONV_WIDTH = 4
EPS = 1e-6

SUB = 16
SUBS_PER_SEG = 16
SEG_LEN = SUB * SUBS_PER_SEG
N_SEG = 8
SEQ = N_SEG * SEG_LEN
BLK_LANES = N_SEG * SUBS_PER_SEG
S5_ROW = SUB * SSM_GROUP
STATE_HALF = 128
STATE_LANES = 2 * STATE_HALF
S5_GROUPS_PER_STEP = 8
L_PAIR = 2
OUT_STREAMS = 4
M_AUG = HEAD_DIM + 128
VMEM_LIMIT = 56 * 1024 * 1024
MLSTM_VMEM_LIMIT = 60 * 1024 * 1024

_NT = (((1,), (1,)), ((), ()))
_TN = (((0,), (0,)), ((), ()))


def _dot(a, b):
    return jnp.dot(a, b, preferred_element_type=F32)


def _dot_nt(a, b):
    return lax.dot_general(a, b, _NT, preferred_element_type=F32)


def _dot_tn(a, b):
    return lax.dot_general(a, b, _TN, preferred_element_type=F32)


def _sigmoid(x):
    return jax.nn.sigmoid(x)


def _rms(x):
    return x * lax.rsqrt(jnp.mean(x * x, axis=-1, keepdims=True) + EPS)


def _modulated_norm(x, gain, mod):
    shift = mod[:, 0:D_MODEL]
    scale = mod[:, D_MODEL:2 * D_MODEL]
    return _rms(x) * (gain * (1.0 + scale)) + shift


def _split_hi_lo(v):
    hi = v.astype(BF16)
    lo = (v - hi.astype(F32)).astype(BF16)
    return hi, lo


def _dot_nt_f32(a, b):
    a_hi, a_lo = _split_hi_lo(a)
    b_hi, b_lo = _split_hi_lo(b)
    return _dot_nt(a_hi, b_hi) + _dot_nt(a_hi, b_lo) + _dot_nt(a_lo, b_hi)


def _cmul(a_re, a_im, b_re, b_im):
    return a_re * b_re - a_im * b_im, a_re * b_im + a_im * b_re


def _cmul_add(p_re, p_im, s_re, s_im, add_re, add_im):
    return (p_re * s_re - p_im * s_im + add_re,
            p_re * s_im + p_im * s_re + add_im)


def _mod_kernel(c_ref, w_ref, b_ref, o_ref):
    cv = c_ref[...]
    act = cv * _sigmoid(cv)
    o_ref[...] = _dot(act.astype(BF16), w_ref[...].astype(BF16)) + b_ref[...]


def _modulation(c, w_mod, b_mod):
    depth, d, n = w_mod.shape
    bsz = c.shape[0]
    return pl.pallas_call(
        _mod_kernel,
        out_shape=jax.ShapeDtypeStruct((depth, bsz, n), F32),
        grid=(depth, n // d),
        in_specs=[
            pl.BlockSpec((bsz, d), lambda l, j: (0, 0)),
            pl.BlockSpec((None, d, d), lambda l, j: (l, 0, j)),
            pl.BlockSpec((None, 1, d), lambda l, j: (l, 0, j)),
        ],
        out_specs=pl.BlockSpec((None, bsz, d), lambda l, j: (l, 0, j)),
        compiler_params=pltpu.CompilerParams(
            dimension_semantics=("parallel", "parallel"),
            vmem_limit_bytes=VMEM_LIMIT),
        name="adaln_mod",
    )(c, w_mod, b_mod.reshape(depth, 1, n))


def _s5_prep_kernel(lre_ref, lim_ref, ldt_ref, btre_ref, btim_ref, cre_ref, cim_ref, dlag_ref,
                    toept_ref, ws_ref, wot_ref, atab_ref, seg_ref):
    gp = S5_GROUPS_PER_STEP
    lane_blk = lax.broadcasted_iota(jnp.int32, (S5_ROW, S5_ROW), 1) // SSM_GROUP
    row8 = lax.broadcasted_iota(jnp.int32, (8, STATE_HALF), 0)
    for j in range(gp):
        lr = lre_ref[j]
        li = lim_ref[j]
        dt = jnp.exp(ldt_ref[j])
        mag = jnp.exp(lr * dt)
        a_re = mag * jnp.cos(li * dt)
        a_im = mag * jnp.sin(li * dt)
        inv = 1.0 / (lr * lr + li * li)
        k_re = ((a_re - 1.0) * lr + a_im * li) * inv
        k_im = (a_im * lr - (a_re - 1.0) * li) * inv
        bb_re, bb_im = _cmul(k_re, k_im, btre_ref[j], btim_ref[j])
        c_re = cre_ref[j]
        c_im = cim_ref[j]

        pw = [(jnp.ones_like(a_re), jnp.zeros_like(a_re))]
        for _ in range(SUB):
            pw.append(_cmul(pw[-1][0], pw[-1][1], a_re, a_im))

        ws_rows, wot_rows, ca_re, ca_im = [], [], [], []
        for l in range(SUB):
            w_re, w_im = _cmul(pw[SUB - 1 - l][0], pw[SUB - 1 - l][1], bb_re, bb_im)
            ws_rows.append(jnp.concatenate([w_re, w_im], axis=-1))
            o_re, o_im = _cmul(c_re, c_im, pw[l + 1][0], pw[l + 1][1])
            wot_rows.append(jnp.concatenate([o_re, -o_im], axis=-1))
            g_re, g_im = _cmul(c_re, c_im, pw[l][0], pw[l][1])
            ca_re.append(g_re)
            ca_im.append(g_im)
        ws_ref[j] = jnp.concatenate(ws_rows, axis=0).astype(BF16)
        wot_ref[j] = jnp.concatenate(wot_rows, axis=0).astype(BF16)

        bbt_re = jnp.concatenate([bb_re] * SUB, axis=0)
        bbt_im = jnp.concatenate([bb_im] * SUB, axis=0)
        kw = (_dot_nt_f32(jnp.concatenate(ca_re, axis=0), bbt_re)
              - _dot_nt_f32(jnp.concatenate(ca_im, axis=0), bbt_im))
        kw = kw + jnp.concatenate(
            [dlag_ref[j], jnp.zeros((S5_ROW - SSM_GROUP, S5_ROW), F32)], axis=0)
        toep = jnp.where(lane_blk == 0, kw, 0.0)
        for lp in range(1, SUB):
            shifted = jnp.concatenate(
                [jnp.zeros((SSM_GROUP * lp, S5_ROW), F32), kw[0:S5_ROW - SSM_GROUP * lp]], axis=0)
            toep = jnp.where(lane_blk == lp, shifted, toep)
        toept_ref[j] = toep.astype(BF16)

        s_re, s_im = pw[SUB]
        atab_ref[j] = jnp.concatenate([jnp.broadcast_to(s_re, (8, STATE_HALF)),
                                       jnp.broadcast_to(s_im, (8, STATE_HALF))], axis=0)
        for _ in range(4):
            s_re, s_im = _cmul(s_re, s_im, s_re, s_im)
        seg = jnp.zeros((8, STATE_HALF), F32)
        for i in range(3):
            seg = jnp.where(row8 == i, s_re, jnp.where(row8 == 3 + i, s_im, seg))
            s_re, s_im = _cmul(s_re, s_im, s_re, s_im)
        seg_ref[j] = seg


def _s5_prep(lam_re, lam_im, log_dt, b_re, b_im, c_re, c_im, d_skip):
    depth, g, p = lam_re.shape
    cg = SSM_GROUP
    n = depth * g
    gp = S5_GROUPS_PER_STEP
    lane_pad = ((0, 0), (0, 0), (0, STATE_HALF - p))

    def state_rows(v, fill):
        v = v.reshape(n, -1, p)
        return jnp.pad(v, lane_pad, constant_values=fill)

    args = (state_rows(lam_re, -1.0), state_rows(lam_im, 0.0), log_dt.reshape(n, 1, 1),
            state_rows(b_re.transpose(0, 1, 3, 2), 0.0), state_rows(b_im.transpose(0, 1, 3, 2), 0.0),
            state_rows(c_re, 0.0), state_rows(c_im, 0.0))
    eye = jnp.eye(cg, dtype=d_skip.dtype)
    dlag = jnp.tile(d_skip.reshape(n, cg, 1) * eye[None], (1, 1, SUB))
    args = args + (dlag,)
    vec = pl.BlockSpec((gp, 1, STATE_HALF), lambda i: (i, 0, 0))
    mat = pl.BlockSpec((gp, cg, STATE_HALF), lambda i: (i, 0, 0))
    big = pl.BlockSpec((gp, S5_ROW, STATE_LANES), lambda i: (i, 0, 0))
    return pl.pallas_call(
        _s5_prep_kernel,
        out_shape=(jax.ShapeDtypeStruct((n, S5_ROW, S5_ROW), BF16),
                   jax.ShapeDtypeStruct((n, S5_ROW, STATE_LANES), BF16),
                   jax.ShapeDtypeStruct((n, S5_ROW, STATE_LANES), BF16),
                   jax.ShapeDtypeStruct((n, 16, STATE_HALF), F32),
                   jax.ShapeDtypeStruct((n, 8, STATE_HALF), F32)),
        grid=(n // gp,),
        in_specs=[vec, vec, pl.BlockSpec((gp, 1, 1), lambda i: (i, 0, 0)), mat, mat, mat, mat,
                  pl.BlockSpec((gp, cg, S5_ROW), lambda i: (i, 0, 0))],
        out_specs=(big, big, big,
                   pl.BlockSpec((gp, 16, STATE_HALF), lambda i: (i, 0, 0)),
                   pl.BlockSpec((gp, 8, STATE_HALF), lambda i: (i, 0, 0))),
        compiler_params=pltpu.CompilerParams(
            dimension_semantics=("parallel",),
            vmem_limit_bytes=VMEM_LIMIT),
        name="s5_prep",
    )(*args)


def _stream_rows(x_ref, natural):
    n_l = OUT_STREAMS * L_PAIR
    rows = L_PAIR * BLK_LANES
    if natural:
        by_l = jnp.swapaxes(x_ref[...].reshape(BLK_LANES, n_l, D_MODEL), 0, 1)
        return lambda s: by_l[s * L_PAIR:(s + 1) * L_PAIR].reshape(rows, D_MODEL)
    return lambda s: x_ref[s * L_PAIR:(s + 1) * L_PAIR].reshape(rows, D_MODEL)


def _s5_in_kernel(x_ref, mod_ref, gain_ref, wut_ref, wgt_ref, q_ref, zt_ref, gt_ref,
                  xp_ref=None, *, natural):
    d = D_MODEL
    x_rows = _stream_rows(x_ref, natural)

    def stage_norm(s):
        x = x_rows(s)
        if natural:
            xp_ref[s * L_PAIR:(s + 1) * L_PAIR] = x.reshape(L_PAIR, N_SEG, SUBS_PER_SEG, d)
        return _modulated_norm(x, gain_ref[...], mod_ref[...]).astype(BF16)

    def stage_proj(s, h):
        ut = _dot_nt(wut_ref[...], h).astype(BF16)
        sgt = _dot_nt(wgt_ref[...], h)
        p = _dot(ut, q_ref[...]).astype(BF16)
        for l in range(L_PAIR):
            lanes = slice(l * BLK_LANES, (l + 1) * BLK_LANES)
            row0 = (s * L_PAIR + l) * SSM_GROUP
            zt_ref[:, row0:row0 + SSM_GROUP, :] = (
                p[:, lanes].reshape(SSM_GROUPS, SSM_GROUP, BLK_LANES))
            g = sgt[:, lanes]
            gt_ref[s * L_PAIR + l] = (g * _sigmoid(g)).astype(BF16)

    hs = [stage_norm(s) for s in range(OUT_STREAMS)]
    for s in range(OUT_STREAMS):
        stage_proj(s, hs[s])


def _token_spec(natural):
    n_l = OUT_STREAMS * L_PAIR
    if natural:
        assert n_l == 8
        return pl.BlockSpec((None, N_SEG, SUBS_PER_SEG, n_l, D_MODEL), lambda b, p: (b, 0, 0, p, 0))
    return pl.BlockSpec((None, n_l, N_SEG, SUBS_PER_SEG, D_MODEL), lambda b, p: (b, p, 0, 0, 0))


def _s5_in(x, mod_all, gain_all, w_ssm_t, qperm, layer, natural):
    bsz = x.shape[0]
    d = D_MODEL
    n_l = OUT_STREAMS * L_PAIR
    out_shape = [jax.ShapeDtypeStruct((SSM_GROUPS, S5_ROW, bsz * BLK_LANES), BF16),
                 jax.ShapeDtypeStruct((bsz, SUB, d, BLK_LANES), BF16)]
    out_specs = [pl.BlockSpec((SSM_GROUPS, n_l * SSM_GROUP, BLK_LANES), lambda b, p: (0, p, b)),
                 pl.BlockSpec((None, n_l, d, BLK_LANES), lambda b, p: (b, p, 0, 0))]
    if natural:
        out_shape.append(jax.ShapeDtypeStruct((bsz, SUB, N_SEG, SUBS_PER_SEG, d), F32))
        out_specs.append(_token_spec(False))
    return pl.pallas_call(
        functools.partial(_s5_in_kernel, natural=natural),
        out_shape=tuple(out_shape),
        grid=(bsz, SUB // n_l),
        in_specs=[
            _token_spec(natural),
            pl.BlockSpec((None, None, 1, 3 * d), lambda b, p: (layer, b, 0, 0)),
            pl.BlockSpec((None, 1, d), lambda b, p: (layer, 0, 0)),
            pl.BlockSpec((None, d, d), lambda b, p: (layer, 0, 0)),
            pl.BlockSpec((None, d, d), lambda b, p: (layer, 1, 0)),
            pl.BlockSpec((L_PAIR * BLK_LANES, L_PAIR * BLK_LANES), lambda b, p: (0, 0)),
        ],
        out_specs=tuple(out_specs),
        compiler_params=pltpu.CompilerParams(
            dimension_semantics=("parallel", "parallel"),
            vmem_limit_bytes=VMEM_LIMIT),
        name="s5_in",
    )(x, mod_all, gain_all, w_ssm_t, w_ssm_t, qperm)


def _s5_core_kernel(zt_ref, toept_ref, ws_ref, wot_ref, atab_ref, seg_ref,
                    yt_ref, loc_ref, sprev_ref, *, bsz):
    gs = S5_GROUPS_PER_STEP
    half = STATE_HALF
    row8 = lax.broadcasted_iota(jnp.int32, (N_SEG, half), 0)

    def shift_down(v, n):
        return jnp.where(row8 >= n, pltpu.roll(v, n, axis=0), 0.0)

    for j in range(gs):
        loc_ref[j] = _dot_tn(zt_ref[j], ws_ref[j])

    for j in range(gs):
        a_re = atab_ref[j, 0:8, :]
        a_im = atab_ref[j, 8:16, :]
        for b in range(bsz):
            base = b * BLK_LANES

            def rows(kl, base=base):
                return pl.ds(base + kl * N_SEG, N_SEG)

            e_re = jnp.zeros((N_SEG, half), F32)
            e_im = jnp.zeros((N_SEG, half), F32)
            for kl in range(SUBS_PER_SEG):
                e_re, e_im = _cmul_add(a_re, a_im, e_re, e_im,
                                       loc_ref[j, rows(kl), 0:half],
                                       loc_ref[j, rows(kl), half:STATE_LANES])
            for i, n in enumerate((1, 2, 4)):
                p_re = seg_ref[j, i:i + 1, :]
                p_im = seg_ref[j, 3 + i:4 + i, :]
                e_re, e_im = _cmul_add(p_re, p_im, shift_down(e_re, n), shift_down(e_im, n),
                                       e_re, e_im)
            s_re = shift_down(e_re, 1)
            s_im = shift_down(e_im, 1)
            for kl in range(SUBS_PER_SEG):
                sprev_ref[j, rows(kl), 0:half] = s_re
                sprev_ref[j, rows(kl), half:STATE_LANES] = s_im
                s_re, s_im = _cmul_add(a_re, a_im, s_re, s_im,
                                       loc_ref[j, rows(kl), 0:half],
                                       loc_ref[j, rows(kl), half:STATE_LANES])

    for j in range(gs):
        zt = zt_ref[j]
        out = (_dot(toept_ref[j], zt)
               + _dot_nt(wot_ref[j], sprev_ref[j].astype(BF16)))
        yt_ref[j] = out.astype(BF16)


def _s5_core(zt, toept, ws, wot, atab, segtab, bsz, layer):
    g, rows, lanes = zt.shape
    gs = S5_GROUPS_PER_STEP
    off = layer * (g // gs)
    wspec = pl.BlockSpec((gs, S5_ROW, STATE_LANES), lambda i: (off + i, 0, 0))
    return pl.pallas_call(
        functools.partial(_s5_core_kernel, bsz=bsz),
        out_shape=jax.ShapeDtypeStruct((g, rows, lanes), BF16),
        grid=(g // gs,),
        in_specs=[
            pl.BlockSpec((gs, rows, lanes), lambda i: (i, 0, 0)),
            wspec, wspec, wspec,
            pl.BlockSpec((gs, 16, STATE_HALF), lambda i: (off + i, 0, 0)),
            pl.BlockSpec((gs, 8, STATE_HALF), lambda i: (off + i, 0, 0)),
        ],
        out_specs=pl.BlockSpec((gs, rows, lanes), lambda i: (i, 0, 0)),
        scratch_shapes=[pltpu.VMEM((gs, lanes, STATE_LANES), F32),
                        pltpu.VMEM((gs, lanes, STATE_LANES), F32)],
        compiler_params=pltpu.CompilerParams(
            dimension_semantics=("parallel",),
            vmem_limit_bytes=VMEM_LIMIT),
        name="s5_core",
    )(zt, toept, ws, wot, atab, segtab)


def _log_sigmoid(x):
    return -(jnp.maximum(-x, 0.0) + jnp.log1p(jnp.exp(-jnp.abs(x))))


def _chunk_time(idx):
    return ((idx & (SUBS_PER_SEG - 1)) * SUB) | (idx >> 4)


M_CHAINS = 4


def _mlstm_kernel(x_ref, mod_ref, gain_ref, w_min_ref, w_mo_ref, w_mg_ref, cw_ref, cb_ref,
                   wq_ref, wk_ref, wv_ref, wg_ref, bg_ref, ng_ref, skip_ref, o_ref,
                   ct_ref, m_ref, tail_ref, ext_ref, qkv_ref, proj_ref, xc_ref, hn_ref, cvb_ref):
    L, dh, d = SEG_LEN, HEAD_DIM, D_MODEL
    halo = (CONV_WIDTH - 1) * SUBS_PER_SEG

    @pl.when(pl.program_id(1) == 0)
    def _():
        ct_ref[...] = jnp.zeros_like(ct_ref)
        m_ref[...] = jnp.zeros_like(m_ref)
        tail_ref[...] = jnp.zeros_like(tail_ref)

    t_row = _chunk_time(lax.broadcasted_iota(jnp.int32, (L, L), 0))
    t_col = _chunk_time(lax.broadcasted_iota(jnp.int32, (L, L), 1))
    causal = t_col <= t_row
    tri_l = jnp.where(causal, 1.0, 0.0).astype(BF16)
    lane = lax.broadcasted_iota(jnp.int32, (L, 128), 1)
    ones_blk = jnp.where(lane == 0, 1.0, 0.0).astype(BF16)
    row16 = lax.broadcasted_iota(jnp.int32, (SUBS_PER_SEG, d), 0)

    def stage_norm_in(c):
        x = x_ref[c].reshape(L, d)
        hn_ref[c] = _modulated_norm(x, gain_ref[...], mod_ref[c]).astype(BF16)
        ext_ref[c, halo:halo + L, :] = _dot(hn_ref[c], w_min_ref[...])

    def stage_gate_proj(c):
        proj_ref[c, :, 0:d] = _dot(hn_ref[c], w_mo_ref[...]).astype(BF16)
        proj_ref[c, :, d:2 * d] = _dot(hn_ref[c], w_mg_ref[...]).astype(BF16)

    def stage_conv(c):
        m_in = ext_ref[c, halo:halo + L, :]
        cvb_ref[c, :, d:2 * d] = m_in.astype(BF16)
        for i in range(CONV_WIDTH - 1):
            lsrc = SUB - (CONV_WIDTH - 1) + i
            r0 = halo + lsrc * SUBS_PER_SEG
            shifted = ext_ref[c, pl.ds(r0 - 1, SUBS_PER_SEG), :]
            ext_ref[c, i * SUBS_PER_SEG:(i + 1) * SUBS_PER_SEG, :] = jnp.where(
                row16 == 0, tail_ref[c, i:i + 1, :], shifted)
            tail_ref[c, i:i + 1, :] = ext_ref[c, r0 + SUBS_PER_SEG - 1:r0 + SUBS_PER_SEG, :]
        acc = m_in * cw_ref[CONV_WIDTH - 1:CONV_WIDTH, :] + cb_ref[...]
        for j in range(CONV_WIDTH - 1):
            back = CONV_WIDTH - 1 - j
            acc = acc + ext_ref[c, pl.ds(halo - back * SUBS_PER_SEG, L), :] * cw_ref[j:j + 1, :]
        xc = acc * _sigmoid(acc)
        xc_ref[c] = xc
        cvb_ref[c, :, 0:d] = xc.astype(BF16)

    def stage_qkv(c):
        for hd in range(HEADS):
            sl = slice(hd * dh, (hd + 1) * dh)
            xcb = cvb_ref[c, :, hd * dh:(hd + 1) * dh]
            qkv_ref[c, :, hd * dh:(hd + 1) * dh] = _dot(xcb, wq_ref[hd]).astype(BF16)
            qkv_ref[c, :, d + hd * dh:d + (hd + 1) * dh] = (
                _dot(xcb, wk_ref[hd]) * (dh ** -0.5)).astype(BF16)
            qkv_ref[c, :, 2 * d + hd * dh:2 * d + (hd + 1) * dh] = (
                _dot(cvb_ref[c, :, d + hd * dh:d + (hd + 1) * dh], wv_ref[hd]).astype(BF16))
        gates = _dot(qkv_ref[c], wg_ref[...]) + bg_ref[...]
        log_f = _log_sigmoid(gates)
        lf_hi, lf_lo = _split_hi_lo(log_f)
        b_cols = _dot(tri_l, lf_hi) + _dot(tri_l, lf_lo)
        return gates.T, b_cols.T, gates, b_cols

    def stage_head(c, hd, gate_forms):
        gates_t, b_rows, gates, b_cols = gate_forms
        sl = slice(hd * dh, (hd + 1) * dh)
        i_r = gates_t[hd:hd + 1, :]
        b_r = b_rows[HEADS + hd:HEADS + hd + 1, :]
        i_c = gates[:, hd:hd + 1]
        b_c = b_cols[:, HEADS + hd:HEADS + hd + 1]
        m_prev = m_ref[c, hd:hd + 1, 0:1]

        log_d = jnp.where(causal, b_c + (i_r - b_r), -jnp.inf)
        m_inter = b_c + m_prev
        m_t = jnp.maximum(m_inter, jnp.max(log_d, axis=-1, keepdims=True))
        dmat = jnp.exp(log_d - m_t)
        qb = qkv_ref[c, :, sl]
        kb = qkv_ref[c, :, d + hd * dh:d + (hd + 1) * dh]
        s = _dot_nt(qb, kb)
        p = (s * dmat).astype(BF16)
        v_aug = jnp.concatenate([qkv_ref[c, :, 2 * d + hd * dh:2 * d + (hd + 1) * dh], ones_blk],
                                axis=-1)
        w_inter = jnp.exp(m_inter - m_t)
        inter = _dot(qb, ct_ref[c, hd].astype(BF16))
        intra = _dot(p, v_aug)
        nd = w_inter * inter + intra
        num = nd[:, 0:dh]
        den = nd[:, dh:dh + 1]
        hh = num / jnp.maximum(jnp.abs(den), jnp.exp(-m_t))

        b_tot = b_c[L - 1:L, :]
        log_w = b_tot - b_c + i_c
        m_next = jnp.maximum(b_tot + m_prev, jnp.max(log_w, axis=0, keepdims=True))
        decay = jnp.exp(b_tot + m_prev - m_next)
        w_c = jnp.exp(log_w - m_next)
        upd = _dot(kb.T, (w_c * v_aug.astype(F32)).astype(BF16))
        ct_ref[c, hd] = decay * ct_ref[c, hd] + upd
        m_ref[c, hd:hd + 1, :] = jnp.broadcast_to(m_next, (1, 128))

        hg = hh * _sigmoid(proj_ref[c, :, sl].astype(F32))
        mu = jnp.mean(hg, axis=-1, keepdims=True)
        dev = hg - mu
        var = jnp.mean(dev * dev, axis=-1, keepdims=True)
        hn = dev * lax.rsqrt(var + EPS) * ng_ref[:, sl] + skip_ref[:, sl] * xc_ref[c, :, sl]
        gate = proj_ref[c, :, d + hd * dh:d + (hd + 1) * dh].astype(F32)
        y = hn * (gate * _sigmoid(gate))
        o_ref[c, :, :, sl] = y.reshape(SUB, SUBS_PER_SEG, dh).astype(o_ref.dtype)

    chains = range(M_CHAINS)
    for c in chains:
        stage_norm_in(c)
    for c in chains:
        stage_conv(c)
    early, late = chains[:M_CHAINS // 2], chains[M_CHAINS // 2:]
    assert len(late) * 2 == HEADS
    gfs = {}
    for c in early:
        stage_gate_proj(c)
    for c in early:
        gfs[c] = stage_qkv(c)
    for hd in range(HEADS):
        for c in early:
            stage_head(c, hd, gfs[c])
        c_late = late[hd % len(late)]
        if hd < len(late):
            stage_gate_proj(c_late)
        else:
            gfs[c_late] = stage_qkv(c_late)
    for hd in range(HEADS):
        for c in late:
            stage_head(c, hd, gfs[c])


def _mlstm(x, mod_all, gain_all, w_in_b, cw, cb, wq, wk, wv, wg, bg, ng, skip, layer):
    bsz = x.shape[0]
    d = D_MODEL
    L = SEG_LEN
    nc = M_CHAINS
    once = pl.Buffered(1)
    lay2 = lambda shape: pl.BlockSpec((None,) + shape, lambda b, c: (layer, 0, 0),
                                      pipeline_mode=once)
    lay3 = lambda shape: pl.BlockSpec((None,) + shape, lambda b, c: (layer, 0, 0, 0),
                                      pipeline_mode=once)
    seg_spec = pl.BlockSpec((nc, SUB, None, SUBS_PER_SEG, d), lambda b, c: (b, 0, c, 0, 0))
    w_cols = lambda j: pl.BlockSpec((None, d, d), lambda b, c: (layer, 0, j), pipeline_mode=once)
    halo = (CONV_WIDTH - 1) * SUBS_PER_SEG
    return pl.pallas_call(
        _mlstm_kernel,
        out_shape=jax.ShapeDtypeStruct((bsz, SUB, N_SEG, SUBS_PER_SEG, d), BF16),
        grid=(bsz // nc, N_SEG),
        in_specs=[
            seg_spec,
            pl.BlockSpec((None, nc, 1, 3 * d), lambda b, c: (layer, b, 0, 0)),
            lay2((1, d)),
            w_cols(2), w_cols(3), w_cols(4),
            lay2((CONV_WIDTH, d)),
            lay2((1, d)),
            lay3((HEADS, HEAD_DIM, HEAD_DIM)),
            lay3((HEADS, HEAD_DIM, HEAD_DIM)),
            lay3((HEADS, HEAD_DIM, HEAD_DIM)),
            lay2((3 * d, 128)),
            lay2((1, 128)),
            lay2((1, d)),
            lay2((1, d)),
        ],
        out_specs=seg_spec,
        scratch_shapes=[
            pltpu.VMEM((nc, HEADS, HEAD_DIM, M_AUG), F32),
            pltpu.VMEM((nc, 8, 128), F32),
            pltpu.VMEM((nc, 8, d), F32),
            pltpu.VMEM((nc, halo + L, d), F32),
            pltpu.VMEM((nc, L, 3 * d), BF16),
            pltpu.VMEM((nc, L, 2 * d), BF16),
            pltpu.VMEM((nc, L, d), F32),
            pltpu.VMEM((nc, L, d), BF16),
            pltpu.VMEM((nc, L, 2 * d), BF16),
        ],
        compiler_params=pltpu.CompilerParams(
            dimension_semantics=("parallel", "arbitrary"),
            vmem_limit_bytes=MLSTM_VMEM_LIMIT),
        name="mlstm",
    )(x, mod_all, gain_all, w_in_b, w_in_b, w_in_b, cw, cb, wq, wk, wv, wg, bg, ng, skip)


def _gelu_tanh(x):
    inner = math.sqrt(2.0 / math.pi) * (x + 0.044715 * (x * x * x))
    return x * (0.5 * (1.0 + jnp.tanh(inner)))


def _out_kernel(yt_ref, gt_ref, my_ref, x_ref, mod_ref, bglu_ref, og_ref, fg_ref,
                wglut_ref, wout_ref, qt_ref, o_ref, *, final):
    d = D_MODEL
    rows = L_PAIR * BLK_LANES
    x_rows = _stream_rows(x_ref, natural=False)

    def stage_in(s):
        ls = range(s * L_PAIR, (s + 1) * L_PAIR)
        out_m = _dot(my_ref[s * L_PAIR:(s + 1) * L_PAIR].reshape(rows, d),
                     wout_ref[d:2 * d, :])
        yt = jnp.concatenate(
            [yt_ref[:, l * SSM_GROUP:(l + 1) * SSM_GROUP, :].reshape(d, BLK_LANES) for l in ls],
            axis=-1)
        y = _gelu_tanh(_dot(yt, qt_ref[...]))
        return out_m, y

    def stage_glu(s, y):
        ls = range(s * L_PAIR, (s + 1) * L_PAIR)
        glu = y * _sigmoid(_dot(wglut_ref[...], y.astype(BF16)) + bglu_ref[...])
        gate_s = jnp.concatenate([gt_ref[l] for l in ls], axis=-1).astype(F32)
        ms = jnp.mean(glu * glu, axis=0, keepdims=True)
        return (glu * lax.rsqrt(ms + EPS) * og_ref[...] * gate_s).astype(BF16)

    def stage_out(s, out_m, ssm_y):
        out = _dot_tn(ssm_y, wout_ref[0:d, :]) + out_m
        xn = x_rows(s) + mod_ref[:, 2 * d:3 * d] * out
        if final:
            return _rms(xn) * fg_ref[...]
        o_ref[s * L_PAIR:(s + 1) * L_PAIR] = xn.reshape(L_PAIR, N_SEG, SUBS_PER_SEG, d)

    streams = range(OUT_STREAMS)
    ins = [stage_in(s) for s in streams]
    ssm = [stage_glu(s, ins[s][1]) for s in streams]
    outs = [stage_out(s, ins[s][0], ssm[s]) for s in streams]
    if final:
        n_l = OUT_STREAMS * L_PAIR
        by_l = jnp.concatenate(outs, axis=0).reshape(n_l, BLK_LANES, d)
        o_ref[...] = jnp.swapaxes(by_l, 0, 1).reshape(N_SEG, SUBS_PER_SEG, n_l, d)


def _out_stage(yt, gt, my, x, mod_all, bglu_col, og_col, fg, wglut, wout, qperm_t, layer, final):
    bsz = x.shape[0]
    d = D_MODEL
    n_l = OUT_STREAMS * L_PAIR
    col = pl.BlockSpec((None, d, 1), lambda b, p: (layer, 0, 0))
    out_dims = (bsz, N_SEG, SUBS_PER_SEG, SUB, d) if final else (bsz, SUB, N_SEG, SUBS_PER_SEG, d)
    return pl.pallas_call(
        functools.partial(_out_kernel, final=final),
        out_shape=jax.ShapeDtypeStruct(out_dims, F32),
        grid=(bsz, SUB // n_l),
        in_specs=[
            pl.BlockSpec((SSM_GROUPS, n_l * SSM_GROUP, BLK_LANES), lambda b, p: (0, p, b)),
            pl.BlockSpec((None, n_l, d, BLK_LANES), lambda b, p: (b, p, 0, 0)),
            _token_spec(False), _token_spec(False),
            pl.BlockSpec((None, None, 1, 3 * d), lambda b, p: (layer, b, 0, 0)),
            col, col,
            pl.BlockSpec((1, d), lambda b, p: (0, 0)),
            pl.BlockSpec((None, d, d), lambda b, p: (layer, 0, 0)),
            pl.BlockSpec((None, 2 * d, d), lambda b, p: (layer, 0, 0)),
            pl.BlockSpec((L_PAIR * BLK_LANES, L_PAIR * BLK_LANES), lambda b, p: (0, 0)),
        ],
        out_specs=_token_spec(final),
        compiler_params=pltpu.CompilerParams(
            dimension_semantics=("parallel", "parallel"),
            vmem_limit_bytes=VMEM_LIMIT),
        name="out_stage",
    )(yt, gt, my, x, mod_all, bglu_col, og_col, fg, wglut, wout, qperm_t)


def _lane_permutation():
    q = np.zeros((BLK_LANES, BLK_LANES), np.float32)
    for seg in range(N_SEG):
        for kl in range(SUBS_PER_SEG):
            q[seg * SUBS_PER_SEG + kl, kl * N_SEG + seg] = 1.0
    return np.kron(np.eye(L_PAIR, dtype=np.float32), q)


def kernel(x, c, norm_gain, w_mod, b_mod, w_in, ssm_lambda_re, ssm_lambda_im, ssm_log_dt,
           ssm_b_re, ssm_b_im, ssm_c_re, ssm_c_im, ssm_d, ssm_w_glu, ssm_b_glu,
           ssm_out_gain, m_conv_w, m_conv_b, m_wq, m_wk, m_wv, m_w_gates, m_b_igate,
           m_b_fgate, m_norm_gain, m_skip, w_out, final_gain):
    bsz, seq, d = x.shape
    depth = w_in.shape[0]
    assert d == D_MODEL and seq == SEQ

    h = x.reshape(bsz, N_SEG, SUBS_PER_SEG, SUB, d)
    mod_all = _modulation(c, w_mod, b_mod).reshape(depth, bsz, 1, 3 * d)
    rows = lambda v: v.reshape(depth, 1, -1)
    cols = lambda v: v.reshape(depth, -1, 1)
    qperm = jnp.asarray(_lane_permutation(), BF16)
    qperm_t = jnp.asarray(_lane_permutation().T, BF16)

    gain_all = rows(norm_gain)
    w_in_b = w_in.astype(BF16)
    w_ssm_t = w_in_b[:, :, 0:2 * d].transpose(0, 2, 1)
    toept, ws, wot, atab, segtab = _s5_prep(ssm_lambda_re, ssm_lambda_im, ssm_log_dt,
                                            ssm_b_re, ssm_b_im, ssm_c_re, ssm_c_im, ssm_d)
    wq, wk, wv = m_wq.astype(BF16), m_wk.astype(BF16), m_wv.astype(BF16)
    gate_pad = ((0, 0), (0, 0), (0, 128 - 2 * HEADS))
    wg = jnp.pad(m_w_gates, gate_pad).astype(BF16)
    bg = jnp.pad(jnp.concatenate([m_b_igate, m_b_fgate], axis=-1).reshape(depth, 1, 2 * HEADS),
                 gate_pad)
    wglut = ssm_w_glu.astype(BF16).transpose(0, 2, 1)
    wout = w_out.astype(BF16)
    fg = final_gain.reshape(1, d)

    for l in range(depth):
        final = l == depth - 1
        if l == 0:
            zt, gt, h = _s5_in(h, mod_all, gain_all, w_ssm_t, qperm, l, natural=True)
        else:
            zt, gt = _s5_in(h, mod_all, gain_all, w_ssm_t, qperm, l, natural=False)
        yt = _s5_core(zt, toept, ws, wot, atab, segtab, bsz, l)
        my = _mlstm(h, mod_all, gain_all, w_in_b, m_conv_w, rows(m_conv_b), wq, wk, wv, wg, bg,
                    rows(m_norm_gain), rows(m_skip), l)
        h = _out_stage(yt, gt, my, h, mod_all, cols(ssm_b_glu), cols(ssm_out_gain), fg,
                       wglut, wout, qperm_t, l, final)

    return h.reshape(bsz, seq, d)
```
